```python
import jax, jax.numpy as jnp
from jax import lax
import numpy as np

D_MODEL = 2048
BATCH = 2
SEQ = 4096
DEPTH = 2

GRID_W = 64
CTX_LEN = 256

D_MIX = D_MODEL
D_CONV = D_MIX // 4
D_LRU = D_MIX // 4
HEAD_DIM = 128
N_Q_HEADS = (D_MIX // 2) // HEAD_DIM
N_KV_HEADS = 2
GQA_GROUP = N_Q_HEADS // N_KV_HEADS
D_ATTN = N_Q_HEADS * HEAD_DIM
D_KV = N_KV_HEADS * HEAD_DIM

CONV_WIDTH = 31
LRU_CONV_WIDTH = 4
LRU_BLOCKS = 8
LRU_BLOCK = D_LRU // LRU_BLOCKS
LRU_C = 8.0
WINDOW = 128
Q_BLOCK = 128
ROPE_BASE = 10000.0
EPS = 1e-6

OFF_LRU_X = 0
OFF_K = OFF_LRU_X + D_LRU
OFF_V = OFF_K + D_KV
OFF_MEM_END = OFF_V + D_KV
OFF_CONV_GLU = OFF_MEM_END
OFF_CONV_GATE = OFF_CONV_GLU + 2 * D_CONV
OFF_LRU_GATE = OFF_CONV_GATE + D_CONV
OFF_Q = OFF_LRU_GATE + D_LRU
OFF_ATTN_GATE = OFF_Q + D_ATTN
D_IN = OFF_ATTN_GATE + D_ATTN

kernel_name = "hybrid_conv_rglru_swa_prefix_dit"


def rmsnorm(x, g):
    xf = x.astype(jnp.float32)
    y = xf * lax.rsqrt(jnp.mean(xf * xf, axis=-1, keepdims=True) + EPS)
    return (y * g.astype(jnp.float32)).astype(x.dtype)


def layernorm(x, g, b):
    xf = x.astype(jnp.float32)
    mu = jnp.mean(xf, axis=-1, keepdims=True)
    var = jnp.mean(jnp.square(xf - mu), axis=-1, keepdims=True)
    y = (xf - mu) * lax.rsqrt(var + EPS)
    return (y * g.astype(jnp.float32) + b.astype(jnp.float32)).astype(x.dtype)


def modulate(x, norm_g, ada):
    shift, scale, gate = jnp.split(ada, 3, axis=-1)
    return rmsnorm(x, norm_g) * (1 + scale) + shift, gate


def depthwise_conv(x, w, b, pad):
    y = lax.conv_general_dilated(x, w[:, None, :].astype(x.dtype), window_strides=(1,), padding=[pad],
                                 dimension_numbers=("NWC", "WIO", "NWC"), feature_group_count=x.shape[-1])
    return y + b


def axial_rope_tables(L):
    rows = L // GRID_W
    row = jnp.repeat(jnp.arange(rows, dtype=jnp.float32), GRID_W)
    col = jnp.tile(jnp.arange(GRID_W, dtype=jnp.float32), rows)
    half = HEAD_DIM // 2
    inv = ROPE_BASE ** (-jnp.arange(0, half, 2, dtype=jnp.float32) / half)
    ang_r = row[:, None] * inv[None, :]
    ang_c = col[:, None] * inv[None, :]
    ang = jnp.concatenate([ang_r, ang_r, ang_c, ang_c], axis=-1)
    return jnp.cos(ang), jnp.sin(ang)


def _rotate_half(t):
    t1, t2 = jnp.split(t, 2, axis=-1)
    return jnp.concatenate([-t2, t1], axis=-1)


def apply_axial_rope(x, cos, sin):
    xf = x.astype(jnp.float32)
    xr, xc = jnp.split(xf, 2, axis=-1)
    xrot = jnp.concatenate([_rotate_half(xr), _rotate_half(xc)], axis=-1)
    return (xf * cos[:, None, :] + xrot * sin[:, None, :]).astype(x.dtype)


def sink_softmax(logits, sink_b):
    full = jnp.concatenate([logits, jnp.broadcast_to(sink_b, logits.shape[:-1] + (1,))], axis=-1)
    return jax.nn.softmax(full, axis=-1)[..., :-1]


def conv_branch(u_glu, u_gate, dw_w, dw_b, ln_g, ln_b, pw_w, pw_b):
    a, b = jnp.split(u_glu, 2, axis=-1)
    y = a * jax.nn.sigmoid(b)
    pad = (CONV_WIDTH - 1) // 2
    y = depthwise_conv(y, dw_w, dw_b, (pad, pad))
    y = jax.nn.silu(layernorm(y, ln_g, ln_b))
    y = y @ pw_w + pw_b
    return y * jax.nn.silu(u_gate)


def rglru_scan(x, h0, conv_w, conv_b, w_r, b_r, w_i, b_i, lam):
    xc = depthwise_conv(x, conv_w, conv_b, (LRU_CONV_WIDTH - 1, 0))
    xb = xc.reshape(xc.shape[:-1] + (LRU_BLOCKS, LRU_BLOCK))
    r = jax.nn.sigmoid(jnp.einsum("blhi,hij->blhj", xb, w_r).reshape(xc.shape) + b_r)
    i = jax.nn.sigmoid(jnp.einsum("blhi,hij->blhj", xb, w_i).reshape(xc.shape) + b_i)
    log_a = -LRU_C * r.astype(jnp.float32) * jax.nn.softplus(-lam.astype(jnp.float32))
    a = jnp.exp(log_a)
    bx = jnp.sqrt(-jnp.expm1(2.0 * log_a)) * (i * xc).astype(jnp.float32)

    def combine(p, q):
        a1, b1 = p
        a2, b2 = q
        return a1 * a2, a2 * b1 + b2

    a_cum, b_cum = lax.associative_scan(combine, (a, bx), axis=1)
    h = b_cum if h0 is None else a_cum * h0[:, None, :] + b_cum
    return h, h[:, -1]


def rglru_direction(x_lat, x_ctx, conv_w, conv_b, w_r, b_r, w_i, b_i, lam, reverse):
    if reverse:
        x_lat, x_ctx = jnp.flip(x_lat, 1), jnp.flip(x_ctx, 1)
    h_ctx, h_last = rglru_scan(x_ctx, None, conv_w, conv_b, w_r, b_r, w_i, b_i, lam)
    h_lat, _ = rglru_scan(x_lat, h_last, conv_w, conv_b, w_r, b_r, w_i, b_i, lam)
    if reverse:
        h_lat, h_ctx = jnp.flip(h_lat, 1), jnp.flip(h_ctx, 1)
    return h_lat.astype(x_lat.dtype), h_ctx.astype(x_ctx.dtype)


def window_attention(q, k, v, k_ctx, v_ctx, sink):
    B, L = q.shape[:2]
    nb = L // Q_BLOCK
    qb = q.reshape(B, nb, Q_BLOCK, N_KV_HEADS, GQA_GROUP, HEAD_DIM)

    def bands(t):
        tp = jnp.pad(t, ((0, 0), (Q_BLOCK, Q_BLOCK), (0, 0), (0, 0)))
        tp = tp.reshape(B, nb + 2, Q_BLOCK, N_KV_HEADS, HEAD_DIM)
        return jnp.concatenate([tp[:, :-2], tp[:, 1:-1], tp[:, 2:]], axis=2)

    kb, vb = bands(k), bands(v)
    scale = HEAD_DIM ** -0.5
    s_loc = jnp.einsum("bnqkgd,bnskd->bnkgqs", qb, kb, preferred_element_type=jnp.float32) * scale
    s_ctx = jnp.einsum("bnqkgd,bskd->bnkgqs", qb, k_ctx, preferred_element_type=jnp.float32) * scale
    qi = jnp.arange(Q_BLOCK)[:, None]
    kj = jnp.arange(3 * Q_BLOCK)[None, :]
    rel = kj - Q_BLOCK - qi
    kpos = jnp.arange(nb)[:, None, None] * Q_BLOCK - Q_BLOCK + kj[None]
    valid = (jnp.abs(rel) <= WINDOW)[None] & (kpos >= 0) & (kpos < L)
    s_loc = jnp.where(valid[None, :, None, None], s_loc, -jnp.inf)
    sink_b = sink.astype(jnp.float32).reshape(N_KV_HEADS, GQA_GROUP, 1, 1)
    p = sink_softmax(jnp.concatenate([s_loc, s_ctx], axis=-1), sink_b)
    p_loc = p[..., :3 * Q_BLOCK].astype(v.dtype)
    p_ctx = p[..., 3 * Q_BLOCK:].astype(v.dtype)
    out = jnp.einsum("bnkgqs,bnskd->bnqkgd", p_loc, vb) + jnp.einsum("bnkgqs,bskd->bnqkgd", p_ctx, v_ctx)
    return out.reshape(B, L, D_ATTN).astype(q.dtype)


def context_attention(q_ctx, k_ctx, v_ctx, sink):
    B, Lc = q_ctx.shape[:2]
    qg = q_ctx.reshape(B, Lc, N_KV_HEADS, GQA_GROUP, HEAD_DIM)
    s = jnp.einsum("bqkgd,bskd->bkgqs", qg, k_ctx, preferred_element_type=jnp.float32) * HEAD_DIM ** -0.5
    p = sink_softmax(s, sink.astype(jnp.float32).reshape(N_KV_HEADS, GQA_GROUP, 1, 1)).astype(v_ctx.dtype)
    return jnp.einsum("bkgqs,bskd->bqkgd", p, v_ctx).reshape(B, Lc, D_ATTN)


def mixer_layer(x, xc, c_act, cc_act, norm_g, w_ada, b_ada, w_in, dw_w, dw_b, ln_g, ln_b, pw_w, pw_b,
                lru_conv_w, lru_conv_b, lru_w_r, lru_b_r, lru_w_i, lru_b_i, lru_lam, sink, w_out,
                cos, sin, last):
    B, L, _ = x.shape
    Lc = xc.shape[1]
    h, gate = modulate(x, norm_g, (c_act @ w_ada + b_ada)[:, None, :])
    hc, gate_c = modulate(xc, norm_g, cc_act @ w_ada + b_ada)
    u = h @ w_in
    uc = hc @ (w_in[:, :OFF_MEM_END] if last else w_in)

    q = apply_axial_rope(u[..., OFF_Q:OFF_ATTN_GATE].reshape(B, L, N_Q_HEADS, HEAD_DIM), cos, sin)
    k = apply_axial_rope(u[..., OFF_K:OFF_V].reshape(B, L, N_KV_HEADS, HEAD_DIM), cos, sin)
    v = u[..., OFF_V:OFF_MEM_END].reshape(B, L, N_KV_HEADS, HEAD_DIM)
    k_ctx = uc[..., OFF_K:OFF_V].reshape(B, Lc, N_KV_HEADS, HEAD_DIM)
    v_ctx = uc[..., OFF_V:OFF_MEM_END].reshape(B, Lc, N_KV_HEADS, HEAD_DIM)

    y_lru, y_lru_c = None, None
    for d in range(2):
        hl, hcx = rglru_direction(u[..., OFF_LRU_X:OFF_K], uc[..., OFF_LRU_X:OFF_K],
                                  lru_conv_w[d], lru_conv_b[d], lru_w_r[d], lru_b_r[d],
                                  lru_w_i[d], lru_b_i[d], lru_lam[d], reverse=(d == 1))
        y_lru = hl if y_lru is None else y_lru + hl
        y_lru_c = hcx if y_lru_c is None else y_lru_c + hcx

    y_conv = conv_branch(u[..., OFF_CONV_GLU:OFF_CONV_GATE], u[..., OFF_CONV_GATE:OFF_LRU_GATE],
                         dw_w, dw_b, ln_g, ln_b, pw_w, pw_b)
    y_lru = y_lru * jax.nn.silu(u[..., OFF_LRU_GATE:OFF_Q])
    y_attn = window_attention(q, k, v, k_ctx, v_ctx, sink) * jax.nn.silu(u[..., OFF_ATTN_GATE:D_IN])
    x = x + gate * (jnp.concatenate([y_conv, y_lru, y_attn], axis=-1) @ w_out)

    if not last:
        q_ctx = uc[..., OFF_Q:OFF_ATTN_GATE].reshape(B, Lc, N_Q_HEADS, HEAD_DIM)
        yc_conv = conv_branch(uc[..., OFF_CONV_GLU:OFF_CONV_GATE], uc[..., OFF_CONV_GATE:OFF_LRU_GATE],
                              dw_w, dw_b, ln_g, ln_b, pw_w, pw_b)
        yc_lru = y_lru_c * jax.nn.silu(uc[..., OFF_LRU_GATE:OFF_Q])
        yc_attn = context_attention(q_ctx, k_ctx, v_ctx, sink) * jax.nn.silu(uc[..., OFF_ATTN_GATE:D_IN])
        xc = xc + gate_c * (jnp.concatenate([yc_conv, yc_lru, yc_attn], axis=-1) @ w_out)
    return x, xc


def setup_inputs(seed: int = 0) -> dict:
    key = jax.random.key(seed)
    ks = jax.random.split(key, 24)
    f32 = jnp.float32

    def nrm(k, shape, s):
        return jax.random.normal(k, shape, f32) * s

    a0 = jax.random.uniform(ks[20], (DEPTH, 2, D_LRU), f32, 0.9, 0.999)
    return {
        "x": nrm(ks[0], (BATCH, SEQ, D_MODEL), 1.0),
        "c": nrm(ks[1], (BATCH, D_MODEL), 1.0),
        "ctx": nrm(ks[2], (BATCH, CTX_LEN, D_MODEL), 1.0),
        "c_ctx": nrm(ks[3], (D_MODEL,), 1.0),
        "norm_g": 1.0 + nrm(ks[4], (DEPTH, D_MODEL), 0.02),
        "w_ada": nrm(ks[5], (DEPTH, D_MODEL, 3 * D_MODEL), 0.5 * D_MODEL ** -0.5),
        "b_ada": nrm(ks[6], (DEPTH, 3 * D_MODEL), 0.02),
        "w_in": nrm(ks[7], (DEPTH, D_MODEL, D_IN), D_MODEL ** -0.5),
        "dw_w": nrm(ks[8], (DEPTH, CONV_WIDTH, D_CONV), CONV_WIDTH ** -0.5),
        "dw_b": nrm(ks[9], (DEPTH, D_CONV), 0.02),
        "ln_g": 1.0 + nrm(ks[10], (DEPTH, D_CONV), 0.02),
        "ln_b": nrm(ks[11], (DEPTH, D_CONV), 0.02),
        "pw_w": nrm(ks[12], (DEPTH, D_CONV, D_CONV), D_CONV ** -0.5),
        "pw_b": nrm(ks[13], (DEPTH, D_CONV), 0.02),
        "lru_conv_w": nrm(ks[14], (DEPTH, 2, LRU_CONV_WIDTH, D_LRU), LRU_CONV_WIDTH ** -0.5),
        "lru_conv_b": nrm(ks[15], (DEPTH, 2, D_LRU), 0.02),
        "lru_w_r": nrm(ks[16], (DEPTH, 2, LRU_BLOCKS, LRU_BLOCK, LRU_BLOCK), LRU_BLOCK ** -0.5),
        "lru_b_r": nrm(ks[17], (DEPTH, 2, D_LRU), 0.02),
        "lru_w_i": nrm(ks[18], (DEPTH, 2, LRU_BLOCKS, LRU_BLOCK, LRU_BLOCK), LRU_BLOCK ** -0.5),
        "lru_b_i": nrm(ks[19], (DEPTH, 2, D_LRU), 0.02),
        "lru_lam": jnp.log(a0) - jnp.log1p(-a0),
        "attn_sink": nrm(ks[21], (DEPTH, N_Q_HEADS), 1.0),
        "w_out": nrm(ks[22], (DEPTH, D_MIX, D_MODEL), D_MIX ** -0.5),
        "final_g": 1.0 + nrm(ks[23], (D_MODEL,), 0.02),
    }


def reference(x, c, ctx, c_ctx, norm_g, w_ada, b_ada, w_in, dw_w, dw_b, ln_g, ln_b, pw_w, pw_b,
              lru_conv_w, lru_conv_b, lru_w_r, lru_b_r, lru_w_i, lru_b_i, lru_lam, attn_sink,
              w_out, final_g):
    cos, sin = axial_rope_tables(x.shape[1])
    c_act = jax.nn.silu(c)
    cc_act = jax.nn.silu(c_ctx)
    xc = ctx
    for l in range(DEPTH):
        x, xc = mixer_layer(x, xc, c_act, cc_act, norm_g[l], w_ada[l], b_ada[l], w_in[l],
                            dw_w[l], dw_b[l], ln_g[l], ln_b[l], pw_w[l], pw_b[l],
                            lru_conv_w[l], lru_conv_b[l], lru_w_r[l], lru_b_r[l],
                            lru_w_i[l], lru_b_i[l], lru_lam[l], attn_sink[l], w_out[l],
                            cos, sin, last=(l == DEPTH - 1))
    return rmsnorm(x, final_g)
```

```python
import functools

import jax
import jax.numpy as jnp
from jax import lax
from jax.experimental import pallas as pl
from jax.experimental.pallas import tpu as pltpu

F32 = jnp.float32
BF16 = jnp.bfloat16

D_MODEL = 2048
D_CONV = 512
D_LRU = 512
HEAD_DIM = 128
N_Q_HEADS = 8
N_KV_HEADS = 2
GQA_GROUP = N_Q_HEADS // N_KV_HEADS
D_ATTN = N_Q_HEADS * HEAD_DIM
D_KV = N_KV_HEADS * HEAD_DIM
GRID_W = 64
CONV_WIDTH = 31
CONV_PAD = (CONV_WIDTH - 1) // 2
LRU_CONV_WIDTH = 4
LRU_BLOCKS = 8
LRU_C = 8.0
WINDOW = 128
ROPE_BASE = 10000.0
EPS = 1e-6

OFF_K = D_LRU
OFF_V = OFF_K + D_KV
OFF_MEM_END = OFF_V + D_KV
OFF_CONV_GLU = OFF_MEM_END
OFF_CONV_GATE = OFF_CONV_GLU + 2 * D_CONV
OFF_LRU_GATE = OFF_CONV_GATE + D_CONV
OFF_Q = OFF_LRU_GATE + D_LRU
OFF_ATTN_GATE = OFF_Q + D_ATTN
D_IN = OFF_ATTN_GATE + D_ATTN

ADA_ROWS = 8
IN_TILE_N = 512
CONV_HALO = 16
LRU_HALO = 8
SUBLANES = 8
VMEM_LIMIT = 56 * 1024 * 1024


def _silu(x):
    return x * jax.nn.sigmoid(x)


def _ada_kernel(c_ref, w_ref, b_ref, o_ref):
    ca = _silu(c_ref[...])
    o_ref[...] = jnp.dot(ca.astype(BF16), w_ref[...].astype(BF16),
                         preferred_element_type=F32) + b_ref[...]


def _ada_call(c_rows, w_ada, b_ada):
    depth, d, n = w_ada.shape
    tn = 1024
    return pl.pallas_call(
        _ada_kernel,
        grid=(depth, n // tn),
        in_specs=[
            pl.BlockSpec((ADA_ROWS, d), lambda l, j: (0, 0)),
            pl.BlockSpec((None, d, tn), lambda l, j: (l, 0, j)),
            pl.BlockSpec((None, 1, tn), lambda l, j: (l, 0, j)),
        ],
        out_specs=pl.BlockSpec((None, ADA_ROWS, tn), lambda l, j: (l, 0, j)),
        out_shape=jax.ShapeDtypeStruct((depth, ADA_ROWS, n), F32),
        compiler_params=pltpu.CompilerParams(
            dimension_semantics=("arbitrary", "arbitrary"), vmem_limit_bytes=VMEM_LIMIT),
        name="ada_proj",
    )(c_rows, w_ada, b_ada.reshape(depth, 1, n))


def _rope(t, cos, sin_a, sin_b):
    return (t * cos + pltpu.roll(t, HEAD_DIM - 32, 1) * sin_a + pltpu.roll(t, 32, 1) * sin_b)


def _inproj_kernel(*refs, rope):
    if rope:
        x_ref, g_ref, shift_ref, scale_ref, w_ref, cos_ref, sa_ref, sb_ref, o_ref, h_ref = refs
    else:
        x_ref, g_ref, shift_ref, scale_ref, w_ref, o_ref, h_ref = refs
    j = pl.program_id(1)

    @pl.when(j == 0)
    def _():
        x = x_ref[...]
        ms = jnp.mean(x * x, axis=-1, keepdims=True)
        y = x * lax.rsqrt(ms + EPS) * g_ref[...]
        h_ref[...] = (y * (1.0 + scale_ref[...]) + shift_ref[...]).astype(BF16)

    u = jnp.dot(h_ref[...], w_ref[...], preferred_element_type=F32)
    if not rope:
        o_ref[...] = u
        return

    k_tile = OFF_K // IN_TILE_N
    q_tile0 = OFF_Q // IN_TILE_N
    q_tile1 = OFF_ATTN_GATE // IN_TILE_N
    is_k = j == k_tile
    is_q = (j >= q_tile0) & (j < q_tile1)

    @pl.when(is_k)
    def _():
        cos, sa, sb = cos_ref[...], sa_ref[...], sb_ref[...]
        for h in range(N_KV_HEADS):
            sl = slice(h * HEAD_DIM, (h + 1) * HEAD_DIM)
            o_ref[:, sl] = _rope(u[:, sl], cos, sa, sb)
        o_ref[:, D_KV:] = u[:, D_KV:]

    @pl.when(is_q)
    def _():
        cos, sa, sb = cos_ref[...], sa_ref[...], sb_ref[...]
        scale = HEAD_DIM ** -0.5
        for h in range(IN_TILE_N // HEAD_DIM):
            sl = slice(h * HEAD_DIM, (h + 1) * HEAD_DIM)
            o_ref[:, sl] = _rope(u[:, sl], cos, sa, sb) * scale

    @pl.when(jnp.logical_not(is_k | is_q))
    def _():
        o_ref[...] = u


def _inproj_call(x2, norm_g, ada4, layer, w_bf, rope_tabs, *, tm, rows_per_batch, ada_row_fn,
                 n_cols):
    m, d = x2.shape
    tiles_per_batch = rows_per_batch // tm
    rope = rope_tabs is not None

    def ada_spec(part):
        return pl.BlockSpec((None, None, 1, d),
                            lambda i, j: (layer, ada_row_fn(i // tiles_per_batch), 0, part))

    in_specs = [
        pl.BlockSpec((tm, d), lambda i, j: (i, 0)),
        pl.BlockSpec((1, d), lambda i, j: (0, 0)),
        ada_spec(0), ada_spec(1),
        pl.BlockSpec((d, IN_TILE_N), lambda i, j: (0, j)),
    ]
    args = [x2, norm_g.reshape(1, d), ada4, ada4, w_bf]
    if rope:
        tab_spec = pl.BlockSpec((tm, HEAD_DIM), lambda i, j: (i % tiles_per_batch, 0))
        in_specs += [tab_spec] * 3
        args += list(rope_tabs)
    return pl.pallas_call(
        functools.partial(_inproj_kernel, rope=rope),
        grid=(m // tm, n_cols // IN_TILE_N),
        in_specs=in_specs,
        out_specs=pl.BlockSpec((tm, IN_TILE_N), lambda i, j: (i, j)),
        out_shape=jax.ShapeDtypeStruct((m, n_cols), F32),
        scratch_shapes=[pltpu.VMEM((tm, d), BF16)],
        compiler_params=pltpu.CompilerParams(
            dimension_semantics=("arbitrary", "arbitrary"), vmem_limit_bytes=VMEM_LIMIT),
        name="in_proj",
    )(*args)


def _lru_kernel(*refs, reverse, has_acc, tt):
    if has_acc:
        (x_ref, h0_ref, acc_ref, cw_ref, cb_ref, wg_ref, br_ref, bi_ref, lam_ref,
         o_ref, ext_ref, a_ref, b_ref, h_ref) = refs
    else:
        (x_ref, h0_ref, cw_ref, cb_ref, wg_ref, br_ref, bi_ref, lam_ref,
         o_ref, ext_ref, a_ref, b_ref, h_ref) = refs
        acc_ref = None
    c = pl.program_id(1)
    carry_rows = slice(LRU_HALO + tt, 2 * LRU_HALO + tt) if reverse else slice(0, LRU_HALO)

    @pl.when(c == 0)
    def _():
        ext_ref[carry_rows, :] = jnp.zeros((LRU_HALO, D_LRU), F32)
        h_ref[...] = jnp.broadcast_to(h0_ref[...], (SUBLANES, D_LRU))

    x = x_ref[...]
    ext_ref[LRU_HALO:LRU_HALO + tt, :] = x
    xc = jnp.broadcast_to(cb_ref[...], (tt, D_LRU))
    for k in range(LRU_CONV_WIDTH):
        off = (LRU_CONV_WIDTH - 1 - k) if reverse else (k - (LRU_CONV_WIDTH - 1))
        xc = xc + cw_ref[k:k + 1, :] * ext_ref[pl.ds(LRU_HALO + off, tt), :]
    if reverse:
        ext_ref[carry_rows, :] = x[:LRU_HALO, :]
    else:
        ext_ref[carry_rows, :] = x[tt - LRU_HALO:, :]

    g = jnp.dot(xc.astype(BF16), wg_ref[...], preferred_element_type=F32)
    r = jax.nn.sigmoid(g[:, :D_LRU] + br_ref[...])
    gi = jax.nn.sigmoid(g[:, D_LRU:] + bi_ref[...])
    lam = lam_ref[...]
    softplus_neg_lam = jnp.maximum(-lam, 0.0) + jnp.log1p(jnp.exp(-jnp.abs(lam)))
    log_a = (-LRU_C) * r * softplus_neg_lam
    a_ref[...] = jnp.exp(log_a)
    th = jnp.tanh(log_a)
    one_minus_a2 = (-2.0 * th) / (1.0 - th)
    b_ref[...] = jnp.sqrt(one_minus_a2) * (gi * xc)

    row = lax.broadcasted_iota(jnp.int32, (SUBLANES, D_LRU), 0)
    n_groups = tt // SUBLANES

    def group(gidx, h_prev):
        gi_ = (n_groups - 1 - gidx) if reverse else gidx
        rows = pl.ds(pl.multiple_of(gi_ * SUBLANES, SUBLANES), SUBLANES)
        a = a_ref[rows, :]
        b = b_ref[rows, :]
        for d in (1, 2, 4):
            if reverse:
                keep = row < SUBLANES - d
                shift = SUBLANES - d
            else:
                keep = row >= d
                shift = d
            a_s = jnp.where(keep, pltpu.roll(a, shift, 0), 1.0)
            b_s = jnp.where(keep, pltpu.roll(b, shift, 0), 0.0)
            b = a * b_s + b
            a = a * a_s
        h = a * h_prev + b
        if has_acc:
            o_ref[rows, :] = acc_ref[rows, :] + h
        else:
            o_ref[rows, :] = h
        edge = h[0:1, :] if reverse else h[SUBLANES - 1:SUBLANES, :]
        return jnp.broadcast_to(edge, (SUBLANES, D_LRU))

    h_ref[...] = lax.fori_loop(0, n_groups, group, h_ref[...])


def _lru_call(u2, h0, acc, params, *, reverse, rows_per_batch, tt):
    cw, cb, wg, br, bi, lam = params
    m = u2.shape[0]
    nb = m // rows_per_batch
    nc = rows_per_batch // tt
    has_acc = acc is not None

    def chunk(c):
        return (nc - 1 - c) if reverse else c

    row_spec = pl.BlockSpec((tt, D_LRU), lambda b, c: (b * nc + chunk(c), 0))
    full = lambda shape: pl.BlockSpec(shape, lambda b, c: (0,) * len(shape))
    in_specs = [row_spec, pl.BlockSpec((None, 1, D_LRU), lambda b, c: (b, 0, 0))]
    args = [u2, h0]
    if has_acc:
        in_specs.append(row_spec)
        args.append(acc)
    in_specs += [full((LRU_CONV_WIDTH, D_LRU)), full((1, D_LRU)), full((D_LRU, 2 * D_LRU)),
                 full((1, D_LRU)), full((1, D_LRU)), full((1, D_LRU))]
    args += [cw, cb, wg, br, bi, lam]
    return pl.pallas_call(
        functools.partial(_lru_kernel, reverse=reverse, has_acc=has_acc, tt=tt),
        grid=(nb, nc),
        in_specs=in_specs,
        out_specs=row_spec,
        out_shape=jax.ShapeDtypeStruct((m, D_LRU), F32),
        scratch_shapes=[pltpu.VMEM((tt + 2 * LRU_HALO, D_LRU), F32),
                        pltpu.VMEM((tt, D_LRU), F32),
                        pltpu.VMEM((tt, D_LRU), F32),
                        pltpu.VMEM((SUBLANES, D_LRU), F32)],
        compiler_params=pltpu.CompilerParams(
            dimension_semantics=("arbitrary", "arbitrary"), vmem_limit_bytes=VMEM_LIMIT),
        name="rglru_rev" if reverse else "rglru_fwd",
    )(*args)


def _mix_kernel(*refs, names, tq, local, final):
    r = dict(zip(names, refs))
    i = pl.program_id(1)
    nt = pl.num_programs(1)
    ext_ref, ycat_ref = r["ext"], r["ycat"]

    def glu(ref):
        t = ref[...]
        return t[:, :D_CONV] * jax.nn.sigmoid(t[:, D_CONV:])

    ext_ref[CONV_HALO:CONV_HALO + tq, :] = glu(r["glu"])
    zero_halo = jnp.zeros((CONV_HALO, D_CONV), F32)
    if local:
        ext_ref[0:CONV_HALO, :] = jnp.where(i > 0, glu(r["glu_prev"]), zero_halo)
        ext_ref[CONV_HALO + tq:, :] = jnp.where(i < nt - 1, glu(r["glu_next"]), zero_halo)
    else:
        ext_ref[0:CONV_HALO, :] = zero_halo
        ext_ref[CONV_HALO + tq:, :] = zero_halo

    conv_rows = 64
    for rc in range(tq // conv_rows):
        acc = jnp.broadcast_to(r["dw_b"][...], (conv_rows, D_CONV))
        base = CONV_HALO - CONV_PAD + rc * conv_rows
        for k in range(CONV_WIDTH):
            acc = acc + r["dw_w"][k:k + 1, :] * ext_ref[pl.ds(base + k, conv_rows), :]
        mu = jnp.mean(acc, axis=-1, keepdims=True)
        cen = acc - mu
        var = jnp.mean(cen * cen, axis=-1, keepdims=True)
        y = cen * lax.rsqrt(var + EPS) * r["ln_g"][...] + r["ln_b"][...]
        y = _silu(y).astype(BF16)
        y = jnp.dot(y, r["pw_w"][...], preferred_element_type=F32) + r["pw_b"][...]
        rows = slice(rc * conv_rows, (rc + 1) * conv_rows)
        y = y * _silu(r["conv_gate"][rows, :])
        ycat_ref[rows, 0:D_CONV] = y.astype(BF16)

    ylru = r["ylru"][...]
    if "ylru_rev" in r:
        ylru = ylru + r["ylru_rev"][...]
    ycat_ref[:, D_CONV:D_CONV + D_LRU] = (ylru * _silu(r["lru_gate"][...])).astype(BF16)

    qrows = GQA_GROUP * WINDOW
    qi = lax.broadcasted_iota(jnp.int32, (qrows, WINDOW), 0) % WINDOW
    kj = lax.broadcasted_iota(jnp.int32, (qrows, WINDOW), 1)
    neg_inf = jnp.float32(-jnp.inf)
    dn = (((1,), (1,)), ((), ()))
    n_qb = tq // WINDOW
    for qb in range(n_qb):
        rows = slice(qb * WINDOW, (qb + 1) * WINDOW)
        for g in range(N_KV_HEADS):
            kv_cols = slice(g * HEAD_DIM, (g + 1) * HEAD_DIM)
            q_st = jnp.concatenate(
                [r["q"][rows, (g * GQA_GROUP + hh) * HEAD_DIM:(g * GQA_GROUP + hh + 1) * HEAD_DIM]
                 for hh in range(GQA_GROUP)], axis=0)
            if not local:
                q_st = q_st * (HEAD_DIM ** -0.5)
            q_st = q_st.astype(BF16)
            sink = jnp.concatenate(
                [jnp.broadcast_to(r["sink"][g * GQA_GROUP + hh:g * GQA_GROUP + hh + 1, 0:1],
                                  (WINDOW, 1)) for hh in range(GQA_GROUP)], axis=0)
            kc = r["kc"][:, kv_cols].astype(BF16)
            vc = r["vc"][:, kv_cols].astype(BF16)
            s_ctx = lax.dot_general(q_st, kc, dn, preferred_element_type=F32)
            m = jnp.maximum(jnp.max(s_ctx, axis=-1, keepdims=True), sink)
            if local:
                if qb == 0:
                    k_p, v_p = r["k_prev"][:, kv_cols], r["v_prev"][:, kv_cols]
                    ok_p = i > 0
                else:
                    prows = slice((qb - 1) * WINDOW, qb * WINDOW)
                    k_p, v_p = r["k"][prows, kv_cols], r["v"][prows, kv_cols]
                    ok_p = True
                if qb == n_qb - 1:
                    k_n, v_n = r["k_next"][:, kv_cols], r["v_next"][:, kv_cols]
                    ok_n = i < nt - 1
                else:
                    nrows = slice((qb + 1) * WINDOW, (qb + 2) * WINDOW)
                    k_n, v_n = r["k"][nrows, kv_cols], r["v"][nrows, kv_cols]
                    ok_n = True
                k_c, v_c = r["k"][rows, kv_cols], r["v"][rows, kv_cols]
                s_p = lax.dot_general(q_st, k_p.astype(BF16), dn, preferred_element_type=F32)
                s_c = lax.dot_general(q_st, k_c.astype(BF16), dn, preferred_element_type=F32)
                s_n = lax.dot_general(q_st, k_n.astype(BF16), dn, preferred_element_type=F32)
                s_p = jnp.where((kj >= qi) & ok_p, s_p, neg_inf)
                s_n = jnp.where((kj <= qi) & ok_n, s_n, neg_inf)
                m = jnp.maximum(m, jnp.max(s_p, axis=-1, keepdims=True))
                m = jnp.maximum(m, jnp.max(s_c, axis=-1, keepdims=True))
                m = jnp.maximum(m, jnp.max(s_n, axis=-1, keepdims=True))
            p_ctx = jnp.exp(s_ctx - m)
            den = jnp.sum(p_ctx, axis=-1, keepdims=True) + jnp.exp(sink - m)
            o = jnp.dot(p_ctx.astype(BF16), vc, preferred_element_type=F32)
            if local:
                for s_x, v_x in ((s_p, v_p), (s_c, v_c), (s_n, v_n)):
                    p_x = jnp.exp(s_x - m)
                    den = den + jnp.sum(p_x, axis=-1, keepdims=True)
                    o = o + jnp.dot(p_x.astype(BF16), v_x.astype(BF16),
                                    preferred_element_type=F32)
            o = o / den
            for hh in range(GQA_GROUP):
                cols = slice((g * GQA_GROUP + hh) * HEAD_DIM, (g * GQA_GROUP + hh + 1) * HEAD_DIM)
                gate = _silu(r["attn_gate"][rows, cols])
                ycat_ref[rows, D_CONV + D_LRU + cols.start:D_CONV + D_LRU + cols.stop] = (
                    o[hh * WINDOW:(hh + 1) * WINDOW, :] * gate).astype(BF16)

    y = jnp.dot(ycat_ref[...], r["w_out"][...], preferred_element_type=F32)
    xn = r["x"][...] + r["gate"][...] * y
    if final:
        ms = jnp.mean(xn * xn, axis=-1, keepdims=True)
        xn = xn * lax.rsqrt(ms + EPS) * r["final_g"][...]
    r["out"][...] = xn


def _mix_call(x2, u2, uc2, ylru, ada4, layer, ada_row_fn, wts, *, tq, rows_per_batch, ctx_rows,
              local, final):
    m, d = x2.shape
    nt = rows_per_batch // tq
    nb = m // rows_per_batch
    names, specs, args = [], [], []

    def add(name, arr, spec):
        names.append(name)
        specs.append(spec)
        args.append(arr)

    def rowblk(width, col_off):
        return pl.BlockSpec((tq, width), lambda b, i: (b * nt + i, col_off // width))

    def halo(rows, width, col_off, nxt):
        per = tq // rows
        last = m // rows - 1
        if nxt:
            fn = lambda b, i: (jnp.minimum((b * nt + i + 1) * per, last), col_off // width)
        else:
            fn = lambda b, i: (jnp.maximum((b * nt + i) * per - 1, 0), col_off // width)
        return pl.BlockSpec((rows, width), fn)

    full = lambda shape: pl.BlockSpec(shape, lambda b, i: (0,) * len(shape),
                                      pipeline_mode=pl.Buffered(1))

    add("x", x2, pl.BlockSpec((tq, d), lambda b, i: (b * nt + i, 0)))
    add("gate", ada4, pl.BlockSpec((None, None, 1, d), lambda b, i: (layer, ada_row_fn(b), 0, 2)))
    add("glu", u2, rowblk(2 * D_CONV, OFF_CONV_GLU))
    if local:
        add("glu_prev", u2, halo(CONV_HALO, 2 * D_CONV, OFF_CONV_GLU, False))
        add("glu_next", u2, halo(CONV_HALO, 2 * D_CONV, OFF_CONV_GLU, True))
    add("conv_gate", u2, rowblk(D_CONV, OFF_CONV_GATE))
    add("lru_gate", u2, rowblk(D_LRU, OFF_LRU_GATE))
    if isinstance(ylru, tuple):
        add("ylru", ylru[0], pl.BlockSpec((tq, D_LRU), lambda b, i: (b * nt + i, 0)))
        add("ylru_rev", ylru[1], pl.BlockSpec((tq, D_LRU), lambda b, i: (b * nt + i, 0)))
    else:
        add("ylru", ylru, pl.BlockSpec((tq, D_LRU), lambda b, i: (b * nt + i, 0)))
    add("q", u2, rowblk(D_ATTN, OFF_Q))
    add("attn_gate", u2, rowblk(D_ATTN, OFF_ATTN_GATE))
    if local:
        add("k", u2, rowblk(D_KV, OFF_K))
        add("v", u2, rowblk(D_KV, OFF_V))
        add("k_prev", u2, halo(WINDOW, D_KV, OFF_K, False))
        add("v_prev", u2, halo(WINDOW, D_KV, OFF_V, False))
        add("k_next", u2, halo(WINDOW, D_KV, OFF_K, True))
        add("v_next", u2, halo(WINDOW, D_KV, OFF_V, True))
    add("kc", uc2, pl.BlockSpec((ctx_rows, D_KV), lambda b, i: (b, OFF_K // D_KV)))
    add("vc", uc2, pl.BlockSpec((ctx_rows, D_KV), lambda b, i: (b, OFF_V // D_KV)))
    for name in ("dw_w", "dw_b", "ln_g", "ln_b", "pw_w", "pw_b", "sink", "w_out"):
        add(name, wts[name], full(wts[name].shape))
    if final:
        add("final_g", wts["final_g"], full(wts["final_g"].shape))
    names += ["out", "ext", "ycat"]
    return pl.pallas_call(
        functools.partial(_mix_kernel, names=tuple(names), tq=tq, local=local, final=final),
        grid=(nb, nt),
        in_specs=specs,
        out_specs=pl.BlockSpec((tq, d), lambda b, i: (b * nt + i, 0)),
        out_shape=jax.ShapeDtypeStruct((m, d), F32),
        scratch_shapes=[pltpu.VMEM((tq + 2 * CONV_HALO, D_CONV), F32),
                        pltpu.VMEM((tq, D_MODEL), BF16)],
        compiler_params=pltpu.CompilerParams(
            dimension_semantics=("arbitrary", "arbitrary"), vmem_limit_bytes=VMEM_LIMIT),
        name="mix_lat" if local else "mix_ctx",
    )(*args)


def _rope_tables(seq):
    rows = seq // GRID_W
    row = jnp.repeat(jnp.arange(rows, dtype=F32), GRID_W)
    col = jnp.tile(jnp.arange(GRID_W, dtype=F32), rows)
    half = HEAD_DIM // 2
    inv = ROPE_BASE ** (-jnp.arange(0, half, 2, dtype=F32) / half)
    ang_r = row[:, None] * inv[None, :]
    ang_c = col[:, None] * inv[None, :]
    ang = jnp.concatenate([ang_r, ang_r, ang_c, ang_c], axis=-1)
    cos, sin = jnp.cos(ang), jnp.sin(ang)
    first = (jnp.arange(HEAD_DIM) % half) < (half // 2)
    return cos, jnp.where(first, -sin, 0.0), jnp.where(first, 0.0, sin)


def _block_diag(w):
    nblk, blk, _ = w.shape
    eye = jnp.eye(nblk, dtype=w.dtype)
    return (w[:, :, None, :] * eye[:, None, :, None]).reshape(nblk * blk, nblk * blk)


def _pick_tile(n, pref):
    t = min(n, pref)
    while n % t:
        t //= 2
    return t


def kernel(x, c, ctx, c_ctx, norm_g, w_ada, b_ada, w_in, dw_w, dw_b, ln_g, ln_b, pw_w, pw_b,
           lru_conv_w, lru_conv_b, lru_w_r, lru_b_r, lru_w_i, lru_b_i, lru_lam, attn_sink,
           w_out, final_g):
    nb, seq, d = x.shape
    lc = ctx.shape[1]
    depth = w_in.shape[0]
    ctx_row = nb

    c_rows = jnp.zeros((ADA_ROWS, d), F32).at[:nb].set(c).at[ctx_row].set(c_ctx)
    ada = _ada_call(c_rows, w_ada, b_ada)
    ada4 = ada.reshape(depth, ADA_ROWS, 1, 3 * d)
    rope_tabs = _rope_tables(seq)

    x2 = x.reshape(nb * seq, d)
    xc2 = ctx.reshape(nb * lc, d)
    tm = _pick_tile(seq, 1024)
    tq = _pick_tile(seq, 256)
    tt = _pick_tile(seq, 512)
    zeros_h0 = jnp.zeros((nb, 1, D_LRU), F32)

    for l in range(depth):
        last = l == depth - 1
        w_bf = w_in[l].astype(BF16)
        u2 = _inproj_call(x2, norm_g[l], ada4, l, w_bf, rope_tabs, tm=tm, rows_per_batch=seq,
                          ada_row_fn=lambda b: b, n_cols=D_IN)
        uc2 = _inproj_call(xc2, norm_g[l], ada4, l, w_bf, None, tm=lc, rows_per_batch=lc,
                           ada_row_fn=lambda b: ctx_row, n_cols=OFF_MEM_END if last else D_IN)

        ylru, ylru_c = None, ()
        for dr in range(2):
            rev = dr == 1
            wg = jnp.concatenate([_block_diag(lru_w_r[l, dr]), _block_diag(lru_w_i[l, dr])],
                                 axis=1).astype(BF16)
            params = (lru_conv_w[l, dr], lru_conv_b[l, dr].reshape(1, D_LRU), wg,
                      lru_b_r[l, dr].reshape(1, D_LRU), lru_b_i[l, dr].reshape(1, D_LRU),
                      lru_lam[l, dr].reshape(1, D_LRU))
            h_ctx = _lru_call(uc2, zeros_h0, None, params, reverse=rev, rows_per_batch=lc, tt=lc)
            edge = 0 if rev else lc - 1
            h0 = h_ctx.reshape(nb, lc, D_LRU)[:, edge:edge + 1, :]
            ylru = _lru_call(u2, h0, ylru, params, reverse=rev, rows_per_batch=seq, tt=tt)
            ylru_c += (h_ctx,)

        wts = {
            "dw_w": dw_w[l], "dw_b": dw_b[l].reshape(1, D_CONV),
            "ln_g": ln_g[l].reshape(1, D_CONV), "ln_b": ln_b[l].reshape(1, D_CONV),
            "pw_w": pw_w[l].astype(BF16), "pw_b": pw_b[l].reshape(1, D_CONV),
            "sink": jnp.broadcast_to(attn_sink[l].reshape(N_Q_HEADS, 1), (N_Q_HEADS, HEAD_DIM)),
            "w_out": w_out[l].astype(BF16), "final_g": final_g.reshape(1, d),
        }
        x2_new = _mix_call(x2, u2, uc2, ylru, ada4, l, lambda b: b, wts, tq=tq,
                           rows_per_batch=seq, ctx_rows=lc, local=True, final=last)
        if not last:
            xc2 = _mix_call(xc2, uc2, uc2, ylru_c, ada4, l, lambda b: ctx_row, wts, tq=lc,
                            rows_per_batch=lc, ctx_rows=lc, local=False, final=False)
        x2 = x2_new
    return x2.reshape(nb, seq, d)
```

```python
import functools
import math

import jax
import jax.numpy as jnp
from jax import lax
from jax.experimental import pallas as pl
from jax.experimental.pallas import tpu as pltpu

F32 = jnp.float32
BF16 = jnp.bfloat16

D_MODEL = 2048
D_CONV = 512
D_LRU = 512
HEAD_DIM = 128
N_Q_HEADS = 8
N_KV_HEADS = 2
GQA_GROUP = N_Q_HEADS // N_KV_HEADS
D_ATTN = N_Q_HEADS * HEAD_DIM
D_KV = N_KV_HEADS * HEAD_DIM
GRID_W = 64
CONV_WIDTH = 31
CONV_PAD = (CONV_WIDTH - 1) // 2
LRU_CONV_WIDTH = 4
LRU_BLOCKS = 8
LRU_C = 8.0
WINDOW = 128
ROPE_BASE = 10000.0
EPS = 1e-6

OFF_K = D_LRU
OFF_V = OFF_K + D_KV
OFF_MEM_END = OFF_V + D_KV
OFF_CONV_GLU = OFF_MEM_END
OFF_CONV_GATE = OFF_CONV_GLU + 2 * D_CONV
OFF_LRU_GATE = OFF_CONV_GATE + D_CONV
OFF_Q = OFF_LRU_GATE + D_LRU
OFF_ATTN_GATE = OFF_Q + D_ATTN
D_IN = OFF_ATTN_GATE + D_ATTN
U_ATTN_GATE = OFF_Q
D_U = D_IN - D_ATTN

ADA_ROWS = 8
IN_TILE_N = 1024
Q_TILE = OFF_Q // IN_TILE_N
CONV_HALO = 16
CONV_ROWS = 64
CONV_COLS = 256
LRU_HALO = 8
SUBLANES = 8
VMEM_LIMIT = 56 * 1024 * 1024
LOG2E = math.log2(math.e)
Q_SCALE = HEAD_DIM ** -0.5 * LOG2E

assert OFF_Q % IN_TILE_N == 0 and D_ATTN == IN_TILE_N and OFF_MEM_END == IN_TILE_N


def _sigmoid(x):
    return 0.5 * jnp.tanh(0.5 * x) + 0.5


def _silu(x):
    hx = 0.5 * x
    return hx * jnp.tanh(hx) + hx


def _ada_kernel(c_ref, w_ref, b_ref, o_ref):
    ca = _silu(c_ref[...])
    o_ref[...] = jnp.dot(ca.astype(BF16), w_ref[...].astype(BF16),
                         preferred_element_type=F32) + b_ref[...]


def _ada_call(c_rows, w_ada, b_ada):
    depth, d, n = w_ada.shape
    tn = 1024
    return pl.pallas_call(
        _ada_kernel,
        grid=(depth, n // tn),
        in_specs=[
            pl.BlockSpec((ADA_ROWS, d), lambda l, j: (0, 0)),
            pl.BlockSpec((None, d, tn), lambda l, j: (l, 0, j)),
            pl.BlockSpec((None, 1, tn), lambda l, j: (l, 0, j)),
        ],
        out_specs=pl.BlockSpec((None, ADA_ROWS, tn), lambda l, j: (l, 0, j)),
        out_shape=jax.ShapeDtypeStruct((depth, ADA_ROWS, n), F32),
        compiler_params=pltpu.CompilerParams(
            dimension_semantics=("arbitrary", "arbitrary"), vmem_limit_bytes=VMEM_LIMIT),
        name="ada_proj",
    )(c_rows, w_ada, b_ada.reshape(depth, 1, n))


def _rope(t, cos, sin_a, sin_b):
    return (t * cos + pltpu.roll(t, HEAD_DIM - 32, 1) * sin_a + pltpu.roll(t, 32, 1) * sin_b)


def _inproj_kernel(*refs, rope, with_q, tm):
    refs = list(refs)
    x_ref, g_ref, shift_ref, scale_ref, w_ref = refs[:5]
    del refs[:5]
    if rope:
        cos_ref, sa_ref, sb_ref = refs[:3]
        del refs[:3]
    o_ref = refs.pop(0)
    q_ref = refs.pop(0) if with_q else None
    h_ref = refs.pop(0)
    j = pl.program_id(1)

    @pl.when(j == 0)
    def _():
        x = x_ref[...]
        ms = jnp.mean(x * x, axis=-1, keepdims=True)
        y = x * lax.rsqrt(ms + EPS) * g_ref[...]
        h_ref[...] = (y * (1.0 + scale_ref[...]) + shift_ref[...]).astype(BF16)

    u = jnp.dot(h_ref[...], w_ref[...], preferred_element_type=F32)

    if with_q:
        @pl.when(j == Q_TILE)
        def _():
            for h in range(N_Q_HEADS):
                t = u[:, h * HEAD_DIM:(h + 1) * HEAD_DIM]
                if rope:
                    t = _rope(t, cos_ref[...], sa_ref[...], sb_ref[...])
                t = (t * Q_SCALE).astype(BF16)
                for qb in range(tm // WINDOW):
                    q_ref[qb, h] = t[qb * WINDOW:(qb + 1) * WINDOW, :]

    if rope:
        @pl.when(j == 0)
        def _():
            o_ref[:, :OFF_K] = u[:, :OFF_K].astype(BF16)
            for h in range(N_KV_HEADS):
                sl = slice(OFF_K + h * HEAD_DIM, OFF_K + (h + 1) * HEAD_DIM)
                o_ref[:, sl] = _rope(u[:, sl], cos_ref[...], sa_ref[...], sb_ref[...]).astype(BF16)
            o_ref[:, OFF_V:] = u[:, OFF_V:].astype(BF16)

        plain = (j != 0) & (j != Q_TILE)
    else:
        plain = j != Q_TILE if with_q else None

    if plain is None:
        o_ref[...] = u.astype(BF16)
    else:
        @pl.when(plain)
        def _():
            o_ref[...] = u.astype(BF16)


def _inproj_call(x2, norm_g, ada4, layer, w_bf, rope_tabs, *, tm, rows_per_batch, ada_row_fn,
                 with_q):
    m, d = x2.shape
    tiles_per_batch = rows_per_batch // tm
    rope = rope_tabs is not None
    n_tiles = D_IN // IN_TILE_N if with_q else 1

    def ada_spec(part):
        return pl.BlockSpec((None, None, 1, d),
                            lambda i, j: (layer, ada_row_fn(i // tiles_per_batch), 0, part))

    in_specs = [
        pl.BlockSpec((tm, d), lambda i, j: (i, 0)),
        pl.BlockSpec((1, d), lambda i, j: (0, 0)),
        ada_spec(0), ada_spec(1),
        pl.BlockSpec((d, IN_TILE_N), lambda i, j: (0, j)),
    ]
    args = [x2, norm_g.reshape(1, d), ada4, ada4, w_bf]
    if rope:
        tab_spec = pl.BlockSpec((tm, HEAD_DIM), lambda i, j: (i % tiles_per_batch, 0))
        in_specs += [tab_spec] * 3
        args += list(rope_tabs)
    out_specs = [pl.BlockSpec((tm, IN_TILE_N), lambda i, j: (i, jnp.where(j > Q_TILE, j - 1, j)))]
    out_shape = [jax.ShapeDtypeStruct((m, D_U if with_q else IN_TILE_N), BF16)]
    if with_q:
        out_specs.append(pl.BlockSpec((tm // WINDOW, N_Q_HEADS, WINDOW, HEAD_DIM),
                                      lambda i, j: (i, 0, 0, 0)))
        out_shape.append(jax.ShapeDtypeStruct((m // WINDOW, N_Q_HEADS, WINDOW, HEAD_DIM), BF16))
    outs = pl.pallas_call(
        functools.partial(_inproj_kernel, rope=rope, with_q=with_q, tm=tm),
        grid=(m // tm, n_tiles),
        in_specs=in_specs,
        out_specs=out_specs,
        out_shape=out_shape,
        scratch_shapes=[pltpu.VMEM((tm, d), BF16)],
        compiler_params=pltpu.CompilerParams(
            dimension_semantics=("arbitrary", "arbitrary"), vmem_limit_bytes=VMEM_LIMIT),
        name="in_proj",
    )(*args)
    return (outs[0], outs[1]) if with_q else (outs[0], None)


def _lru_kernel(*refs, reverse, has_acc, tt):
    if has_acc:
        (x_ref, h0_ref, acc_ref, cw_ref, cb_ref, wg_ref, br_ref, bi_ref, lam_ref,
         o_ref, ext_ref, a_ref, b_ref, h_ref) = refs
    else:
        (x_ref, h0_ref, cw_ref, cb_ref, wg_ref, br_ref, bi_ref, lam_ref,
         o_ref, ext_ref, a_ref, b_ref, h_ref) = refs
        acc_ref = None
    c = pl.program_id(1)
    carry_rows = slice(LRU_HALO + tt, 2 * LRU_HALO + tt) if reverse else slice(0, LRU_HALO)

    @pl.when(c == 0)
    def _():
        ext_ref[carry_rows, :] = jnp.zeros((LRU_HALO, D_LRU), F32)
        h_ref[...] = jnp.broadcast_to(h0_ref[...], (SUBLANES, D_LRU))

    x = x_ref[...].astype(F32)
    ext_ref[LRU_HALO:LRU_HALO + tt, :] = x
    xc = jnp.broadcast_to(cb_ref[...], (tt, D_LRU))
    for k in range(LRU_CONV_WIDTH):
        off = (LRU_CONV_WIDTH - 1 - k) if reverse else (k - (LRU_CONV_WIDTH - 1))
        xc = xc + cw_ref[k:k + 1, :] * ext_ref[pl.ds(LRU_HALO + off, tt), :]
    if reverse:
        ext_ref[carry_rows, :] = x[:LRU_HALO, :]
    else:
        ext_ref[carry_rows, :] = x[tt - LRU_HALO:, :]

    g = jnp.dot(xc.astype(BF16), wg_ref[...], preferred_element_type=F32)
    r = _sigmoid(g[:, :D_LRU] + br_ref[...])
    gi = _sigmoid(g[:, D_LRU:] + bi_ref[...])
    lam = lam_ref[...]
    softplus_neg_lam = jnp.maximum(-lam, 0.0) + jnp.log1p(jnp.exp(-jnp.abs(lam)))
    log_a = (-LRU_C) * r * softplus_neg_lam
    a_ref[...] = jnp.exp(log_a)
    th = jnp.tanh(log_a)
    one_minus_a2 = (-2.0 * th) / (1.0 - th)
    b_ref[...] = jnp.sqrt(one_minus_a2) * (gi * xc)

    row = lax.broadcasted_iota(jnp.int32, (SUBLANES, D_LRU), 0)
    n_groups = tt // SUBLANES

    def group(gidx, h_prev):
        gi_ = (n_groups - 1 - gidx) if reverse else gidx
        rows = pl.ds(pl.multiple_of(gi_ * SUBLANES, SUBLANES), SUBLANES)
        a = a_ref[rows, :]
        b = b_ref[rows, :]
        for d in (1, 2, 4):
            if reverse:
                keep = row < SUBLANES - d
                shift = SUBLANES - d
            else:
                keep = row >= d
                shift = d
            a_s = jnp.where(keep, pltpu.roll(a, shift, 0), 1.0)
            b_s = jnp.where(keep, pltpu.roll(b, shift, 0), 0.0)
            b = a * b_s + b
            a = a * a_s
        h = a * h_prev + b
        if has_acc:
            o_ref[rows, :] = acc_ref[rows, :] + h
        else:
            o_ref[rows, :] = h
        edge = h[0:1, :] if reverse else h[SUBLANES - 1:SUBLANES, :]
        return jnp.broadcast_to(edge, (SUBLANES, D_LRU))

    h_ref[...] = lax.fori_loop(0, n_groups, group, h_ref[...])


def _lru_call(u2, h0, acc, params, *, reverse, rows_per_batch, tt):
    cw, cb, wg, br, bi, lam = params
    m = u2.shape[0]
    nb = m // rows_per_batch
    nc = rows_per_batch // tt
    has_acc = acc is not None

    def chunk(c):
        return (nc - 1 - c) if reverse else c

    row_spec = pl.BlockSpec((tt, D_LRU), lambda b, c: (b * nc + chunk(c), 0))
    full = lambda shape: pl.BlockSpec(shape, lambda b, c: (0,) * len(shape))
    in_specs = [row_spec, pl.BlockSpec((None, 1, D_LRU), lambda b, c: (b, 0, 0))]
    args = [u2, h0]
    if has_acc:
        in_specs.append(row_spec)
        args.append(acc)
    in_specs += [full((LRU_CONV_WIDTH, D_LRU)), full((1, D_LRU)), full((D_LRU, 2 * D_LRU)),
                 full((1, D_LRU)), full((1, D_LRU)), full((1, D_LRU))]
    args += [cw, cb, wg, br, bi, lam]
    return pl.pallas_call(
        functools.partial(_lru_kernel, reverse=reverse, has_acc=has_acc, tt=tt),
        grid=(nb, nc),
        in_specs=in_specs,
        out_specs=row_spec,
        out_shape=jax.ShapeDtypeStruct((m, D_LRU), F32),
        scratch_shapes=[pltpu.VMEM((tt + 2 * LRU_HALO, D_LRU), F32),
                        pltpu.VMEM((tt, D_LRU), F32),
                        pltpu.VMEM((tt, D_LRU), F32),
                        pltpu.VMEM((SUBLANES, D_LRU), F32)],
        compiler_params=pltpu.CompilerParams(
            dimension_semantics=("arbitrary", "arbitrary"), vmem_limit_bytes=VMEM_LIMIT),
        name="rglru_rev" if reverse else "rglru_fwd",
    )(*args)


def _mix_kernel(*refs, names, tq, local, final):
    r = dict(zip(names, refs))
    i = pl.program_id(1)
    nt = pl.num_programs(1)
    ext_ref, ycat_ref = r["ext"], r["ycat"]

    def glu(ref):
        t = ref[...].astype(F32)
        return t[:, :D_CONV] * _sigmoid(t[:, D_CONV:])

    ext_ref[CONV_HALO:CONV_HALO + tq, :] = glu(r["glu"])
    zero_halo = jnp.zeros((CONV_HALO, D_CONV), F32)
    if local:
        ext_ref[0:CONV_HALO, :] = jnp.where(i > 0, glu(r["glu_prev"]), zero_halo)
        ext_ref[CONV_HALO + tq:, :] = jnp.where(i < nt - 1, glu(r["glu_next"]), zero_halo)
    else:
        ext_ref[0:CONV_HALO, :] = zero_halo
        ext_ref[CONV_HALO + tq:, :] = zero_halo

    for rc in range(tq // CONV_ROWS):
        t0 = rc * CONV_ROWS
        rows = slice(t0, t0 + CONV_ROWS)
        pieces = []
        for cc in range(D_CONV // CONV_COLS):
            cols = slice(cc * CONV_COLS, (cc + 1) * CONV_COLS)
            acc = jnp.broadcast_to(r["dw_b"][:, cols], (CONV_ROWS, CONV_COLS))
            for b in range(SUBLANES):
                z = None
                for a in range(-(-(CONV_WIDTH + 1) // SUBLANES)):
                    o = SUBLANES * a + b
                    if o < 1 or o > CONV_WIDTH:
                        continue
                    term = (r["dw_w"][o - 1:o, cols]
                            * ext_ref[t0 + SUBLANES * a:t0 + SUBLANES * a + CONV_ROWS + SUBLANES,
                                      cols])
                    z = term if z is None else z + term
                acc = acc + z[b:b + CONV_ROWS, :]
            pieces.append(acc)
        acc = jnp.concatenate(pieces, axis=1)
        mu = jnp.mean(acc, axis=-1, keepdims=True)
        cen = acc - mu
        var = jnp.mean(cen * cen, axis=-1, keepdims=True)
        y = cen * lax.rsqrt(var + EPS) * r["ln_g"][...] + r["ln_b"][...]
        y = _silu(y).astype(BF16)
        y = jnp.dot(y, r["pw_w"][...], preferred_element_type=F32) + r["pw_b"][...]
        y = y * _silu(r["conv_gate"][rows, :].astype(F32))
        ycat_ref[rows, 0:D_CONV] = y.astype(BF16)

    ylru = r["ylru"][...]
    if "ylru_rev" in r:
        ylru = ylru + r["ylru_rev"][...]
    ycat_ref[:, D_CONV:D_CONV + D_LRU] = (
        ylru * _silu(r["lru_gate"][...].astype(F32))).astype(BF16)

    qrows = GQA_GROUP * WINDOW
    qi = lax.broadcasted_iota(jnp.int32, (qrows, WINDOW), 0) % WINDOW
    kj = lax.broadcasted_iota(jnp.int32, (qrows, WINDOW), 1)
    neg_inf = jnp.float32(-jnp.inf)
    dn = (((1,), (1,)), ((), ()))
    n_qb = tq // WINDOW
    lc = r["kc"].shape[0]
    for qb in range(n_qb):
        rows = slice(qb * WINDOW, (qb + 1) * WINDOW)
        for g in range(N_KV_HEADS):
            kv_cols = slice(g * HEAD_DIM, (g + 1) * HEAD_DIM)
            q_st = r["q"][qb, g * GQA_GROUP:(g + 1) * GQA_GROUP].reshape(qrows, HEAD_DIM)
            sink = jnp.concatenate(
                [jnp.broadcast_to(r["sink"][g * GQA_GROUP + hh:g * GQA_GROUP + hh + 1, 0:1],
                                  (WINDOW, 1)) for hh in range(GQA_GROUP)], axis=0) * LOG2E
            blocks = []
            for cb in range(lc // WINDOW):
                crow = slice(cb * WINDOW, (cb + 1) * WINDOW)
                s = lax.dot_general(q_st, r["kc"][crow, kv_cols], dn, preferred_element_type=F32)
                blocks.append((s, r["vc"][crow, kv_cols]))
            if local:
                if qb == 0:
                    k_p, v_p = r["k_prev"][:, kv_cols], r["v_prev"][:, kv_cols]
                    ok_p = i > 0
                else:
                    prows = slice((qb - 1) * WINDOW, qb * WINDOW)
                    k_p, v_p = r["k"][prows, kv_cols], r["v"][prows, kv_cols]
                    ok_p = True
                if qb == n_qb - 1:
                    k_n, v_n = r["k_next"][:, kv_cols], r["v_next"][:, kv_cols]
                    ok_n = i < nt - 1
                else:
                    nrows = slice((qb + 1) * WINDOW, (qb + 2) * WINDOW)
                    k_n, v_n = r["k"][nrows, kv_cols], r["v"][nrows, kv_cols]
                    ok_n = True
                s_p = lax.dot_general(q_st, k_p, dn, preferred_element_type=F32)
                s_c = lax.dot_general(q_st, r["k"][rows, kv_cols], dn, preferred_element_type=F32)
                s_n = lax.dot_general(q_st, k_n, dn, preferred_element_type=F32)
                s_p = jnp.where((kj >= qi) & ok_p, s_p, neg_inf)
                s_n = jnp.where((kj <= qi) & ok_n, s_n, neg_inf)
                blocks += [(s_p, v_p), (s_c, r["v"][rows, kv_cols]), (s_n, v_n)]
            m_el = blocks[0][0]
            for s, _ in blocks[1:]:
                m_el = jnp.maximum(m_el, s)
            m = jnp.maximum(jnp.max(m_el, axis=-1, keepdims=True), sink)
            den_el = None
            o = None
            for s, v in blocks:
                p = jnp.exp2(s - m)
                den_el = p if den_el is None else den_el + p
                pv = jnp.dot(p.astype(BF16), v, preferred_element_type=F32)
                o = pv if o is None else o + pv
            den = jnp.sum(den_el, axis=-1, keepdims=True) + jnp.exp2(sink - m)
            o = o / den
            for hh in range(GQA_GROUP):
                c0 = (g * GQA_GROUP + hh) * HEAD_DIM
                gate = _silu(r["attn_gate"][rows, c0:c0 + HEAD_DIM].astype(F32))
                ycat_ref[rows, D_CONV + D_LRU + c0:D_CONV + D_LRU + c0 + HEAD_DIM] = (
                    o[hh * WINDOW:(hh + 1) * WINDOW, :] * gate).astype(BF16)

    y = jnp.dot(ycat_ref[...], r["w_out"][...], preferred_element_type=F32)
    xn = r["x"][...] + r["gate"][...] * y
    if final:
        ms = jnp.mean(xn * xn, axis=-1, keepdims=True)
        xn = xn * lax.rsqrt(ms + EPS) * r["final_g"][...]
    r["out"][...] = xn


def _mix_call(x2, u2, q4, uc2, ylru, ada4, layer, ada_row_fn, wts, *, tq, rows_per_batch,
              ctx_rows, local, final):
    m, d = x2.shape
    nt = rows_per_batch // tq
    nb = m // rows_per_batch
    names, specs, args = [], [], []

    def add(name, arr, spec):
        names.append(name)
        specs.append(spec)
        args.append(arr)

    def rowblk(width, col_off):
        return pl.BlockSpec((tq, width), lambda b, i: (b * nt + i, col_off // width))

    def halo(rows, width, col_off, nxt):
        per = tq // rows
        last = m // rows - 1
        if nxt:
            fn = lambda b, i: (jnp.minimum((b * nt + i + 1) * per, last), col_off // width)
        else:
            fn = lambda b, i: (jnp.maximum((b * nt + i) * per - 1, 0), col_off // width)
        return pl.BlockSpec((rows, width), fn)

    full = lambda shape: pl.BlockSpec(shape, lambda b, i: (0,) * len(shape),
                                      pipeline_mode=pl.Buffered(1))

    add("x", x2, pl.BlockSpec((tq, d), lambda b, i: (b * nt + i, 0)))
    add("gate", ada4, pl.BlockSpec((None, None, 1, d), lambda b, i: (layer, ada_row_fn(b), 0, 2)))
    add("glu", u2, rowblk(2 * D_CONV, OFF_CONV_GLU))
    if local:
        add("glu_prev", u2, halo(CONV_HALO, 2 * D_CONV, OFF_CONV_GLU, False))
        add("glu_next", u2, halo(CONV_HALO, 2 * D_CONV, OFF_CONV_GLU, True))
    add("conv_gate", u2, rowblk(D_CONV, OFF_CONV_GATE))
    add("lru_gate", u2, rowblk(D_LRU, OFF_LRU_GATE))
    ylru_spec = pl.BlockSpec((tq, D_LRU), lambda b, i: (b * nt + i, 0))
    if isinstance(ylru, tuple):
        add("ylru", ylru[0], ylru_spec)
        add("ylru_rev", ylru[1], ylru_spec)
    else:
        add("ylru", ylru, ylru_spec)
    add("q", q4, pl.BlockSpec((tq // WINDOW, N_Q_HEADS, WINDOW, HEAD_DIM),
                              lambda b, i: (b * nt + i, 0, 0, 0)))
    add("attn_gate", u2, rowblk(D_ATTN, U_ATTN_GATE))
    if local:
        add("k", u2, rowblk(D_KV, OFF_K))
        add("v", u2, rowblk(D_KV, OFF_V))
        add("k_prev", u2, halo(WINDOW, D_KV, OFF_K, False))
        add("v_prev", u2, halo(WINDOW, D_KV, OFF_V, False))
        add("k_next", u2, halo(WINDOW, D_KV, OFF_K, True))
        add("v_next", u2, halo(WINDOW, D_KV, OFF_V, True))
    add("kc", uc2, pl.BlockSpec((ctx_rows, D_KV), lambda b, i: (b, OFF_K // D_KV)))
    add("vc", uc2, pl.BlockSpec((ctx_rows, D_KV), lambda b, i: (b, OFF_V // D_KV)))
    for name in ("dw_w", "dw_b", "ln_g", "ln_b", "pw_w", "pw_b", "sink", "w_out"):
        add(name, wts[name], full(wts[name].shape))
    if final:
        add("final_g", wts["final_g"], full(wts["final_g"].shape))
    names += ["out", "ext", "ycat"]
    return pl.pallas_call(
        functools.partial(_mix_kernel, names=tuple(names), tq=tq, local=local, final=final),
        grid=(nb, nt),
        in_specs=specs,
        out_specs=pl.BlockSpec((tq, d), lambda b, i: (b * nt + i, 0)),
        out_shape=jax.ShapeDtypeStruct((m, d), F32),
        scratch_shapes=[pltpu.VMEM((tq + 2 * CONV_HALO, D_CONV), F32),
                        pltpu.VMEM((tq, D_MODEL), BF16)],
        compiler_params=pltpu.CompilerParams(
            dimension_semantics=("arbitrary", "arbitrary"), vmem_limit_bytes=VMEM_LIMIT),
        name="mix_lat" if local else "mix_ctx",
    )(*args)


def _rope_tables(seq):
    rows = seq // GRID_W
    row = jnp.repeat(jnp.arange(rows, dtype=F32), GRID_W)
    col = jnp.tile(jnp.arange(GRID_W, dtype=F32), rows)
    half = HEAD_DIM // 2
    inv = ROPE_BASE ** (-jnp.arange(0, half, 2, dtype=F32) / half)
    ang_r = row[:, None] * inv[None, :]
    ang_c = col[:, None] * inv[None, :]
    ang = jnp.concatenate([ang_r, ang_r, ang_c, ang_c], axis=-1)
    cos, sin = jnp.cos(ang), jnp.sin(ang)
    first = (jnp.arange(HEAD_DIM) % half) < (half // 2)
    return cos, jnp.where(first, -sin, 0.0), jnp.where(first, 0.0, sin)


def _block_diag(w):
    nblk, blk, _ = w.shape
    eye = jnp.eye(nblk, dtype=w.dtype)
    return (w[:, :, None, :] * eye[:, None, :, None]).reshape(nblk * blk, nblk * blk)


def _pick_tile(n, pref):
    t = min(n, pref)
    while n % t:
        t //= 2
    return t


def kernel(x, c, ctx, c_ctx, norm_g, w_ada, b_ada, w_in, dw_w, dw_b, ln_g, ln_b, pw_w, pw_b,
           lru_conv_w, lru_conv_b, lru_w_r, lru_b_r, lru_w_i, lru_b_i, lru_lam, attn_sink,
           w_out, final_g):
    nb, seq, d = x.shape
    lc = ctx.shape[1]
    depth = w_in.shape[0]
    ctx_row = nb

    c_rows = jnp.zeros((ADA_ROWS, d), F32).at[:nb].set(c).at[ctx_row].set(c_ctx)
    ada = _ada_call(c_rows, w_ada, b_ada)
    ada4 = ada.reshape(depth, ADA_ROWS, 1, 3 * d)
    rope_tabs = _rope_tables(seq)

    x2 = x.reshape(nb * seq, d)
    xc2 = ctx.reshape(nb * lc, d)
    tm = _pick_tile(seq, 1024)
    tq = _pick_tile(seq, 512)
    tt = _pick_tile(seq, 512)
    zeros_h0 = jnp.zeros((nb, 1, D_LRU), F32)

    for l in range(depth):
        last = l == depth - 1
        w_bf = w_in[l].astype(BF16)
        u2, q4 = _inproj_call(x2, norm_g[l], ada4, l, w_bf, rope_tabs, tm=tm, rows_per_batch=seq,
                              ada_row_fn=lambda b: b, with_q=True)
        uc2, qc4 = _inproj_call(xc2, norm_g[l], ada4, l, w_bf, None, tm=lc, rows_per_batch=lc,
                                ada_row_fn=lambda b: ctx_row, with_q=not last)

        ylru, ylru_c = None, ()
        for dr in range(2):
            rev = dr == 1
            wg = jnp.concatenate([_block_diag(lru_w_r[l, dr]), _block_diag(lru_w_i[l, dr])],
                                 axis=1).astype(BF16)
            params = (lru_conv_w[l, dr], lru_conv_b[l, dr].reshape(1, D_LRU), wg,
                      lru_b_r[l, dr].reshape(1, D_LRU), lru_b_i[l, dr].reshape(1, D_LRU),
                      lru_lam[l, dr].reshape(1, D_LRU))
            h_ctx = _lru_call(uc2, zeros_h0, None, params, reverse=rev, rows_per_batch=lc, tt=lc)
            edge = 0 if rev else lc - 1
            h0 = h_ctx.reshape(nb, lc, D_LRU)[:, edge:edge + 1, :]
            ylru = _lru_call(u2, h0, ylru, params, reverse=rev, rows_per_batch=seq, tt=tt)
            ylru_c += (h_ctx,)

        wts = {
            "dw_w": dw_w[l], "dw_b": dw_b[l].reshape(1, D_CONV),
            "ln_g": ln_g[l].reshape(1, D_CONV), "ln_b": ln_b[l].reshape(1, D_CONV),
            "pw_w": pw_w[l].astype(BF16), "pw_b": pw_b[l].reshape(1, D_CONV),
            "sink": jnp.broadcast_to(attn_sink[l].reshape(N_Q_HEADS, 1), (N_Q_HEADS, HEAD_DIM)),
            "w_out": w_out[l].astype(BF16), "final_g": final_g.reshape(1, d),
        }
        x2_new = _mix_call(x2, u2, q4, uc2, ylru, ada4, l, lambda b: b, wts, tq=tq,
                           rows_per_batch=seq, ctx_rows=lc, local=True, final=last)
        if not last:
            xc2 = _mix_call(xc2, uc2, qc4, uc2, ylru_c, ada4, l, lambda b: ctx_row, wts, tq=lc,
                            rows_per_batch=lc, ctx_rows=lc, local=False, final=False)
        x2 = x2_new
    return x2.reshape(nb, seq, d)
```

```python
import functools
import math

import jax
import jax.numpy as jnp
from jax import lax
from jax.experimental import pallas as pl
from jax.experimental.pallas import tpu as pltpu

F32 = jnp.float32
BF16 = jnp.bfloat16

D_MODEL = 2048
D_CONV = 512
D_LRU = 512
HEAD_DIM = 128
N_Q_HEADS = 8
N_KV_HEADS = 2
GQA_GROUP = N_Q_HEADS // N_KV_HEADS
D_ATTN = N_Q_HEADS * HEAD_DIM
D_KV = N_KV_HEADS * HEAD_DIM
GRID_W = 64
CONV_WIDTH = 31
CONV_PAD = (CONV_WIDTH - 1) // 2
LRU_CONV_WIDTH = 4
LRU_BLOCKS = 8
LRU_C = 8.0
WINDOW = 128
ROPE_BASE = 10000.0
EPS = 1e-6

OFF_K = D_LRU
OFF_V = OFF_K + D_KV
OFF_MEM_END = OFF_V + D_KV
OFF_CONV_GLU = OFF_MEM_END
OFF_CONV_GATE = OFF_CONV_GLU + 2 * D_CONV
OFF_LRU_GATE = OFF_CONV_GATE + D_CONV
OFF_Q = OFF_LRU_GATE + D_LRU
OFF_ATTN_GATE = OFF_Q + D_ATTN
D_IN = OFF_ATTN_GATE + D_ATTN
U_ATTN_GATE = OFF_Q
D_U = D_IN - D_ATTN

ADA_ROWS = 8
IN_TILE_N = 1024
IN_CHUNK_N = 256
Q_TILE = OFF_Q // IN_TILE_N
CONV_HALO = 16
CONV_ROWS = 64
CONV_COLS = 256
SUBLANES = 8
LANES = 128
VMEM_LIMIT = 56 * 1024 * 1024
LOG2E = math.log2(math.e)
Q_SCALE = HEAD_DIM ** -0.5 * LOG2E

assert OFF_Q % IN_TILE_N == 0 and D_ATTN == IN_TILE_N and OFF_MEM_END == IN_TILE_N


def _sigmoid(x):
    return 0.5 * jnp.tanh(0.5 * x) + 0.5


def _silu(x):
    hx = 0.5 * x
    return hx * jnp.tanh(hx) + hx


def _ada_kernel(c_ref, w_ref, b_ref, o_ref):
    ca = _silu(c_ref[...])
    o_ref[...] = jnp.dot(ca.astype(BF16), w_ref[...].astype(BF16),
                         preferred_element_type=F32) + b_ref[...]


def _ada_call(c_rows, w_ada, b_ada):
    depth, d, n = w_ada.shape
    tn = 1024
    return pl.pallas_call(
        _ada_kernel,
        grid=(depth, n // tn),
        in_specs=[
            pl.BlockSpec((ADA_ROWS, d), lambda l, j: (0, 0)),
            pl.BlockSpec((None, d, tn), lambda l, j: (l, 0, j)),
            pl.BlockSpec((None, 1, tn), lambda l, j: (l, 0, j)),
        ],
        out_specs=pl.BlockSpec((None, ADA_ROWS, tn), lambda l, j: (l, 0, j)),
        out_shape=jax.ShapeDtypeStruct((depth, ADA_ROWS, n), F32),
        compiler_params=pltpu.CompilerParams(
            dimension_semantics=("arbitrary", "arbitrary"), vmem_limit_bytes=VMEM_LIMIT),
        name="ada_proj",
    )(c_rows, w_ada, b_ada.reshape(depth, 1, n))


def _rope(t, cos, sin_a, sin_b):
    return (t * cos + pltpu.roll(t, HEAD_DIM - 32, 1) * sin_a + pltpu.roll(t, 32, 1) * sin_b)


def _inproj_kernel(*refs, rope, with_q, tm):
    refs = list(refs)
    x_ref, g_ref, shift_ref, scale_ref, w_ref = refs[:5]
    del refs[:5]
    if rope:
        cos_ref, sa_ref, sb_ref = refs[:3]
        del refs[:3]
    o_ref = refs.pop(0)
    q_ref = refs.pop(0) if with_q else None
    h_ref = refs.pop(0)
    j = pl.program_id(1)

    @pl.when(j == 0)
    def _():
        x = x_ref[...]
        ms = jnp.mean(x * x, axis=-1, keepdims=True)
        gain = g_ref[...] * (1.0 + scale_ref[...])
        h_ref[...] = (x * lax.rsqrt(ms + EPS) * gain + shift_ref[...]).astype(BF16)

    chunks = [slice(c0, c0 + IN_CHUNK_N) for c0 in range(0, IN_TILE_N, IN_CHUNK_N)]
    heads_per_chunk = IN_CHUNK_N // HEAD_DIM

    def proj(cols):
        return jnp.dot(h_ref[...], w_ref[:, cols], preferred_element_type=F32)

    def rope_heads(u):
        if not rope:
            return [u[:, hh * HEAD_DIM:(hh + 1) * HEAD_DIM] for hh in range(heads_per_chunk)]
        return [_rope(u[:, hh * HEAD_DIM:(hh + 1) * HEAD_DIM], cos_ref[...], sa_ref[...],
                      sb_ref[...]) for hh in range(heads_per_chunk)]

    def plain_tile():
        for cols in chunks:
            o_ref[:, cols] = proj(cols).astype(BF16)

    def q_tile():
        for ci, cols in enumerate(chunks):
            for hh, t in enumerate(rope_heads(proj(cols))):
                t = (t * Q_SCALE).astype(BF16)
                for qb in range(tm // WINDOW):
                    q_ref[qb, ci * heads_per_chunk + hh] = t[qb * WINDOW:(qb + 1) * WINDOW, :]

    def kv_tile():
        for cols in chunks:
            u = proj(cols)
            if OFF_K <= cols.start < OFF_V:
                u = jnp.concatenate(rope_heads(u), axis=1)
            o_ref[:, cols] = u.astype(BF16)

    if not with_q:
        plain_tile()
    else:
        pl.when(j == Q_TILE)(q_tile)
        if rope:
            pl.when(j == 0)(kv_tile)
            pl.when((j != 0) & (j != Q_TILE))(plain_tile)
        else:
            pl.when(j != Q_TILE)(plain_tile)


def _inproj_call(x2, norm_g, ada4, layer, w_bf, rope_tabs, *, tm, rows_per_batch, ada_row_fn,
                 with_q):
    m, d = x2.shape
    tiles_per_batch = rows_per_batch // tm
    rope = rope_tabs is not None
    n_tiles = D_IN // IN_TILE_N if with_q else 1

    def ada_spec(part):
        return pl.BlockSpec((None, None, 1, d),
                            lambda i, j: (layer, ada_row_fn(i // tiles_per_batch), 0, part))

    in_specs = [
        pl.BlockSpec((tm, d), lambda i, j: (i, 0)),
        pl.BlockSpec((1, d), lambda i, j: (0, 0)),
        ada_spec(0), ada_spec(1),
        pl.BlockSpec((None, d, IN_TILE_N), lambda i, j: (layer, 0, j)),
    ]
    args = [x2, norm_g.reshape(1, d), ada4, ada4, w_bf]
    if rope:
        tab_spec = pl.BlockSpec((tm, HEAD_DIM), lambda i, j: (i % tiles_per_batch, 0))
        in_specs += [tab_spec] * 3
        args += list(rope_tabs)
    out_specs = [pl.BlockSpec((tm, IN_TILE_N), lambda i, j: (i, jnp.where(j > Q_TILE, j - 1, j)))]
    out_shape = [jax.ShapeDtypeStruct((m, D_U if with_q else IN_TILE_N), BF16)]
    if with_q:
        out_specs.append(pl.BlockSpec((tm // WINDOW, N_Q_HEADS, WINDOW, HEAD_DIM),
                                      lambda i, j: (i, 0, 0, 0)))
        out_shape.append(jax.ShapeDtypeStruct((m // WINDOW, N_Q_HEADS, WINDOW, HEAD_DIM), BF16))
    outs = pl.pallas_call(
        functools.partial(_inproj_kernel, rope=rope, with_q=with_q, tm=tm),
        grid=(m // tm, n_tiles),
        in_specs=in_specs,
        out_specs=out_specs,
        out_shape=out_shape,
        scratch_shapes=[pltpu.VMEM((tm, d), BF16)],
        compiler_params=pltpu.CompilerParams(
            dimension_semantics=("arbitrary", "arbitrary"), vmem_limit_bytes=VMEM_LIMIT),
        name="in_proj",
    )(*args)
    return (outs[0], outs[1]) if with_q else (outs[0], None)


def _lru_kernel(*refs, reverse, has_acc, tt):
    if has_acc:
        (x_ref, h0_ref, acc_ref, perm_ref, unperm_ref, cw_ref, cb_ref, wg_ref, br_ref, bi_ref,
         lam_ref, o_ref, ext_ref, a_ref, b_ref, edge_ref, h_ref) = refs
    else:
        (x_ref, h0_ref, perm_ref, unperm_ref, cw_ref, cb_ref, wg_ref, br_ref, bi_ref,
         lam_ref, o_ref, ext_ref, a_ref, b_ref, edge_ref, h_ref) = refs
        acc_ref = None
    c = pl.program_id(1)
    seg = tt // SUBLANES
    hist = LRU_CONV_WIDTH - 1
    row = lax.broadcasted_iota(jnp.int32, (SUBLANES, D_LRU), 0)

    def blk(g):
        return slice(g * SUBLANES, (g + 1) * SUBLANES)

    @pl.when(c == 0)
    def _():
        edge_ref[...] = jnp.zeros((hist * SUBLANES, D_LRU), F32)
        h_ref[...] = jnp.broadcast_to(h0_ref[...], (SUBLANES, D_LRU))

    base = 0 if reverse else hist
    ext_ref[base * SUBLANES:base * SUBLANES + tt, :] = jnp.dot(
        perm_ref[...], x_ref[...], preferred_element_type=F32)
    for j in range(1, hist + 1):
        if reverse:
            rolled = pltpu.roll(ext_ref[blk(j - 1), :], SUBLANES - 1, 0)
            ext_ref[blk(seg + j - 1), :] = jnp.where(row == SUBLANES - 1, edge_ref[blk(j - 1), :],
                                                     rolled)
        else:
            rolled = pltpu.roll(ext_ref[blk(hist + seg - j), :], 1, 0)
            ext_ref[blk(hist - j), :] = jnp.where(row == 0, edge_ref[blk(j - 1), :], rolled)
        edge_ref[blk(j - 1), :] = rolled

    xc = jnp.broadcast_to(cb_ref[...], (tt, D_LRU))
    for k in range(LRU_CONV_WIDTH):
        off = (hist - k) if reverse else k
        xc = xc + cw_ref[k:k + 1, :] * ext_ref[off * SUBLANES:off * SUBLANES + tt, :]

    g = jnp.dot(xc.astype(BF16), wg_ref[...], preferred_element_type=F32)
    r = _sigmoid(g[:, :D_LRU] + br_ref[...])
    gi = _sigmoid(g[:, D_LRU:] + bi_ref[...])
    lam = lam_ref[...]
    softplus_neg_lam = jnp.maximum(-lam, 0.0) + jnp.log1p(jnp.exp(-jnp.abs(lam)))
    log_a = (-LRU_C) * r * softplus_neg_lam
    a_ref[...] = jnp.exp(log_a)
    th = jnp.tanh(log_a)
    one_minus_a2 = (-2.0 * th) / (1.0 - th)
    b_ref[...] = jnp.sqrt(one_minus_a2) * (gi * xc)

    def step(t, carry):
        h, p = carry
        g_ = (seg - 1 - t) if reverse else t
        rows = pl.ds(pl.multiple_of(g_ * SUBLANES, SUBLANES), SUBLANES)
        a = a_ref[rows, :]
        h = a * h + b_ref[rows, :]
        p = a * p
        b_ref[rows, :] = h
        a_ref[rows, :] = p
        return h, p

    h_fin, p_fin = lax.fori_loop(
        0, seg, step, (jnp.zeros((SUBLANES, D_LRU), F32), jnp.ones((SUBLANES, D_LRU), F32)),
        unroll=8)

    a, b = p_fin, h_fin
    for d in (1, 2, 4):
        keep = (row < SUBLANES - d) if reverse else (row >= d)
        shift = (SUBLANES - d) if reverse else d
        a_s = jnp.where(keep, pltpu.roll(a, shift, 0), 1.0)
        b_s = jnp.where(keep, pltpu.roll(b, shift, 0), 0.0)
        b = a * b_s + b
        a = a * a_s
    h_in = h_ref[...]
    end = a * h_in + b
    if reverse:
        carry_in = jnp.where(row < SUBLANES - 1, pltpu.roll(end, SUBLANES - 1, 0), h_in)
        h_ref[...] = jnp.broadcast_to(end[0:1, :], (SUBLANES, D_LRU))
    else:
        carry_in = jnp.where(row >= 1, pltpu.roll(end, 1, 0), h_in)
        h_ref[...] = jnp.broadcast_to(end[SUBLANES - 1:SUBLANES, :], (SUBLANES, D_LRU))

    h_all = (b_ref[...].reshape(seg, SUBLANES, D_LRU)
             + a_ref[...].reshape(seg, SUBLANES, D_LRU) * carry_in[None]).reshape(tt, D_LRU)
    out = jnp.dot(unperm_ref[...], h_all.astype(BF16), preferred_element_type=F32)
    if has_acc:
        out = out + acc_ref[...].astype(F32)
    o_ref[...] = out.astype(o_ref.dtype)


def _lru_call(u2, h0, acc, params, *, reverse, rows_per_batch, tt):
    cw, cb, wg, br, bi, lam = params
    m = u2.shape[0]
    nb = m // rows_per_batch
    nc = rows_per_batch // tt
    has_acc = acc is not None

    def chunk(c):
        return (nc - 1 - c) if reverse else c

    row_spec = pl.BlockSpec((tt, D_LRU), lambda b, c: (b * nc + chunk(c), 0))
    full = lambda shape: pl.BlockSpec(shape, lambda b, c: (0,) * len(shape))
    in_specs = [row_spec, pl.BlockSpec((None, 1, D_LRU), lambda b, c: (b, 0, 0))]
    args = [u2, h0]
    if has_acc:
        in_specs.append(row_spec)
        args.append(acc)
    seg = tt // SUBLANES
    src = (jnp.arange(tt) % SUBLANES) * seg + jnp.arange(tt) // SUBLANES
    perm = (src[:, None] == jnp.arange(tt)[None, :]).astype(BF16)
    in_specs += [full((tt, tt)), full((tt, tt)),
                 full((LRU_CONV_WIDTH, D_LRU)), full((1, D_LRU)), full((D_LRU, 2 * D_LRU)),
                 full((1, D_LRU)), full((1, D_LRU)), full((1, D_LRU))]
    args += [perm, perm.T, cw, cb, wg, br, bi, lam]
    return pl.pallas_call(
        functools.partial(_lru_kernel, reverse=reverse, has_acc=has_acc, tt=tt),
        grid=(nb, nc),
        in_specs=in_specs,
        out_specs=row_spec,
        out_shape=jax.ShapeDtypeStruct((m, D_LRU), BF16),
        scratch_shapes=[pltpu.VMEM((tt + (LRU_CONV_WIDTH - 1) * SUBLANES, D_LRU), F32),
                        pltpu.VMEM((tt, D_LRU), F32),
                        pltpu.VMEM((tt, D_LRU), F32),
                        pltpu.VMEM(((LRU_CONV_WIDTH - 1) * SUBLANES, D_LRU), F32),
                        pltpu.VMEM((SUBLANES, D_LRU), F32)],
        compiler_params=pltpu.CompilerParams(
            dimension_semantics=("arbitrary", "arbitrary"), vmem_limit_bytes=VMEM_LIMIT),
        name="rglru_rev" if reverse else "rglru_fwd",
    )(*args)


def _mix_kernel(*refs, names, tq, local, final):
    r = dict(zip(names, refs))
    i = pl.program_id(1)
    nt = pl.num_programs(1)
    ext_ref, ycat_ref = r["ext"], r["ycat"]

    def glu(ref):
        t = ref[...].astype(F32)
        return t[:, :D_CONV] * _sigmoid(t[:, D_CONV:])

    ext_ref[CONV_HALO:CONV_HALO + tq, :] = glu(r["glu"])
    zero_halo = jnp.zeros((CONV_HALO, D_CONV), F32)
    if local:
        ext_ref[0:CONV_HALO, :] = jnp.where(i > 0, glu(r["glu_prev"]), zero_halo)
        ext_ref[CONV_HALO + tq:, :] = jnp.where(i < nt - 1, glu(r["glu_next"]), zero_halo)
    else:
        ext_ref[0:CONV_HALO, :] = zero_halo
        ext_ref[CONV_HALO + tq:, :] = zero_halo

    for rc in range(tq // CONV_ROWS):
        t0 = rc * CONV_ROWS
        rows = slice(t0, t0 + CONV_ROWS)
        pieces = []
        for cc in range(D_CONV // CONV_COLS):
            cols = slice(cc * CONV_COLS, (cc + 1) * CONV_COLS)
            acc = jnp.broadcast_to(r["dw_b"][:, cols], (CONV_ROWS, CONV_COLS))
            for b in range(SUBLANES):
                z = None
                for a in range(-(-(CONV_WIDTH + 1) // SUBLANES)):
                    o = SUBLANES * a + b
                    if o < 1 or o > CONV_WIDTH:
                        continue
                    term = (r["dw_w"][o - 1:o, cols]
                            * ext_ref[t0 + SUBLANES * a:t0 + SUBLANES * a + CONV_ROWS + SUBLANES,
                                      cols])
                    z = term if z is None else z + term
                acc = acc + z[b:b + CONV_ROWS, :]
            pieces.append(acc)
        acc = jnp.concatenate(pieces, axis=1)
        mu = jnp.mean(acc, axis=-1, keepdims=True)
        cen = acc - mu
        var = jnp.mean(cen * cen, axis=-1, keepdims=True)
        y = cen * lax.rsqrt(var + EPS) * r["ln_g"][...] + r["ln_b"][...]
        y = _silu(y).astype(BF16)
        y = jnp.dot(y, r["pw_w"][...], preferred_element_type=F32) + r["pw_b"][...]
        y = y * _silu(r["conv_gate"][rows, :].astype(F32))
        ycat_ref[rows, 0:D_CONV] = y.astype(BF16)

    ylru = r["ylru"][...].astype(F32)
    if "ylru_rev" in r:
        ylru = ylru + r["ylru_rev"][...].astype(F32)
    ycat_ref[:, D_CONV:D_CONV + D_LRU] = (
        ylru * _silu(r["lru_gate"][...].astype(F32))).astype(BF16)

    qrows = GQA_GROUP * WINDOW
    qi = lax.broadcasted_iota(jnp.int32, (qrows, WINDOW), 0) % WINDOW
    kj = lax.broadcasted_iota(jnp.int32, (qrows, WINDOW), 1)
    neg_inf = jnp.float32(-jnp.inf)
    dn = (((1,), (1,)), ((), ()))
    n_qb = tq // WINDOW
    lc = r["kc"].shape[0]
    for qb in range(n_qb):
        rows = slice(qb * WINDOW, (qb + 1) * WINDOW)
        for g in range(N_KV_HEADS):
            kv_cols = slice(g * HEAD_DIM, (g + 1) * HEAD_DIM)
            q_st = r["q"][qb, g * GQA_GROUP:(g + 1) * GQA_GROUP].reshape(qrows, HEAD_DIM)
            sink = jnp.concatenate(
                [jnp.broadcast_to(r["sink"][g * GQA_GROUP + hh:g * GQA_GROUP + hh + 1, 0:1],
                                  (WINDOW, 1)) for hh in range(GQA_GROUP)], axis=0) * LOG2E
            blocks = []
            for cb in range(lc // WINDOW):
                crow = slice(cb * WINDOW, (cb + 1) * WINDOW)
                s = lax.dot_general(q_st, r["kc"][crow, kv_cols], dn, preferred_element_type=F32)
                blocks.append((s, r["vc"][crow, kv_cols]))
            if local:
                if qb == 0:
                    k_p, v_p = r["k_prev"][:, kv_cols], r["v_prev"][:, kv_cols]
                    ok_p = i > 0
                else:
                    prows = slice((qb - 1) * WINDOW, qb * WINDOW)
                    k_p, v_p = r["k"][prows, kv_cols], r["v"][prows, kv_cols]
                    ok_p = True
                if qb == n_qb - 1:
                    k_n, v_n = r["k_next"][:, kv_cols], r["v_next"][:, kv_cols]
                    ok_n = i < nt - 1
                else:
                    nrows = slice((qb + 1) * WINDOW, (qb + 2) * WINDOW)
                    k_n, v_n = r["k"][nrows, kv_cols], r["v"][nrows, kv_cols]
                    ok_n = True
                s_p = lax.dot_general(q_st, k_p, dn, preferred_element_type=F32)
                s_c = lax.dot_general(q_st, r["k"][rows, kv_cols], dn, preferred_element_type=F32)
                s_n = lax.dot_general(q_st, k_n, dn, preferred_element_type=F32)
                s_p = jnp.where((kj >= qi) & ok_p, s_p, neg_inf)
                s_n = jnp.where((kj <= qi) & ok_n, s_n, neg_inf)
                blocks += [(s_p, v_p), (s_c, r["v"][rows, kv_cols]), (s_n, v_n)]
            m_el = blocks[0][0]
            for s, _ in blocks[1:]:
                m_el = jnp.maximum(m_el, s)
            m = jnp.maximum(jnp.max(m_el, axis=-1, keepdims=True), sink)
            den_el = None
            o = None
            for s, v in blocks:
                p = jnp.exp2(s - m)
                den_el = p if den_el is None else den_el + p
                pv = jnp.dot(p.astype(BF16), v, preferred_element_type=F32)
                o = pv if o is None else o + pv
            den = jnp.sum(den_el, axis=-1, keepdims=True) + jnp.exp2(sink - m)
            o = o / den
            for hh in range(GQA_GROUP):
                c0 = (g * GQA_GROUP + hh) * HEAD_DIM
                gate = _silu(r["attn_gate"][rows, c0:c0 + HEAD_DIM].astype(F32))
                ycat_ref[rows, D_CONV + D_LRU + c0:D_CONV + D_LRU + c0 + HEAD_DIM] = (
                    o[hh * WINDOW:(hh + 1) * WINDOW, :] * gate).astype(BF16)

    y = jnp.dot(ycat_ref[...], r["w_out"][...], preferred_element_type=F32)
    xn = r["x"][...] + r["gate"][...] * y
    if final:
        ms = jnp.mean(xn * xn, axis=-1, keepdims=True)
        xn = xn * lax.rsqrt(ms + EPS) * r["final_g"][...]
    r["out"][...] = xn


def _mix_call(x2, u2, q4, uc2, ylru, ada4, layer, ada_row_fn, wts, *, tq, rows_per_batch,
              ctx_rows, local, final):
    m, d = x2.shape
    nt = rows_per_batch // tq
    nb = m // rows_per_batch
    names, specs, args = [], [], []

    def add(name, arr, spec):
        names.append(name)
        specs.append(spec)
        args.append(arr)

    def rowblk(width, col_off):
        return pl.BlockSpec((tq, width), lambda b, i: (b * nt + i, col_off // width))

    def halo(rows, width, col_off, nxt):
        per = tq // rows
        last = m // rows - 1
        if nxt:
            fn = lambda b, i: (jnp.minimum((b * nt + i + 1) * per, last), col_off // width)
        else:
            fn = lambda b, i: (jnp.maximum((b * nt + i) * per - 1, 0), col_off // width)
        return pl.BlockSpec((rows, width), fn)

    full = lambda shape: pl.BlockSpec(shape, lambda b, i: (0,) * len(shape),
                                      pipeline_mode=pl.Buffered(1))

    add("x", x2, pl.BlockSpec((tq, d), lambda b, i: (b * nt + i, 0)))
    add("gate", ada4, pl.BlockSpec((None, None, 1, d), lambda b, i: (layer, ada_row_fn(b), 0, 2)))
    add("glu", u2, rowblk(2 * D_CONV, OFF_CONV_GLU))
    if local:
        add("glu_prev", u2, halo(CONV_HALO, 2 * D_CONV, OFF_CONV_GLU, False))
        add("glu_next", u2, halo(CONV_HALO, 2 * D_CONV, OFF_CONV_GLU, True))
    add("conv_gate", u2, rowblk(D_CONV, OFF_CONV_GATE))
    add("lru_gate", u2, rowblk(D_LRU, OFF_LRU_GATE))
    ylru_spec = pl.BlockSpec((tq, D_LRU), lambda b, i: (b * nt + i, 0))
    if isinstance(ylru, tuple):
        add("ylru", ylru[0], ylru_spec)
        add("ylru_rev", ylru[1], ylru_spec)
    else:
        add("ylru", ylru, ylru_spec)
    add("q", q4, pl.BlockSpec((tq // WINDOW, N_Q_HEADS, WINDOW, HEAD_DIM),
                              lambda b, i: (b * nt + i, 0, 0, 0)))
    add("attn_gate", u2, rowblk(D_ATTN, U_ATTN_GATE))
    if local:
        add("k", u2, rowblk(D_KV, OFF_K))
        add("v", u2, rowblk(D_KV, OFF_V))
        add("k_prev", u2, halo(WINDOW, D_KV, OFF_K, False))
        add("v_prev", u2, halo(WINDOW, D_KV, OFF_V, False))
        add("k_next", u2, halo(WINDOW, D_KV, OFF_K, True))
        add("v_next", u2, halo(WINDOW, D_KV, OFF_V, True))
    add("kc", uc2, pl.BlockSpec((ctx_rows, D_KV), lambda b, i: (b, OFF_K // D_KV)))
    add("vc", uc2, pl.BlockSpec((ctx_rows, D_KV), lambda b, i: (b, OFF_V // D_KV)))
    for name in ("dw_w", "dw_b", "ln_g", "ln_b", "pw_w", "pw_b", "sink"):
        add(name, wts[name], full(wts[name].shape))
    add("w_out", wts["w_out"], pl.BlockSpec((None,) + wts["w_out"].shape[1:],
                                            lambda b, i: (layer, 0, 0),
                                            pipeline_mode=pl.Buffered(1)))
    if final:
        add("final_g", wts["final_g"], full(wts["final_g"].shape))
    names += ["out", "ext", "ycat"]
    return pl.pallas_call(
        functools.partial(_mix_kernel, names=tuple(names), tq=tq, local=local, final=final),
        grid=(nb, nt),
        in_specs=specs,
        out_specs=pl.BlockSpec((tq, d), lambda b, i: (b * nt + i, 0)),
        out_shape=jax.ShapeDtypeStruct((m, d), F32),
        scratch_shapes=[pltpu.VMEM((tq + 2 * CONV_HALO, D_CONV), F32),
                        pltpu.VMEM((tq, D_MODEL), BF16)],
        compiler_params=pltpu.CompilerParams(
            dimension_semantics=("arbitrary", "arbitrary"), vmem_limit_bytes=VMEM_LIMIT),
        name="mix_lat" if local else "mix_ctx",
    )(*args)


def _rope_tables(seq):
    rows = seq // GRID_W
    row = jnp.repeat(jnp.arange(rows, dtype=F32), GRID_W)
    col = jnp.tile(jnp.arange(GRID_W, dtype=F32), rows)
    half = HEAD_DIM // 2
    inv = ROPE_BASE ** (-jnp.arange(0, half, 2, dtype=F32) / half)
    ang_r = row[:, None] * inv[None, :]
    ang_c = col[:, None] * inv[None, :]
    ang = jnp.concatenate([ang_r, ang_r, ang_c, ang_c], axis=-1)
    cos, sin = jnp.cos(ang), jnp.sin(ang)
    first = (jnp.arange(HEAD_DIM) % half) < (half // 2)
    return cos, jnp.where(first, -sin, 0.0), jnp.where(first, 0.0, sin)


def _block_diag(w):
    nblk, blk, _ = w.shape
    eye = jnp.eye(nblk, dtype=w.dtype)
    return (w[:, :, None, :] * eye[:, None, :, None]).reshape(nblk * blk, nblk * blk)


def _pick_tile(n, pref):
    t = min(n, pref)
    while n % t:
        t //= 2
    return t


def kernel(x, c, ctx, c_ctx, norm_g, w_ada, b_ada, w_in, dw_w, dw_b, ln_g, ln_b, pw_w, pw_b,
           lru_conv_w, lru_conv_b, lru_w_r, lru_b_r, lru_w_i, lru_b_i, lru_lam, attn_sink,
           w_out, final_g):
    nb, seq, d = x.shape
    lc = ctx.shape[1]
    depth = w_in.shape[0]
    ctx_row = nb

    c_rows = jnp.zeros((ADA_ROWS, d), F32).at[:nb].set(c).at[ctx_row].set(c_ctx)
    ada = _ada_call(c_rows, w_ada, b_ada)
    ada4 = ada.reshape(depth, ADA_ROWS, 1, 3 * d)
    rope_tabs = _rope_tables(seq)
    w_bf = w_in.astype(BF16)
    w_out_bf = w_out.astype(BF16)

    x2 = x.reshape(nb * seq, d)
    xc2 = ctx.reshape(nb * lc, d)
    tm = _pick_tile(seq, 1024)
    tq = _pick_tile(seq, 512)
    tt = _pick_tile(seq, 512)
    zeros_h0 = jnp.zeros((nb, 1, D_LRU), F32)

    for l in range(depth):
        last = l == depth - 1
        u2, q4 = _inproj_call(x2, norm_g[l], ada4, l, w_bf, rope_tabs, tm=tm, rows_per_batch=seq,
                              ada_row_fn=lambda b: b, with_q=True)
        uc2, qc4 = _inproj_call(xc2, norm_g[l], ada4, l, w_bf, None, tm=lc, rows_per_batch=lc,
                                ada_row_fn=lambda b: ctx_row, with_q=not last)

        ylru, ylru_c = None, ()
        for dr in range(2):
            rev = dr == 1
            wg = jnp.concatenate([_block_diag(lru_w_r[l, dr]), _block_diag(lru_w_i[l, dr])],
                                 axis=1).astype(BF16)
            params = (lru_conv_w[l, dr], lru_conv_b[l, dr].reshape(1, D_LRU), wg,
                      lru_b_r[l, dr].reshape(1, D_LRU), lru_b_i[l, dr].reshape(1, D_LRU),
                      lru_lam[l, dr].reshape(1, D_LRU))
            h_ctx = _lru_call(uc2, zeros_h0, None, params, reverse=rev, rows_per_batch=lc, tt=lc)
            edge = 0 if rev else lc - 1
            h0 = h_ctx.reshape(nb, lc, D_LRU)[:, edge:edge + 1, :].astype(F32)
            ylru = _lru_call(u2, h0, ylru, params, reverse=rev, rows_per_batch=seq, tt=tt)
            ylru_c += (h_ctx,)

        wts = {
            "dw_w": dw_w[l], "dw_b": dw_b[l].reshape(1, D_CONV),
            "ln_g": ln_g[l].reshape(1, D_CONV), "ln_b": ln_b[l].reshape(1, D_CONV),
            "pw_w": pw_w[l].astype(BF16), "pw_b": pw_b[l].reshape(1, D_CONV),
            "sink": jnp.broadcast_to(attn_sink[l].reshape(N_Q_HEADS, 1), (N_Q_HEADS, HEAD_DIM)),
            "w_out": w_out_bf, "final_g": final_g.reshape(1, d),
        }
        x2_new = _mix_call(x2, u2, q4, uc2, ylru, ada4, l, lambda b: b, wts, tq=tq,
                           rows_per_batch=seq, ctx_rows=lc, local=True, final=last)
        if not last:
            xc2 = _mix_call(xc2, uc2, qc4, uc2, ylru_c, ada4, l, lambda b: ctx_row, wts, tq=lc,
                            rows_per_batch=lc, ctx_rows=lc, local=False, final=False)
        x2 = x2_new
    return x2.reshape(nb, seq, d)
```

```python
import functools
import math

import jax
import jax.numpy as jnp
from jax import lax
from jax.experimental import pallas as pl
from jax.experimental.pallas import tpu as pltpu

F32 = jnp.float32
BF16 = jnp.bfloat16

D_MODEL = 2048
D_CONV = 512
D_LRU = 512
HEAD_DIM = 128
N_Q_HEADS = 8
N_KV_HEADS = 2
GQA_GROUP = N_Q_HEADS // N_KV_HEADS
D_ATTN = N_Q_HEADS * HEAD_DIM
D_KV = N_KV_HEADS * HEAD_DIM
GRID_W = 64
CONV_WIDTH = 31
CONV_PAD = (CONV_WIDTH - 1) // 2
LRU_CONV_WIDTH = 4
LRU_BLOCKS = 8
LRU_C = 8.0
WINDOW = 128
ROPE_BASE = 10000.0
EPS = 1e-6

OFF_K = D_LRU
OFF_V = OFF_K + D_KV
OFF_MEM_END = OFF_V + D_KV
OFF_CONV_GLU = OFF_MEM_END
OFF_CONV_GATE = OFF_CONV_GLU + 2 * D_CONV
OFF_LRU_GATE = OFF_CONV_GATE + D_CONV
OFF_Q = OFF_LRU_GATE + D_LRU
OFF_ATTN_GATE = OFF_Q + D_ATTN
D_IN = OFF_ATTN_GATE + D_ATTN
U_ATTN_GATE = OFF_Q
D_U = D_IN - D_ATTN

ADA_ROWS = 8
IN_TILE_N = 1024
IN_CHUNK_N = 256
Q_TILE = OFF_Q // IN_TILE_N
GLU_TILE = OFF_CONV_GLU // IN_TILE_N
CONV_HALO = 16
CONV_ROWS = 64
CONV_COLS = 256
SUBLANES = 8
LANES = 128
VMEM_LIMIT = 56 * 1024 * 1024
LOG2E = math.log2(math.e)
Q_SCALE = HEAD_DIM ** -0.5 * LOG2E

assert OFF_Q % IN_TILE_N == 0 and D_ATTN == IN_TILE_N and OFF_MEM_END == IN_TILE_N
assert OFF_CONV_GLU % IN_TILE_N == 0 and 2 * D_CONV == IN_TILE_N


def _sigmoid(x):
    return 0.5 * jnp.tanh(0.5 * x) + 0.5


def _silu(x):
    hx = 0.5 * x
    return hx * jnp.tanh(hx) + hx


def _ada_kernel(c_ref, w_ref, b_ref, o_ref):
    ca = _silu(c_ref[...])
    o_ref[...] = jnp.dot(ca.astype(BF16), w_ref[...].astype(BF16),
                         preferred_element_type=F32) + b_ref[...]


def _ada_call(c_rows, w_ada, b_ada):
    depth, d, n = w_ada.shape
    tn = 1024
    return pl.pallas_call(
        _ada_kernel,
        grid=(depth, n // tn),
        in_specs=[
            pl.BlockSpec((ADA_ROWS, d), lambda l, j: (0, 0)),
            pl.BlockSpec((None, d, tn), lambda l, j: (l, 0, j)),
            pl.BlockSpec((None, 1, tn), lambda l, j: (l, 0, j)),
        ],
        out_specs=pl.BlockSpec((None, ADA_ROWS, tn), lambda l, j: (l, 0, j)),
        out_shape=jax.ShapeDtypeStruct((depth, ADA_ROWS, n), F32),
        compiler_params=pltpu.CompilerParams(
            dimension_semantics=("arbitrary", "arbitrary"), vmem_limit_bytes=VMEM_LIMIT),
        name="ada_proj",
    )(c_rows, w_ada, b_ada.reshape(depth, 1, n))


def _rope(t, cos, sin_a, sin_b):
    return (t * cos + pltpu.roll(t, HEAD_DIM - 32, 1) * sin_a + pltpu.roll(t, 32, 1) * sin_b)


def _inproj_kernel(*refs, rope, with_q, tm):
    refs = list(refs)
    x_ref, g_ref, shift_ref, scale_ref, w_ref = refs[:5]
    del refs[:5]
    if rope:
        cos_ref, sa_ref, sb_ref = refs[:3]
        del refs[:3]
    o_ref = refs.pop(0)
    q_ref = refs.pop(0) if with_q else None
    h_ref = refs.pop(0)
    j = pl.program_id(1)

    @pl.when(j == 0)
    def _():
        x = x_ref[...]
        ms = jnp.mean(x * x, axis=-1, keepdims=True)
        gain = g_ref[...] * (1.0 + scale_ref[...])
        h_ref[...] = (x * lax.rsqrt(ms + EPS) * gain + shift_ref[...]).astype(BF16)

    chunks = [slice(c0, c0 + IN_CHUNK_N) for c0 in range(0, IN_TILE_N, IN_CHUNK_N)]
    heads_per_chunk = IN_CHUNK_N // HEAD_DIM

    def proj(cols):
        return jnp.dot(h_ref[...], w_ref[:, cols], preferred_element_type=F32)

    def rope_heads(u):
        if not rope:
            return [u[:, hh * HEAD_DIM:(hh + 1) * HEAD_DIM] for hh in range(heads_per_chunk)]
        return [_rope(u[:, hh * HEAD_DIM:(hh + 1) * HEAD_DIM], cos_ref[...], sa_ref[...],
                      sb_ref[...]) for hh in range(heads_per_chunk)]

    def plain_tile():
        for cols in chunks:
            o_ref[:, cols] = proj(cols).astype(BF16)

    def q_tile():
        for ci, cols in enumerate(chunks):
            for hh, t in enumerate(rope_heads(proj(cols))):
                t = (t * Q_SCALE).astype(BF16)
                for qb in range(tm // WINDOW):
                    q_ref[qb, ci * heads_per_chunk + hh] = t[qb * WINDOW:(qb + 1) * WINDOW, :]

    def kv_tile():
        for cols in chunks:
            u = proj(cols)
            if OFF_K <= cols.start < OFF_V:
                u = jnp.concatenate(rope_heads(u), axis=1)
            o_ref[:, cols] = u.astype(BF16)

    def glu_tile():
        half = len(chunks) // 2
        for ci in range(half):
            o_ref[:, chunks[ci]] = (proj(chunks[ci]) * _sigmoid(proj(chunks[ci + half]))
                                    ).astype(BF16)
            o_ref[:, chunks[ci + half]] = jnp.zeros((tm, IN_CHUNK_N), BF16)

    def silu_tile():
        for cols in chunks:
            o_ref[:, cols] = _silu(proj(cols)).astype(BF16)

    if not with_q:
        plain_tile()
    else:
        pl.when(j == 0)(kv_tile if rope else plain_tile)
        pl.when(j == GLU_TILE)(glu_tile)
        pl.when(j == Q_TILE)(q_tile)
        pl.when((j != 0) & (j != GLU_TILE) & (j != Q_TILE))(silu_tile)


def _inproj_call(x2, norm_g, ada4, layer, w_bf, rope_tabs, *, tm, rows_per_batch, ada_row_fn,
                 with_q):
    m, d = x2.shape
    tiles_per_batch = rows_per_batch // tm
    rope = rope_tabs is not None
    n_tiles = D_IN // IN_TILE_N if with_q else 1

    def ada_spec(part):
        return pl.BlockSpec((None, None, 1, d),
                            lambda i, j: (layer, ada_row_fn(i // tiles_per_batch), 0, part))

    in_specs = [
        pl.BlockSpec((tm, d), lambda i, j: (i, 0)),
        pl.BlockSpec((1, d), lambda i, j: (0, 0)),
        ada_spec(0), ada_spec(1),
        pl.BlockSpec((None, d, IN_TILE_N), lambda i, j: (layer, 0, j)),
    ]
    args = [x2, norm_g.reshape(1, d), ada4, ada4, w_bf]
    if rope:
        tab_spec = pl.BlockSpec((tm, HEAD_DIM), lambda i, j: (i % tiles_per_batch, 0))
        in_specs += [tab_spec] * 3
        args += list(rope_tabs)
    out_specs = [pl.BlockSpec((tm, IN_TILE_N), lambda i, j: (i, jnp.where(j > Q_TILE, j - 1, j)))]
    out_shape = [jax.ShapeDtypeStruct((m, D_U if with_q else IN_TILE_N), BF16)]
    if with_q:
        out_specs.append(pl.BlockSpec((tm // WINDOW, N_Q_HEADS, WINDOW, HEAD_DIM),
                                      lambda i, j: (i, 0, 0, 0)))
        out_shape.append(jax.ShapeDtypeStruct((m // WINDOW, N_Q_HEADS, WINDOW, HEAD_DIM), BF16))
    outs = pl.pallas_call(
        functools.partial(_inproj_kernel, rope=rope, with_q=with_q, tm=tm),
        grid=(m // tm, n_tiles),
        in_specs=in_specs,
        out_specs=out_specs,
        out_shape=out_shape,
        scratch_shapes=[pltpu.VMEM((tm, d), BF16)],
        compiler_params=pltpu.CompilerParams(
            dimension_semantics=("arbitrary", "arbitrary"), vmem_limit_bytes=VMEM_LIMIT),
        name="in_proj",
    )(*args)
    return (outs[0], outs[1]) if with_q else (outs[0], None)


def _lru_kernel(*refs, reverse, has_acc, tt, units):
    if has_acc:
        (x_ref, h0_ref, acc_ref, perm_ref, unperm_ref, cw_ref, cb_ref, wg_ref, br_ref, bi_ref,
         lam_ref, o_ref, ext3_ref, a3_ref, b3_ref, edge_ref, h_ref) = refs
    else:
        (x_ref, h0_ref, perm_ref, unperm_ref, cw_ref, cb_ref, wg_ref, br_ref, bi_ref,
         lam_ref, o_ref, ext3_ref, a3_ref, b3_ref, edge_ref, h_ref) = refs
        acc_ref = None
    c = pl.program_id(1)
    tu = tt // units
    seg = tu // SUBLANES
    hist = LRU_CONV_WIDTH - 1
    row = lax.broadcasted_iota(jnp.int32, (SUBLANES, D_LRU), 0)

    def blk(g):
        return slice(g * SUBLANES, (g + 1) * SUBLANES)

    @pl.when(c == 0)
    def _():
        edge_ref[...] = jnp.zeros((hist * SUBLANES, D_LRU), F32)
        h_ref[...] = jnp.broadcast_to(h0_ref[...], (SUBLANES, D_LRU))

    lam = lam_ref[...]
    softplus_neg_lam = jnp.maximum(-lam, 0.0) + jnp.log1p(jnp.exp(-jnp.abs(lam)))
    half_k = (-0.5 * LRU_C) * softplus_neg_lam
    half_br = 0.5 * br_ref[...]
    half_bi = 0.5 * bi_ref[...]

    order = list(range(units - 1, -1, -1) if reverse else range(units))
    for u in order:
        _lru_gates(slice(u * tu, (u + 1) * tu), x_ref, perm_ref, cw_ref, cb_ref, wg_ref, half_br,
                   half_bi, half_k, ext3_ref.at[u], a3_ref.at[u], b3_ref.at[u], edge_ref,
                   reverse=reverse, tt=tu, seg=seg, hist=hist, row=row, blk=blk)

    def step(t, carry):
        g_ = (seg - 1 - t) if reverse else t
        rows = pl.ds(pl.multiple_of(g_ * SUBLANES, SUBLANES), SUBLANES)
        new = []
        for u, (h, p) in zip(order, carry):
            a = a3_ref[u, rows, :]
            h = a * h + b3_ref[u, rows, :]
            p = a * p
            b3_ref[u, rows, :] = h
            a3_ref[u, rows, :] = p
            new.append((h, p))
        return tuple(new)

    init = tuple((jnp.zeros((SUBLANES, D_LRU), F32), jnp.ones((SUBLANES, D_LRU), F32))
                 for _ in order)
    finals = lax.fori_loop(0, seg, step, init, unroll=8)

    for u, (h_fin, p_fin) in zip(order, finals):
        _lru_finish(slice(u * tu, (u + 1) * tu), h_fin, p_fin, acc_ref, unperm_ref, o_ref,
                    a3_ref.at[u], b3_ref.at[u], h_ref, reverse=reverse, tt=tu, seg=seg, row=row)


def _lru_gates(rows_u, x_ref, perm_ref, cw_ref, cb_ref, wg_ref, half_br, half_bi, half_k, ext_ref,
               a_ref, b_ref, edge_ref, *, reverse, tt, seg, hist, row, blk):
    base = 0 if reverse else hist
    ext_ref[base * SUBLANES:base * SUBLANES + tt, :] = jnp.dot(
        perm_ref[...], x_ref[rows_u, :], preferred_element_type=F32)
    for j in range(1, hist + 1):
        if reverse:
            rolled = pltpu.roll(ext_ref[blk(j - 1), :], SUBLANES - 1, 0)
            ext_ref[blk(seg + j - 1), :] = jnp.where(row == SUBLANES - 1, edge_ref[blk(j - 1), :],
                                                     rolled)
        else:
            rolled = pltpu.roll(ext_ref[blk(hist + seg - j), :], 1, 0)
            ext_ref[blk(hist - j), :] = jnp.where(row == 0, edge_ref[blk(j - 1), :], rolled)
        edge_ref[blk(j - 1), :] = rolled

    xc = jnp.broadcast_to(cb_ref[...], (tt, D_LRU))
    for k in range(LRU_CONV_WIDTH):
        off = (hist - k) if reverse else k
        xc = xc + cw_ref[k:k + 1, :] * ext_ref[off * SUBLANES:off * SUBLANES + tt, :]

    g = jnp.dot(xc.astype(BF16), wg_ref[...], preferred_element_type=F32)
    t_r = jnp.tanh(g[:, :D_LRU] + half_br)
    t_i = jnp.tanh(g[:, D_LRU:] + half_bi)
    log_a = half_k * t_r + half_k
    a_ref[...] = jnp.exp(log_a)
    th = jnp.tanh(log_a)
    one_minus_a2 = (-2.0 * th) / (1.0 - th)
    half_xc = 0.5 * xc
    b_ref[...] = jnp.sqrt(one_minus_a2) * (half_xc * t_i + half_xc)


def _lru_finish(rows_u, h_fin, p_fin, acc_ref, unperm_ref, o_ref, a_ref, b_ref, h_ref, *, reverse,
                tt, seg, row):
    has_acc = acc_ref is not None
    a, b = p_fin, h_fin
    for d in (1, 2, 4):
        keep = (row < SUBLANES - d) if reverse else (row >= d)
        shift = (SUBLANES - d) if reverse else d
        a_s = jnp.where(keep, pltpu.roll(a, shift, 0), 1.0)
        b_s = jnp.where(keep, pltpu.roll(b, shift, 0), 0.0)
        b = a * b_s + b
        a = a * a_s
    h_in = h_ref[...]
    end = a * h_in + b
    if reverse:
        carry_in = jnp.where(row < SUBLANES - 1, pltpu.roll(end, SUBLANES - 1, 0), h_in)
        h_ref[...] = jnp.broadcast_to(end[0:1, :], (SUBLANES, D_LRU))
    else:
        carry_in = jnp.where(row >= 1, pltpu.roll(end, 1, 0), h_in)
        h_ref[...] = jnp.broadcast_to(end[SUBLANES - 1:SUBLANES, :], (SUBLANES, D_LRU))

    h_all = (b_ref[...].reshape(seg, SUBLANES, D_LRU)
             + a_ref[...].reshape(seg, SUBLANES, D_LRU) * carry_in[None]).reshape(tt, D_LRU)
    out = jnp.dot(unperm_ref[...], h_all.astype(BF16), preferred_element_type=F32)
    if has_acc:
        out = out + acc_ref[rows_u, :].astype(F32)
    o_ref[rows_u, :] = out.astype(o_ref.dtype)


def _lru_call(u2, h0, acc, params, *, reverse, rows_per_batch, tt, units):
    cw, cb, wg_half, br, bi, lam = params
    m = u2.shape[0]
    nb = m // rows_per_batch
    nc = rows_per_batch // tt
    has_acc = acc is not None
    tu = tt // units

    def chunk(c):
        return (nc - 1 - c) if reverse else c

    row_spec = pl.BlockSpec((tt, D_LRU), lambda b, c: (b * nc + chunk(c), 0))
    full = lambda shape: pl.BlockSpec(shape, lambda b, c: (0,) * len(shape))
    in_specs = [row_spec, pl.BlockSpec((None, 1, D_LRU), lambda b, c: (b, 0, 0))]
    args = [u2, h0]
    if has_acc:
        in_specs.append(row_spec)
        args.append(acc)
    seg = tu // SUBLANES
    src = (jnp.arange(tu) % SUBLANES) * seg + jnp.arange(tu) // SUBLANES
    perm = (src[:, None] == jnp.arange(tu)[None, :]).astype(BF16)
    in_specs += [full((tu, tu)), full((tu, tu)),
                 full((LRU_CONV_WIDTH, D_LRU)), full((1, D_LRU)), full((D_LRU, 2 * D_LRU)),
                 full((1, D_LRU)), full((1, D_LRU)), full((1, D_LRU))]
    args += [perm, perm.T, cw, cb, wg_half, br, bi, lam]
    edge_rows = (LRU_CONV_WIDTH - 1) * SUBLANES
    return pl.pallas_call(
        functools.partial(_lru_kernel, reverse=reverse, has_acc=has_acc, tt=tt, units=units),
        grid=(nb, nc),
        in_specs=in_specs,
        out_specs=row_spec,
        out_shape=jax.ShapeDtypeStruct((m, D_LRU), BF16),
        scratch_shapes=[pltpu.VMEM((units, tu + edge_rows, D_LRU), F32),
                        pltpu.VMEM((units, tu, D_LRU), F32),
                        pltpu.VMEM((units, tu, D_LRU), F32),
                        pltpu.VMEM((edge_rows, D_LRU), F32),
                        pltpu.VMEM((SUBLANES, D_LRU), F32)],
        compiler_params=pltpu.CompilerParams(
            dimension_semantics=("arbitrary", "arbitrary"), vmem_limit_bytes=VMEM_LIMIT),
        name="rglru_rev" if reverse else "rglru_fwd",
    )(*args)


def _mix_kernel(*refs, names, tq, local, final):
    r = dict(zip(names, refs))
    i = pl.program_id(1)
    nt = pl.num_programs(1)
    ext_ref, ycat_ref = r["ext"], r["ycat"]

    def glu(ref):
        return ref[...].astype(F32)

    ext_ref[CONV_HALO:CONV_HALO + tq, :] = glu(r["glu"])
    zero_halo = jnp.zeros((CONV_HALO, D_CONV), F32)
    if local:
        ext_ref[0:CONV_HALO, :] = jnp.where(i > 0, glu(r["glu_prev"]), zero_halo)
        ext_ref[CONV_HALO + tq:, :] = jnp.where(i < nt - 1, glu(r["glu_next"]), zero_halo)
    else:
        ext_ref[0:CONV_HALO, :] = zero_halo
        ext_ref[CONV_HALO + tq:, :] = zero_halo

    for rc in range(tq // CONV_ROWS):
        t0 = rc * CONV_ROWS
        rows = slice(t0, t0 + CONV_ROWS)
        pieces = []
        for cc in range(D_CONV // CONV_COLS):
            cols = slice(cc * CONV_COLS, (cc + 1) * CONV_COLS)
            acc = jnp.broadcast_to(r["dw_b"][:, cols], (CONV_ROWS, CONV_COLS))
            for b in range(SUBLANES):
                z = None
                for a in range(-(-(CONV_WIDTH + 1) // SUBLANES)):
                    o = SUBLANES * a + b
                    if o < 1 or o > CONV_WIDTH:
                        continue
                    term = (r["dw_w"][o - 1:o, cols]
                            * ext_ref[t0 + SUBLANES * a:t0 + SUBLANES * a + CONV_ROWS + SUBLANES,
                                      cols])
                    z = term if z is None else z + term
                acc = acc + z[b:b + CONV_ROWS, :]
            pieces.append(acc)
        acc = jnp.concatenate(pieces, axis=1)
        mu = jnp.mean(acc, axis=-1, keepdims=True)
        cen = acc - mu
        var = jnp.mean(cen * cen, axis=-1, keepdims=True)
        y = cen * lax.rsqrt(var + EPS) * r["ln_g"][...] + r["ln_b"][...]
        y = _silu(y).astype(BF16)
        y = jnp.dot(y, r["pw_w"][...], preferred_element_type=F32) + r["pw_b"][...]
        y = y * r["conv_gate"][rows, :].astype(F32)
        ycat_ref[rows, 0:D_CONV] = y.astype(BF16)

    ylru = r["ylru"][...].astype(F32)
    if "ylru_rev" in r:
        ylru = ylru + r["ylru_rev"][...].astype(F32)
    ycat_ref[:, D_CONV:D_CONV + D_LRU] = (
        ylru * r["lru_gate"][...].astype(F32)).astype(BF16)

    qrows = GQA_GROUP * WINDOW
    qi = lax.broadcasted_iota(jnp.int32, (qrows, WINDOW), 0) % WINDOW
    kj = lax.broadcasted_iota(jnp.int32, (qrows, WINDOW), 1)
    neg_inf = jnp.float32(-jnp.inf)
    dn = (((1,), (1,)), ((), ()))
    n_qb = tq // WINDOW
    lc = r["kc"].shape[0]
    for qb in range(n_qb):
        rows = slice(qb * WINDOW, (qb + 1) * WINDOW)
        for g in range(N_KV_HEADS):
            kv_cols = slice(g * HEAD_DIM, (g + 1) * HEAD_DIM)
            q_st = r["q"][qb, g * GQA_GROUP:(g + 1) * GQA_GROUP].reshape(qrows, HEAD_DIM)
            sink = jnp.concatenate(
                [jnp.broadcast_to(r["sink"][g * GQA_GROUP + hh:g * GQA_GROUP + hh + 1, 0:1],
                                  (WINDOW, 1)) for hh in range(GQA_GROUP)], axis=0) * LOG2E
            blocks = []
            for cb in range(lc // WINDOW):
                crow = slice(cb * WINDOW, (cb + 1) * WINDOW)
                s = lax.dot_general(q_st, r["kc"][crow, kv_cols], dn, preferred_element_type=F32)
                blocks.append((s, r["vc"][crow, kv_cols]))
            if local:
                if qb == 0:
                    k_p, v_p = r["k_prev"][:, kv_cols], r["v_prev"][:, kv_cols]
                    ok_p = i > 0
                else:
                    prows = slice((qb - 1) * WINDOW, qb * WINDOW)
                    k_p, v_p = r["k"][prows, kv_cols], r["v"][prows, kv_cols]
                    ok_p = True
                if qb == n_qb - 1:
                    k_n, v_n = r["k_next"][:, kv_cols], r["v_next"][:, kv_cols]
                    ok_n = i < nt - 1
                else:
                    nrows = slice((qb + 1) * WINDOW, (qb + 2) * WINDOW)
                    k_n, v_n = r["k"][nrows, kv_cols], r["v"][nrows, kv_cols]
                    ok_n = True
                s_p = lax.dot_general(q_st, k_p, dn, preferred_element_type=F32)
                s_c = lax.dot_general(q_st, r["k"][rows, kv_cols], dn, preferred_element_type=F32)
                s_n = lax.dot_general(q_st, k_n, dn, preferred_element_type=F32)
                s_p = jnp.where((kj >= qi) & ok_p, s_p, neg_inf)
                s_n = jnp.where((kj <= qi) & ok_n, s_n, neg_inf)
                blocks += [(s_p, v_p), (s_c, r["v"][rows, kv_cols]), (s_n, v_n)]
            m_el = blocks[0][0]
            for s, _ in blocks[1:]:
                m_el = jnp.maximum(m_el, s)
            m = jnp.maximum(jnp.max(m_el, axis=-1, keepdims=True), sink)
            den_el = None
            o = None
            for s, v in blocks:
                p = jnp.exp2(s - m)
                den_el = p if den_el is None else den_el + p
                pv = jnp.dot(p.astype(BF16), v, preferred_element_type=F32)
                o = pv if o is None else o + pv
            den = jnp.sum(den_el, axis=-1, keepdims=True) + jnp.exp2(sink - m)
            o = o / den
            for hh in range(GQA_GROUP):
                c0 = (g * GQA_GROUP + hh) * HEAD_DIM
                gate = r["attn_gate"][rows, c0:c0 + HEAD_DIM].astype(F32)
                ycat_ref[rows, D_CONV + D_LRU + c0:D_CONV + D_LRU + c0 + HEAD_DIM] = (
                    o[hh * WINDOW:(hh + 1) * WINDOW, :] * gate).astype(BF16)

    y = jnp.dot(ycat_ref[...], r["w_out"][...], preferred_element_type=F32)
    xn = r["x"][...] + r["gate"][...] * y
    if final:
        ms = jnp.mean(xn * xn, axis=-1, keepdims=True)
        xn = xn * lax.rsqrt(ms + EPS) * r["final_g"][...]
    r["out"][...] = xn


def _mix_call(x2, u2, q4, uc2, ylru, ada4, layer, ada_row_fn, wts, *, tq, rows_per_batch,
              ctx_rows, local, final):
    m, d = x2.shape
    nt = rows_per_batch // tq
    nb = m // rows_per_batch
    names, specs, args = [], [], []

    def add(name, arr, spec):
        names.append(name)
        specs.append(spec)
        args.append(arr)

    def rowblk(width, col_off):
        return pl.BlockSpec((tq, width), lambda b, i: (b * nt + i, col_off // width))

    def halo(rows, width, col_off, nxt):
        per = tq // rows
        last = m // rows - 1
        if nxt:
            fn = lambda b, i: (jnp.minimum((b * nt + i + 1) * per, last), col_off // width)
        else:
            fn = lambda b, i: (jnp.maximum((b * nt + i) * per - 1, 0), col_off // width)
        return pl.BlockSpec((rows, width), fn)

    full = lambda shape: pl.BlockSpec(shape, lambda b, i: (0,) * len(shape),
                                      pipeline_mode=pl.Buffered(1))

    add("x", x2, pl.BlockSpec((tq, d), lambda b, i: (b * nt + i, 0)))
    add("gate", ada4, pl.BlockSpec((None, None, 1, d), lambda b, i: (layer, ada_row_fn(b), 0, 2)))
    add("glu", u2, rowblk(D_CONV, OFF_CONV_GLU))
    if local:
        add("glu_prev", u2, halo(CONV_HALO, D_CONV, OFF_CONV_GLU, False))
        add("glu_next", u2, halo(CONV_HALO, D_CONV, OFF_CONV_GLU, True))
    add("conv_gate", u2, rowblk(D_CONV, OFF_CONV_GATE))
    add("lru_gate", u2, rowblk(D_LRU, OFF_LRU_GATE))
    ylru_spec = pl.BlockSpec((tq, D_LRU), lambda b, i: (b * nt + i, 0))
    if isinstance(ylru, tuple):
        add("ylru", ylru[0], ylru_spec)
        add("ylru_rev", ylru[1], ylru_spec)
    else:
        add("ylru", ylru, ylru_spec)
    add("q", q4, pl.BlockSpec((tq // WINDOW, N_Q_HEADS, WINDOW, HEAD_DIM),
                              lambda b, i: (b * nt + i, 0, 0, 0)))
    add("attn_gate", u2, rowblk(D_ATTN, U_ATTN_GATE))
    if local:
        add("k", u2, rowblk(D_KV, OFF_K))
        add("v", u2, rowblk(D_KV, OFF_V))
        add("k_prev", u2, halo(WINDOW, D_KV, OFF_K, False))
        add("v_prev", u2, halo(WINDOW, D_KV, OFF_V, False))
        add("k_next", u2, halo(WINDOW, D_KV, OFF_K, True))
        add("v_next", u2, halo(WINDOW, D_KV, OFF_V, True))
    add("kc", uc2, pl.BlockSpec((ctx_rows, D_KV), lambda b, i: (b, OFF_K // D_KV)))
    add("vc", uc2, pl.BlockSpec((ctx_rows, D_KV), lambda b, i: (b, OFF_V // D_KV)))
    for name in ("dw_w", "dw_b", "ln_g", "ln_b", "pw_w", "pw_b", "sink"):
        add(name, wts[name], full(wts[name].shape))
    add("w_out", wts["w_out"], pl.BlockSpec((None,) + wts["w_out"].shape[1:],
                                            lambda b, i: (layer, 0, 0),
                                            pipeline_mode=pl.Buffered(1)))
    if final:
        add("final_g", wts["final_g"], full(wts["final_g"].shape))
    names += ["out", "ext", "ycat"]
    return pl.pallas_call(
        functools.partial(_mix_kernel, names=tuple(names), tq=tq, local=local, final=final),
        grid=(nb, nt),
        in_specs=specs,
        out_specs=pl.BlockSpec((tq, d), lambda b, i: (b * nt + i, 0)),
        out_shape=jax.ShapeDtypeStruct((m, d), F32),
        scratch_shapes=[pltpu.VMEM((tq + 2 * CONV_HALO, D_CONV), F32),
                        pltpu.VMEM((tq, D_MODEL), BF16)],
        compiler_params=pltpu.CompilerParams(
            dimension_semantics=("arbitrary", "arbitrary"), vmem_limit_bytes=VMEM_LIMIT),
        name="mix_lat" if local else "mix_ctx",
    )(*args)


def _rope_tables(seq):
    rows = seq // GRID_W
    row = jnp.repeat(jnp.arange(rows, dtype=F32), GRID_W)
    col = jnp.tile(jnp.arange(GRID_W, dtype=F32), rows)
    half = HEAD_DIM // 2
    inv = ROPE_BASE ** (-jnp.arange(0, half, 2, dtype=F32) / half)
    ang_r = row[:, None] * inv[None, :]
    ang_c = col[:, None] * inv[None, :]
    ang = jnp.concatenate([ang_r, ang_r, ang_c, ang_c], axis=-1)
    cos, sin = jnp.cos(ang), jnp.sin(ang)
    first = (jnp.arange(HEAD_DIM) % half) < (half // 2)
    return cos, jnp.where(first, -sin, 0.0), jnp.where(first, 0.0, sin)


def _block_diag(w):
    nblk, blk, _ = w.shape
    eye = jnp.eye(nblk, dtype=w.dtype)
    return (w[:, :, None, :] * eye[:, None, :, None]).reshape(nblk * blk, nblk * blk)


def _pick_tile(n, pref):
    t = min(n, pref)
    while n % t:
        t //= 2
    return t


def kernel(x, c, ctx, c_ctx, norm_g, w_ada, b_ada, w_in, dw_w, dw_b, ln_g, ln_b, pw_w, pw_b,
           lru_conv_w, lru_conv_b, lru_w_r, lru_b_r, lru_w_i, lru_b_i, lru_lam, attn_sink,
           w_out, final_g):
    nb, seq, d = x.shape
    lc = ctx.shape[1]
    depth = w_in.shape[0]
    ctx_row = nb

    c_rows = jnp.zeros((ADA_ROWS, d), F32).at[:nb].set(c).at[ctx_row].set(c_ctx)
    ada = _ada_call(c_rows, w_ada, b_ada)
    ada4 = ada.reshape(depth, ADA_ROWS, 1, 3 * d)
    rope_tabs = _rope_tables(seq)
    w_bf = w_in.astype(BF16)
    w_out_bf = w_out.astype(BF16)

    x2 = x.reshape(nb * seq, d)
    xc2 = ctx.reshape(nb * lc, d)
    tm = _pick_tile(seq, 1024)
    tq = _pick_tile(seq, 512)
    tt = _pick_tile(seq, 1024)
    tu = _pick_tile(tt, 512)
    zeros_h0 = jnp.zeros((nb, 1, D_LRU), F32)

    for l in range(depth):
        last = l == depth - 1
        u2, q4 = _inproj_call(x2, norm_g[l], ada4, l, w_bf, rope_tabs, tm=tm, rows_per_batch=seq,
                              ada_row_fn=lambda b: b, with_q=True)
        uc2, qc4 = _inproj_call(xc2, norm_g[l], ada4, l, w_bf, None, tm=lc, rows_per_batch=lc,
                                ada_row_fn=lambda b: ctx_row, with_q=not last)

        ylru, ylru_c = None, ()
        for dr in range(2):
            rev = dr == 1
            wg_half = (0.5 * jnp.concatenate(
                [_block_diag(lru_w_r[l, dr]), _block_diag(lru_w_i[l, dr])], axis=1)).astype(BF16)
            params = (lru_conv_w[l, dr], lru_conv_b[l, dr].reshape(1, D_LRU), wg_half,
                      lru_b_r[l, dr].reshape(1, D_LRU), lru_b_i[l, dr].reshape(1, D_LRU),
                      lru_lam[l, dr].reshape(1, D_LRU))
            h_ctx = _lru_call(uc2, zeros_h0, None, params, reverse=rev, rows_per_batch=lc, tt=lc,
                              units=1)
            edge = 0 if rev else lc - 1
            h0 = h_ctx.reshape(nb, lc, D_LRU)[:, edge:edge + 1, :].astype(F32)
            ylru = _lru_call(u2, h0, ylru, params, reverse=rev, rows_per_batch=seq, tt=tt,
                             units=tt // tu)
            ylru_c += (h_ctx,)

        wts = {
            "dw_w": dw_w[l], "dw_b": dw_b[l].reshape(1, D_CONV),
            "ln_g": ln_g[l].reshape(1, D_CONV), "ln_b": ln_b[l].reshape(1, D_CONV),
            "pw_w": pw_w[l].astype(BF16), "pw_b": pw_b[l].reshape(1, D_CONV),
            "sink": jnp.broadcast_to(attn_sink[l].reshape(N_Q_HEADS, 1), (N_Q_HEADS, HEAD_DIM)),
            "w_out": w_out_bf, "final_g": final_g.reshape(1, d),
        }
        x2_new = _mix_call(x2, u2, q4, uc2, ylru, ada4, l, lambda b: b, wts, tq=tq,
                           rows_per_batch=seq, ctx_rows=lc, local=True, final=last)
        if not last:
            xc2 = _mix_call(xc2, uc2, qc4, uc2, ylru_c, ada4, l, lambda b: ctx_row, wts, tq=lc,
                            rows_per_batch=lc, ctx_rows=lc, local=False, final=False)
        x2 = x2_new
    return x2.reshape(nb, seq, d)
```

```python
import functools
import math

import jax
import jax.numpy as jnp
import numpy as np
from jax import lax
from jax.experimental import pallas as pl
from jax.experimental.pallas import tpu as pltpu

F32 = jnp.float32
BF16 = jnp.bfloat16

D_MODEL = 2048
D_CONV = 512
D_LRU = 512
HEAD_DIM = 128
N_Q_HEADS = 8
N_KV_HEADS = 2
GQA_GROUP = N_Q_HEADS // N_KV_HEADS
D_ATTN = N_Q_HEADS * HEAD_DIM
D_KV = N_KV_HEADS * HEAD_DIM
GRID_W = 64
CONV_WIDTH = 31
CONV_PAD = (CONV_WIDTH - 1) // 2
LRU_CONV_WIDTH = 4
LRU_BLOCKS = 8
LRU_C = 8.0
WINDOW = 128
ROPE_BASE = 10000.0
EPS = 1e-6

OFF_K = D_LRU
OFF_V = OFF_K + D_KV
OFF_MEM_END = OFF_V + D_KV
OFF_CONV_GLU = OFF_MEM_END
OFF_CONV_GATE = OFF_CONV_GLU + 2 * D_CONV
OFF_LRU_GATE = OFF_CONV_GATE + D_CONV
OFF_Q = OFF_LRU_GATE + D_LRU
OFF_ATTN_GATE = OFF_Q + D_ATTN
D_IN = OFF_ATTN_GATE + D_ATTN
U_ATTN_GATE = OFF_Q
D_U = D_IN - D_ATTN

ADA_ROWS = 8
IN_TILE_N = 1024
IN_CHUNK_N = 256
Q_TILE = OFF_Q // IN_TILE_N
GLU_TILE = OFF_CONV_GLU // IN_TILE_N
OUT_CHUNK_N = 256
CONV_HALO = 16
CONV_ROWS = 64
CONV_COLS = 256
SUBLANES = 8
LANES = 128
VMEM_LIMIT = 56 * 1024 * 1024
LOG2E = math.log2(math.e)
Q_SCALE = HEAD_DIM ** -0.5 * LOG2E

assert OFF_Q % IN_TILE_N == 0 and D_ATTN == IN_TILE_N and OFF_MEM_END == IN_TILE_N
assert OFF_CONV_GLU % IN_TILE_N == 0 and 2 * D_CONV == IN_TILE_N


def _sigmoid(x):
    return 0.5 * jnp.tanh(0.5 * x) + 0.5


def _silu(x):
    hx = 0.5 * x
    return hx * jnp.tanh(hx) + hx


def _ada_kernel(c_ref, w_ref, b_ref, o_ref):
    ca = _silu(c_ref[...])
    o_ref[...] = jnp.dot(ca.astype(BF16), w_ref[...].astype(BF16),
                         preferred_element_type=F32) + b_ref[...]


def _ada_call(c_rows, w_ada, b_ada):
    depth, d, n = w_ada.shape
    tn = 1024
    return pl.pallas_call(
        _ada_kernel,
        grid=(depth, n // tn),
        in_specs=[
            pl.BlockSpec((ADA_ROWS, d), lambda l, j: (0, 0)),
            pl.BlockSpec((None, d, tn), lambda l, j: (l, 0, j)),
            pl.BlockSpec((None, 1, tn), lambda l, j: (l, 0, j)),
        ],
        out_specs=pl.BlockSpec((None, ADA_ROWS, tn), lambda l, j: (l, 0, j)),
        out_shape=jax.ShapeDtypeStruct((depth, ADA_ROWS, n), F32),
        compiler_params=pltpu.CompilerParams(
            dimension_semantics=("arbitrary", "arbitrary"), vmem_limit_bytes=VMEM_LIMIT),
        name="ada_proj",
    )(c_rows, w_ada, b_ada.reshape(depth, 1, n))


def _rope(t, cos, sin_a, sin_b):
    return (t * cos + pltpu.roll(t, HEAD_DIM - 32, 1) * sin_a + pltpu.roll(t, 32, 1) * sin_b)


def _inproj_kernel(*refs, rope, with_q, tm):
    refs = list(refs)
    x_ref, g_ref, shift_ref, scale_ref, w_ref = refs[:5]
    del refs[:5]
    if rope:
        cos_ref, sa_ref, sb_ref = refs[:3]
        del refs[:3]
    o_ref = refs.pop(0)
    q_ref = refs.pop(0) if with_q else None
    h_ref = refs.pop(0)
    j = pl.program_id(1)

    @pl.when(j == 0)
    def _():
        x = x_ref[...]
        ms = jnp.mean(x * x, axis=-1, keepdims=True)
        gain = g_ref[...] * (1.0 + scale_ref[...])
        h_ref[...] = (x * lax.rsqrt(ms + EPS) * gain + shift_ref[...]).astype(BF16)

    chunks = [slice(c0, c0 + IN_CHUNK_N) for c0 in range(0, IN_TILE_N, IN_CHUNK_N)]
    heads_per_chunk = IN_CHUNK_N // HEAD_DIM

    def proj(cols):
        return jnp.dot(h_ref[...], w_ref[:, cols], preferred_element_type=F32)

    def rope_heads(u):
        if not rope:
            return [u[:, hh * HEAD_DIM:(hh + 1) * HEAD_DIM] for hh in range(heads_per_chunk)]
        return [_rope(u[:, hh * HEAD_DIM:(hh + 1) * HEAD_DIM], cos_ref[...], sa_ref[...],
                      sb_ref[...]) for hh in range(heads_per_chunk)]

    def plain_tile():
        for cols in chunks:
            o_ref[:, cols] = proj(cols).astype(BF16)

    def q_tile():
        for ci, cols in enumerate(chunks):
            for hh, t in enumerate(rope_heads(proj(cols))):
                t = (t * Q_SCALE).astype(BF16)
                for qb in range(tm // WINDOW):
                    q_ref[qb, ci * heads_per_chunk + hh] = t[qb * WINDOW:(qb + 1) * WINDOW, :]

    def kv_tile():
        for cols in chunks:
            u = proj(cols)
            if OFF_K <= cols.start < OFF_V:
                u = jnp.concatenate(rope_heads(u), axis=1)
            o_ref[:, cols] = u.astype(BF16)

    def glu_tile():
        half = len(chunks) // 2
        for ci in range(half):
            o_ref[:, chunks[ci]] = (proj(chunks[ci]) * _sigmoid(proj(chunks[ci + half]))
                                    ).astype(BF16)
            o_ref[:, chunks[ci + half]] = jnp.zeros((tm, IN_CHUNK_N), BF16)

    def silu_tile():
        for cols in chunks:
            o_ref[:, cols] = _silu(proj(cols)).astype(BF16)

    if not with_q:
        plain_tile()
    else:
        pl.when(j == 0)(kv_tile if rope else plain_tile)
        pl.when(j == GLU_TILE)(glu_tile)
        pl.when(j == Q_TILE)(q_tile)
        pl.when((j != 0) & (j != GLU_TILE) & (j != Q_TILE))(silu_tile)


def _inproj_call(x2, norm_g, ada4, layer, w_bf, rope_tabs, *, tm, rows_per_batch, ada_row_fn,
                 with_q):
    m, d = x2.shape
    tiles_per_batch = rows_per_batch // tm
    rope = rope_tabs is not None
    n_tiles = D_IN // IN_TILE_N if with_q else 1

    def ada_spec(part):
        return pl.BlockSpec((None, None, 1, d),
                            lambda i, j: (layer, ada_row_fn(i // tiles_per_batch), 0, part))

    in_specs = [
        pl.BlockSpec((tm, d), lambda i, j: (i, 0)),
        pl.BlockSpec((None, 1, d), lambda i, j: (layer, 0, 0)),
        ada_spec(0), ada_spec(1),
        pl.BlockSpec((None, d, IN_TILE_N), lambda i, j: (layer, 0, j)),
    ]
    args = [x2, norm_g, ada4, ada4, w_bf]
    if rope:
        tab_spec = pl.BlockSpec((tm, HEAD_DIM), lambda i, j: (i % tiles_per_batch, 0))
        in_specs += [tab_spec] * 3
        args += list(rope_tabs)
    out_specs = [pl.BlockSpec((tm, IN_TILE_N), lambda i, j: (i, jnp.where(j > Q_TILE, j - 1, j)))]
    out_shape = [jax.ShapeDtypeStruct((m, D_U if with_q else IN_TILE_N), BF16)]
    if with_q:
        out_specs.append(pl.BlockSpec((tm // WINDOW, N_Q_HEADS, WINDOW, HEAD_DIM),
                                      lambda i, j: (i, 0, 0, 0)))
        out_shape.append(jax.ShapeDtypeStruct((m // WINDOW, N_Q_HEADS, WINDOW, HEAD_DIM), BF16))
    outs = pl.pallas_call(
        functools.partial(_inproj_kernel, rope=rope, with_q=with_q, tm=tm),
        grid=(m // tm, n_tiles),
        in_specs=in_specs,
        out_specs=out_specs,
        out_shape=out_shape,
        scratch_shapes=[pltpu.VMEM((tm, d), BF16)],
        compiler_params=pltpu.CompilerParams(
            dimension_semantics=("arbitrary", "arbitrary"), vmem_limit_bytes=VMEM_LIMIT),
        name="in_proj",
    )(*args)
    return (outs[0], outs[1]) if with_q else (outs[0], None)


def _lru_kernel(*refs, reverse, has_acc, tt, units):
    if has_acc:
        (x_ref, h0_ref, acc_ref, perm_ref, unperm_ref, cw_ref, cb_ref, wg_ref, br_ref, bi_ref,
         lam_ref, o_ref, ext3_ref, a3_ref, b3_ref, edge_ref, h_ref) = refs
    else:
        (x_ref, h0_ref, perm_ref, unperm_ref, cw_ref, cb_ref, wg_ref, br_ref, bi_ref,
         lam_ref, o_ref, ext3_ref, a3_ref, b3_ref, edge_ref, h_ref) = refs
        acc_ref = None
    c = pl.program_id(1)
    tu = tt // units
    seg = tu // SUBLANES
    hist = LRU_CONV_WIDTH - 1
    row = lax.broadcasted_iota(jnp.int32, (SUBLANES, D_LRU), 0)

    def blk(g):
        return slice(g * SUBLANES, (g + 1) * SUBLANES)

    @pl.when(c == 0)
    def _():
        edge_ref[...] = jnp.zeros((hist * SUBLANES, D_LRU), F32)
        h_ref[...] = jnp.broadcast_to(h0_ref[...], (SUBLANES, D_LRU))

    lam = lam_ref[...]
    softplus_neg_lam = jnp.maximum(-lam, 0.0) + jnp.log1p(jnp.exp(-jnp.abs(lam)))
    half_k = (-0.5 * LRU_C) * softplus_neg_lam
    half_br = 0.5 * br_ref[...]
    half_bi = 0.5 * bi_ref[...]

    order = list(range(units - 1, -1, -1) if reverse else range(units))
    for u in order:
        _lru_gates(slice(u * tu, (u + 1) * tu), x_ref, perm_ref, cw_ref, cb_ref, wg_ref, half_br,
                   half_bi, half_k, ext3_ref.at[u], a3_ref.at[u], b3_ref.at[u], edge_ref,
                   reverse=reverse, tt=tu, seg=seg, hist=hist, row=row, blk=blk)

    def step(t, carry):
        g_ = (seg - 1 - t) if reverse else t
        rows = pl.ds(pl.multiple_of(g_ * SUBLANES, SUBLANES), SUBLANES)
        new = []
        for u, (h, p) in zip(order, carry):
            a = a3_ref[u, rows, :]
            h = a * h + b3_ref[u, rows, :]
            p = a * p
            b3_ref[u, rows, :] = h
            a3_ref[u, rows, :] = p
            new.append((h, p))
        return tuple(new)

    init = tuple((jnp.zeros((SUBLANES, D_LRU), F32), jnp.ones((SUBLANES, D_LRU), F32))
                 for _ in order)
    finals = lax.fori_loop(0, seg, step, init, unroll=8)

    for u, (h_fin, p_fin) in zip(order, finals):
        _lru_finish(slice(u * tu, (u + 1) * tu), h_fin, p_fin, acc_ref, unperm_ref, o_ref,
                    a3_ref.at[u], b3_ref.at[u], h_ref, reverse=reverse, tt=tu, seg=seg, row=row)


def _lru_gates(rows_u, x_ref, perm_ref, cw_ref, cb_ref, wg_ref, half_br, half_bi, half_k, ext_ref,
               a_ref, b_ref, edge_ref, *, reverse, tt, seg, hist, row, blk):
    base = 0 if reverse else hist
    ext_ref[base * SUBLANES:base * SUBLANES + tt, :] = jnp.dot(
        perm_ref[...], x_ref[rows_u, :], preferred_element_type=F32)
    for j in range(1, hist + 1):
        if reverse:
            rolled = pltpu.roll(ext_ref[blk(j - 1), :], SUBLANES - 1, 0)
            ext_ref[blk(seg + j - 1), :] = jnp.where(row == SUBLANES - 1, edge_ref[blk(j - 1), :],
                                                     rolled)
        else:
            rolled = pltpu.roll(ext_ref[blk(hist + seg - j), :], 1, 0)
            ext_ref[blk(hist - j), :] = jnp.where(row == 0, edge_ref[blk(j - 1), :], rolled)
        edge_ref[blk(j - 1), :] = rolled

    xc = jnp.broadcast_to(cb_ref[...], (tt, D_LRU))
    for k in range(LRU_CONV_WIDTH):
        off = (hist - k) if reverse else k
        xc = xc + cw_ref[k:k + 1, :] * ext_ref[off * SUBLANES:off * SUBLANES + tt, :]

    g = jnp.dot(xc.astype(BF16), wg_ref[...], preferred_element_type=F32)
    t_r = jnp.tanh(g[:, :D_LRU] + half_br)
    t_i = jnp.tanh(g[:, D_LRU:] + half_bi)
    log_a = half_k * t_r + half_k
    a_ref[...] = jnp.exp(log_a)
    th = jnp.tanh(log_a)
    one_minus_a2 = (-2.0 * th) / (1.0 - th)
    half_xc = 0.5 * xc
    b_ref[...] = jnp.sqrt(one_minus_a2) * (half_xc * t_i + half_xc)


def _lru_finish(rows_u, h_fin, p_fin, acc_ref, unperm_ref, o_ref, a_ref, b_ref, h_ref, *, reverse,
                tt, seg, row):
    has_acc = acc_ref is not None
    a, b = p_fin, h_fin
    for d in (1, 2, 4):
        keep = (row < SUBLANES - d) if reverse else (row >= d)
        shift = (SUBLANES - d) if reverse else d
        a_s = jnp.where(keep, pltpu.roll(a, shift, 0), 1.0)
        b_s = jnp.where(keep, pltpu.roll(b, shift, 0), 0.0)
        b = a * b_s + b
        a = a * a_s
    h_in = h_ref[...]
    end = a * h_in + b
    if reverse:
        carry_in = jnp.where(row < SUBLANES - 1, pltpu.roll(end, SUBLANES - 1, 0), h_in)
        h_ref[...] = jnp.broadcast_to(end[0:1, :], (SUBLANES, D_LRU))
    else:
        carry_in = jnp.where(row >= 1, pltpu.roll(end, 1, 0), h_in)
        h_ref[...] = jnp.broadcast_to(end[SUBLANES - 1:SUBLANES, :], (SUBLANES, D_LRU))

    h_all = (b_ref[...].reshape(seg, SUBLANES, D_LRU)
             + a_ref[...].reshape(seg, SUBLANES, D_LRU) * carry_in[None]).reshape(tt, D_LRU)
    out = jnp.dot(unperm_ref[...], h_all.astype(BF16), preferred_element_type=F32)
    if has_acc:
        out = out + acc_ref[rows_u, :].astype(F32)
    o_ref[rows_u, :] = out.astype(o_ref.dtype)


def _lru_call(u2, h0, acc, params, layer, direction, *, rows_per_batch, tt, units):
    cw, cb, wg_half, br, bi, lam = params
    reverse = direction == 1
    m = u2.shape[0]
    nb = m // rows_per_batch
    nc = rows_per_batch // tt
    has_acc = acc is not None
    tu = tt // units

    def chunk(c):
        return (nc - 1 - c) if reverse else c

    row_spec = pl.BlockSpec((tt, D_LRU), lambda b, c: (b * nc + chunk(c), 0))
    full = lambda shape: pl.BlockSpec(shape, lambda b, c: (0,) * len(shape))
    in_specs = [row_spec, pl.BlockSpec((None, 1, D_LRU), lambda b, c: (b, 0, 0))]
    args = [u2, h0]
    if has_acc:
        in_specs.append(row_spec)
        args.append(acc)
    seg = tu // SUBLANES
    src = (np.arange(tu) % SUBLANES) * seg + np.arange(tu) // SUBLANES
    perm = jnp.asarray(src[:, None] == np.arange(tu)[None, :], BF16)
    sel = lambda shape: pl.BlockSpec((None, None) + shape,
                                     lambda b, c: (layer, direction) + (0,) * len(shape))
    in_specs += [full((tu, tu)), full((tu, tu)),
                 sel((LRU_CONV_WIDTH, D_LRU)), sel((1, D_LRU)), sel((D_LRU, 2 * D_LRU)),
                 sel((1, D_LRU)), sel((1, D_LRU)), sel((1, D_LRU))]
    args += [perm, perm.T, cw, cb, wg_half, br, bi, lam]
    edge_rows = (LRU_CONV_WIDTH - 1) * SUBLANES
    return pl.pallas_call(
        functools.partial(_lru_kernel, reverse=reverse, has_acc=has_acc, tt=tt, units=units),
        grid=(nb, nc),
        in_specs=in_specs,
        out_specs=row_spec,
        out_shape=jax.ShapeDtypeStruct((m, D_LRU), BF16),
        scratch_shapes=[pltpu.VMEM((units, tu + edge_rows, D_LRU), F32),
                        pltpu.VMEM((units, tu, D_LRU), F32),
                        pltpu.VMEM((units, tu, D_LRU), F32),
                        pltpu.VMEM((edge_rows, D_LRU), F32),
                        pltpu.VMEM((SUBLANES, D_LRU), F32)],
        compiler_params=pltpu.CompilerParams(
            dimension_semantics=("arbitrary", "arbitrary"), vmem_limit_bytes=VMEM_LIMIT),
        name="rglru_rev" if reverse else "rglru_fwd",
    )(*args)


def _mix_kernel(*refs, names, tq, local, final):
    r = dict(zip(names, refs))
    i = pl.program_id(1)
    nt = pl.num_programs(1)
    ext_ref, ycat_ref = r["ext"], r["ycat"]

    def glu(ref):
        return ref[...].astype(F32)

    ext_ref[CONV_HALO:CONV_HALO + tq, :] = glu(r["glu"])
    zero_halo = jnp.zeros((CONV_HALO, D_CONV), F32)
    if local:
        ext_ref[0:CONV_HALO, :] = jnp.where(i > 0, glu(r["glu_prev"]), zero_halo)
        ext_ref[CONV_HALO + tq:, :] = jnp.where(i < nt - 1, glu(r["glu_next"]), zero_halo)
    else:
        ext_ref[0:CONV_HALO, :] = zero_halo
        ext_ref[CONV_HALO + tq:, :] = zero_halo

    for rc in range(tq // CONV_ROWS):
        t0 = rc * CONV_ROWS
        rows = slice(t0, t0 + CONV_ROWS)
        pieces = []
        for cc in range(D_CONV // CONV_COLS):
            cols = slice(cc * CONV_COLS, (cc + 1) * CONV_COLS)
            acc = jnp.broadcast_to(r["dw_b"][:, cols], (CONV_ROWS, CONV_COLS))
            for b in range(SUBLANES):
                z = None
                for a in range(-(-(CONV_WIDTH + 1) // SUBLANES)):
                    o = SUBLANES * a + b
                    if o < 1 or o > CONV_WIDTH:
                        continue
                    xs = ext_ref[t0 + SUBLANES * a:t0 + SUBLANES * a + CONV_ROWS + SUBLANES, cols]
                    term = (xs.reshape(-1, SUBLANES, CONV_COLS) * r["dw_w"][o - 1, :, cols][None]
                            ).reshape(CONV_ROWS + SUBLANES, CONV_COLS)
                    z = term if z is None else z + term
                acc = acc + z[b:b + CONV_ROWS, :]
            pieces.append(acc)
        acc = jnp.concatenate(pieces, axis=1)
        mu = jnp.mean(acc, axis=-1, keepdims=True)
        cen = acc - mu
        var = jnp.mean(cen * cen, axis=-1, keepdims=True)
        y = cen * lax.rsqrt(var + EPS) * r["ln_g"][...] + r["ln_b"][...]
        y = _silu(y).astype(BF16)
        y = jnp.dot(y, r["pw_w"][...], preferred_element_type=F32) + r["pw_b"][...]
        y = y * r["conv_gate"][rows, :].astype(F32)
        ycat_ref[rows, 0:D_CONV] = y.astype(BF16)

    ylru = r["ylru"][...].astype(F32)
    if "ylru_rev" in r:
        ylru = ylru + r["ylru_rev"][...].astype(F32)
    ycat_ref[:, D_CONV:D_CONV + D_LRU] = (
        ylru * r["lru_gate"][...].astype(F32)).astype(BF16)

    qrows = GQA_GROUP * WINDOW
    qi = lax.broadcasted_iota(jnp.int32, (qrows, WINDOW), 0) % WINDOW
    kj = lax.broadcasted_iota(jnp.int32, (qrows, WINDOW), 1)
    neg_inf = jnp.float32(-jnp.inf)
    dn = (((1,), (1,)), ((), ()))
    n_qb = tq // WINDOW
    lc = r["kc"].shape[0]
    for qb in range(n_qb):
        rows = slice(qb * WINDOW, (qb + 1) * WINDOW)
        for g in range(N_KV_HEADS):
            kv_cols = slice(g * HEAD_DIM, (g + 1) * HEAD_DIM)
            q_st = r["q"][qb, g * GQA_GROUP:(g + 1) * GQA_GROUP].reshape(qrows, HEAD_DIM)
            sink = jnp.concatenate(
                [jnp.broadcast_to(r["sink"][g * GQA_GROUP + hh:g * GQA_GROUP + hh + 1, 0:1],
                                  (WINDOW, 1)) for hh in range(GQA_GROUP)], axis=0) * LOG2E
            blocks = []
            for cb in range(lc // WINDOW):
                crow = slice(cb * WINDOW, (cb + 1) * WINDOW)
                s = lax.dot_general(q_st, r["kc"][crow, kv_cols], dn, preferred_element_type=F32)
                blocks.append((s, r["vc"][crow, kv_cols]))
            if local:
                if qb == 0:
                    k_p, v_p = r["k_prev"][:, kv_cols], r["v_prev"][:, kv_cols]
                    ok_p = i > 0
                else:
                    prows = slice((qb - 1) * WINDOW, qb * WINDOW)
                    k_p, v_p = r["k"][prows, kv_cols], r["v"][prows, kv_cols]
                    ok_p = True
                if qb == n_qb - 1:
                    k_n, v_n = r["k_next"][:, kv_cols], r["v_next"][:, kv_cols]
                    ok_n = i < nt - 1
                else:
                    nrows = slice((qb + 1) * WINDOW, (qb + 2) * WINDOW)
                    k_n, v_n = r["k"][nrows, kv_cols], r["v"][nrows, kv_cols]
                    ok_n = True
                s_p = lax.dot_general(q_st, k_p, dn, preferred_element_type=F32)
                s_c = lax.dot_general(q_st, r["k"][rows, kv_cols], dn, preferred_element_type=F32)
                s_n = lax.dot_general(q_st, k_n, dn, preferred_element_type=F32)
                s_p = jnp.where((kj >= qi) & ok_p, s_p, neg_inf)
                s_n = jnp.where((kj <= qi) & ok_n, s_n, neg_inf)
                blocks += [(s_p, v_p), (s_c, r["v"][rows, kv_cols]), (s_n, v_n)]
            m_el = blocks[0][0]
            for s, _ in blocks[1:]:
                m_el = jnp.maximum(m_el, s)
            m = jnp.maximum(jnp.max(m_el, axis=-1, keepdims=True), sink)
            den_el = None
            o = None
            for s, v in blocks:
                p = jnp.exp2(s - m)
                den_el = p if den_el is None else den_el + p
                pv = jnp.dot(p.astype(BF16), v, preferred_element_type=F32)
                o = pv if o is None else o + pv
            den = jnp.sum(den_el, axis=-1, keepdims=True) + jnp.exp2(sink - m)
            o = o / den
            for hh in range(GQA_GROUP):
                c0 = (g * GQA_GROUP + hh) * HEAD_DIM
                gate = r["attn_gate"][rows, c0:c0 + HEAD_DIM].astype(F32)
                ycat_ref[rows, D_CONV + D_LRU + c0:D_CONV + D_LRU + c0 + HEAD_DIM] = (
                    o[hh * WINDOW:(hh + 1) * WINDOW, :] * gate).astype(BF16)

    out_ref = r["out"]
    for c0 in range(0, out_ref.shape[1], OUT_CHUNK_N):
        cols = slice(c0, c0 + OUT_CHUNK_N)
        y = jnp.dot(ycat_ref[...], r["w_out"][:, cols], preferred_element_type=F32)
        out_ref[:, cols] = r["x"][:, cols] + r["gate"][:, cols] * y
    if final:
        xn = out_ref[...]
        ms = jnp.mean(xn * xn, axis=-1, keepdims=True)
        out_ref[...] = xn * lax.rsqrt(ms + EPS) * r["final_g"][...]


def _mix_call(x2, u2, q4, uc2, ylru, ada4, layer, ada_row_fn, wts, *, tq, rows_per_batch,
              ctx_rows, local, final):
    m, d = x2.shape
    nt = rows_per_batch // tq
    nb = m // rows_per_batch
    names, specs, args = [], [], []

    def add(name, arr, spec):
        names.append(name)
        specs.append(spec)
        args.append(arr)

    def rowblk(width, col_off):
        return pl.BlockSpec((tq, width), lambda b, i: (b * nt + i, col_off // width))

    def halo(rows, width, col_off, nxt):
        per = tq // rows
        last = m // rows - 1
        if nxt:
            fn = lambda b, i: (jnp.minimum((b * nt + i + 1) * per, last), col_off // width)
        else:
            fn = lambda b, i: (jnp.maximum((b * nt + i) * per - 1, 0), col_off // width)
        return pl.BlockSpec((rows, width), fn)

    full = lambda shape: pl.BlockSpec(shape, lambda b, i: (0,) * len(shape),
                                      pipeline_mode=pl.Buffered(1))

    add("x", x2, pl.BlockSpec((tq, d), lambda b, i: (b * nt + i, 0)))
    add("gate", ada4, pl.BlockSpec((None, None, 1, d), lambda b, i: (layer, ada_row_fn(b), 0, 2)))
    add("glu", u2, rowblk(D_CONV, OFF_CONV_GLU))
    if local:
        add("glu_prev", u2, halo(CONV_HALO, D_CONV, OFF_CONV_GLU, False))
        add("glu_next", u2, halo(CONV_HALO, D_CONV, OFF_CONV_GLU, True))
    add("conv_gate", u2, rowblk(D_CONV, OFF_CONV_GATE))
    add("lru_gate", u2, rowblk(D_LRU, OFF_LRU_GATE))
    ylru_spec = pl.BlockSpec((tq, D_LRU), lambda b, i: (b * nt + i, 0))
    if isinstance(ylru, tuple):
        add("ylru", ylru[0], ylru_spec)
        add("ylru_rev", ylru[1], ylru_spec)
    else:
        add("ylru", ylru, ylru_spec)
    add("q", q4, pl.BlockSpec((tq // WINDOW, N_Q_HEADS, WINDOW, HEAD_DIM),
                              lambda b, i: (b * nt + i, 0, 0, 0)))
    add("attn_gate", u2, rowblk(D_ATTN, U_ATTN_GATE))
    if local:
        add("k", u2, rowblk(D_KV, OFF_K))
        add("v", u2, rowblk(D_KV, OFF_V))
        add("k_prev", u2, halo(WINDOW, D_KV, OFF_K, False))
        add("v_prev", u2, halo(WINDOW, D_KV, OFF_V, False))
        add("k_next", u2, halo(WINDOW, D_KV, OFF_K, True))
        add("v_next", u2, halo(WINDOW, D_KV, OFF_V, True))
    add("kc", uc2, pl.BlockSpec((ctx_rows, D_KV), lambda b, i: (b, OFF_K // D_KV)))
    add("vc", uc2, pl.BlockSpec((ctx_rows, D_KV), lambda b, i: (b, OFF_V // D_KV)))
    for name in ("dw_w", "dw_b", "ln_g", "ln_b", "pw_w", "pw_b", "sink", "w_out"):
        shape = wts[name].shape[1:]
        add(name, wts[name], pl.BlockSpec((None,) + shape,
                                          lambda b, i, nd=len(shape): (layer,) + (0,) * nd,
                                          pipeline_mode=pl.Buffered(1)))
    if final:
        add("final_g", wts["final_g"], full(wts["final_g"].shape))
    names += ["out", "ext", "ycat"]
    return pl.pallas_call(
        functools.partial(_mix_kernel, names=tuple(names), tq=tq, local=local, final=final),
        grid=(nb, nt),
        in_specs=specs,
        out_specs=pl.BlockSpec((tq, d), lambda b, i: (b * nt + i, 0)),
        out_shape=jax.ShapeDtypeStruct((m, d), F32),
        scratch_shapes=[pltpu.VMEM((tq + 2 * CONV_HALO, D_CONV), F32),
                        pltpu.VMEM((tq, D_MODEL), BF16)],
        compiler_params=pltpu.CompilerParams(
            dimension_semantics=("arbitrary", "arbitrary"), vmem_limit_bytes=VMEM_LIMIT),
        name="mix_lat" if local else "mix_ctx",
    )(*args)


def _rope_tables(seq):
    rows = seq // GRID_W
    row = np.repeat(np.arange(rows, dtype=np.float64), GRID_W)
    col = np.tile(np.arange(GRID_W, dtype=np.float64), rows)
    half = HEAD_DIM // 2
    inv = ROPE_BASE ** (-np.arange(0, half, 2, dtype=np.float64) / half)
    ang_r = row[:, None] * inv[None, :]
    ang_c = col[:, None] * inv[None, :]
    ang = np.concatenate([ang_r, ang_r, ang_c, ang_c], axis=-1)
    cos, sin = np.cos(ang), np.sin(ang)
    first = (np.arange(HEAD_DIM) % half) < (half // 2)
    return tuple(jnp.asarray(t, F32)
                 for t in (cos, np.where(first, -sin, 0.0), np.where(first, 0.0, sin)))


def _block_diag(w):
    nblk, blk = w.shape[-3], w.shape[-2]
    eye = jnp.eye(nblk, dtype=w.dtype)
    dense = w[..., :, :, None, :] * eye[:, None, :, None]
    return dense.reshape(w.shape[:-3] + (nblk * blk, nblk * blk))


def _pick_tile(n, pref):
    t = min(n, pref)
    while n % t:
        t //= 2
    return t


def kernel(x, c, ctx, c_ctx, norm_g, w_ada, b_ada, w_in, dw_w, dw_b, ln_g, ln_b, pw_w, pw_b,
           lru_conv_w, lru_conv_b, lru_w_r, lru_b_r, lru_w_i, lru_b_i, lru_lam, attn_sink,
           w_out, final_g):
    nb, seq, d = x.shape
    lc = ctx.shape[1]
    depth = w_in.shape[0]
    ctx_row = nb

    c_rows = jnp.concatenate([c, c_ctx[None, :], jnp.zeros((ADA_ROWS - nb - 1, d), F32)], axis=0)
    ada = _ada_call(c_rows, w_ada, b_ada)
    ada4 = ada.reshape(depth, ADA_ROWS, 1, 3 * d)
    rope_tabs = _rope_tables(seq)
    w_bf = w_in.astype(BF16)
    norm_g3 = norm_g.reshape(depth, 1, d)
    vec4 = lambda p: p.reshape(depth, 2, 1, D_LRU)
    wg_half = (0.5 * jnp.concatenate([_block_diag(lru_w_r), _block_diag(lru_w_i)], axis=-1)
               ).astype(BF16)
    lru_params = (lru_conv_w, vec4(lru_conv_b), wg_half, vec4(lru_b_r), vec4(lru_b_i),
                  vec4(lru_lam))
    vec3 = lambda p: p.reshape(depth, 1, D_CONV)
    wts = {
        "dw_w": jnp.broadcast_to(dw_w[:, :, None, :], (depth, CONV_WIDTH, SUBLANES, D_CONV)),
        "dw_b": vec3(dw_b), "ln_g": vec3(ln_g), "ln_b": vec3(ln_b),
        "pw_w": pw_w.astype(BF16), "pw_b": vec3(pw_b),
        "sink": jnp.broadcast_to(attn_sink[:, :, None], (depth, N_Q_HEADS, HEAD_DIM)),
        "w_out": w_out.astype(BF16), "final_g": final_g.reshape(1, d),
    }

    x2 = x.reshape(nb * seq, d)
    xc2 = ctx.reshape(nb * lc, d)
    tm = _pick_tile(seq, 1024)
    tq = _pick_tile(seq, 512)
    tt = _pick_tile(seq, 1024)
    tu = _pick_tile(tt, 512)
    zeros_h0 = jnp.zeros((nb, 1, D_LRU), F32)

    for l in range(depth):
        last = l == depth - 1
        u2, q4 = _inproj_call(x2, norm_g3, ada4, l, w_bf, rope_tabs, tm=tm, rows_per_batch=seq,
                              ada_row_fn=lambda b: b, with_q=True)
        uc2, qc4 = _inproj_call(xc2, norm_g3, ada4, l, w_bf, None, tm=lc, rows_per_batch=lc,
                                ada_row_fn=lambda b: ctx_row, with_q=not last)

        ylru, ylru_c = None, ()
        for dr in range(2):
            h_ctx = _lru_call(uc2, zeros_h0, None, lru_params, l, dr, rows_per_batch=lc, tt=lc,
                              units=1)
            edge = 0 if dr == 1 else lc - 1
            h0 = h_ctx.reshape(nb, lc, D_LRU)[:, edge:edge + 1, :].astype(F32)
            ylru = _lru_call(u2, h0, ylru, lru_params, l, dr, rows_per_batch=seq, tt=tt,
                             units=tt // tu)
            ylru_c += (h_ctx,)

        x2_new = _mix_call(x2, u2, q4, uc2, ylru, ada4, l, lambda b: b, wts, tq=tq,
                           rows_per_batch=seq, ctx_rows=lc, local=True, final=last)
        if not last:
            xc2 = _mix_call(xc2, uc2, qc4, uc2, ylru_c, ada4, l, lambda b: ctx_row, wts, tq=lc,
                            rows_per_batch=lc, ctx_rows=lc, local=False, final=False)
        x2 = x2_new
    return x2.reshape(nb, seq, d)
```

```python
import functools
import math

import jax
import jax.numpy as jnp
import numpy as np
from jax import lax
from jax.experimental import pallas as pl
from jax.experimental.pallas import tpu as pltpu

F32 = jnp.float32
BF16 = jnp.bfloat16

D_MODEL = 2048
D_CONV = 512
D_LRU = 512
HEAD_DIM = 128
N_Q_HEADS = 8
N_KV_HEADS = 2
GQA_GROUP = N_Q_HEADS // N_KV_HEADS
D_ATTN = N_Q_HEADS * HEAD_DIM
D_KV = N_KV_HEADS * HEAD_DIM
GRID_W = 64
CONV_WIDTH = 31
CONV_PAD = (CONV_WIDTH - 1) // 2
LRU_CONV_WIDTH = 4
LRU_BLOCKS = 8
LRU_C = 8.0
WINDOW = 128
ROPE_BASE = 10000.0
EPS = 1e-6

OFF_K = D_LRU
OFF_V = OFF_K + D_KV
OFF_MEM_END = OFF_V + D_KV
OFF_CONV_GLU = OFF_MEM_END
OFF_CONV_GATE = OFF_CONV_GLU + 2 * D_CONV
OFF_LRU_GATE = OFF_CONV_GATE + D_CONV
OFF_Q = OFF_LRU_GATE + D_LRU
OFF_ATTN_GATE = OFF_Q + D_ATTN
D_IN = OFF_ATTN_GATE + D_ATTN
U_ATTN_GATE = OFF_Q
D_U = D_IN - D_ATTN

ADA_ROWS = 8
IN_TILE_N = 1024
IN_CHUNK_N = 256
Q_TILE = OFF_Q // IN_TILE_N
GLU_TILE = OFF_CONV_GLU // IN_TILE_N
OUT_CHUNK_N = 256
CONV_HALO = 16
CONV_ROWS = 64
CONV_COLS = 256
SUBLANES = 8
LANES = 128
VMEM_LIMIT = 56 * 1024 * 1024
LOG2E = math.log2(math.e)
Q_SCALE = HEAD_DIM ** -0.5 * LOG2E

assert OFF_Q % IN_TILE_N == 0 and D_ATTN == IN_TILE_N and OFF_MEM_END == IN_TILE_N
assert OFF_CONV_GLU % IN_TILE_N == 0 and 2 * D_CONV == IN_TILE_N


def _sigmoid(x):
    return 0.5 * jnp.tanh(0.5 * x) + 0.5


def _silu(x):
    hx = 0.5 * x
    return hx * jnp.tanh(hx) + hx


def _ada_kernel(c_ref, w_ref, b_ref, o_ref):
    ca = _silu(c_ref[...])
    o_ref[...] = jnp.dot(ca.astype(BF16), w_ref[...].astype(BF16),
                         preferred_element_type=F32) + b_ref[...]


def _ada_call(c_rows, w_ada, b_ada):
    depth, d, n = w_ada.shape
    tn = 1024
    return pl.pallas_call(
        _ada_kernel,
        grid=(depth, n // tn),
        in_specs=[
            pl.BlockSpec((ADA_ROWS, d), lambda l, j: (0, 0)),
            pl.BlockSpec((None, d, tn), lambda l, j: (l, 0, j)),
            pl.BlockSpec((None, 1, tn), lambda l, j: (l, 0, j)),
        ],
        out_specs=pl.BlockSpec((None, ADA_ROWS, tn), lambda l, j: (l, 0, j)),
        out_shape=jax.ShapeDtypeStruct((depth, ADA_ROWS, n), F32),
        compiler_params=pltpu.CompilerParams(
            dimension_semantics=("arbitrary", "arbitrary"), vmem_limit_bytes=VMEM_LIMIT),
        name="ada_proj",
    )(c_rows, w_ada, b_ada.reshape(depth, 1, n))


def _rope(t, cos, sin_a, sin_b):
    return (t * cos + pltpu.roll(t, HEAD_DIM - 32, 1) * sin_a + pltpu.roll(t, 32, 1) * sin_b)


def _inproj_kernel(*refs, rope, with_q, tm):
    refs = list(refs)
    x_ref, g_ref, shift_ref, scale_ref, w_ref = refs[:5]
    del refs[:5]
    if rope:
        cos_ref, sa_ref, sb_ref = refs[:3]
        del refs[:3]
    o_ref = refs.pop(0)
    q_ref = refs.pop(0) if with_q else None
    h_ref = refs.pop(0)
    j = pl.program_id(1)

    @pl.when(j == 0)
    def _():
        x = x_ref[...]
        ms = jnp.mean(x * x, axis=-1, keepdims=True)
        gain = g_ref[...] * (1.0 + scale_ref[...])
        h_ref[...] = (x * lax.rsqrt(ms + EPS) * gain + shift_ref[...]).astype(BF16)

    chunks = [slice(c0, c0 + IN_CHUNK_N) for c0 in range(0, IN_TILE_N, IN_CHUNK_N)]
    heads_per_chunk = IN_CHUNK_N // HEAD_DIM

    def proj(cols):
        return jnp.dot(h_ref[...], w_ref[:, cols].astype(BF16), preferred_element_type=F32)

    def rope_heads(u):
        if not rope:
            return [u[:, hh * HEAD_DIM:(hh + 1) * HEAD_DIM] for hh in range(heads_per_chunk)]
        return [_rope(u[:, hh * HEAD_DIM:(hh + 1) * HEAD_DIM], cos_ref[...], sa_ref[...],
                      sb_ref[...]) for hh in range(heads_per_chunk)]

    def plain_tile():
        for cols in chunks:
            o_ref[:, cols] = proj(cols).astype(BF16)

    def q_tile():
        for ci, cols in enumerate(chunks):
            for hh, t in enumerate(rope_heads(proj(cols))):
                t = (t * Q_SCALE).astype(BF16)
                for qb in range(tm // WINDOW):
                    q_ref[qb, ci * heads_per_chunk + hh] = t[qb * WINDOW:(qb + 1) * WINDOW, :]

    def kv_tile():
        for cols in chunks:
            u = proj(cols)
            if OFF_K <= cols.start < OFF_V:
                u = jnp.concatenate(rope_heads(u), axis=1)
            o_ref[:, cols] = u.astype(BF16)

    def glu_tile():
        half = len(chunks) // 2
        for ci in range(half):
            o_ref[:, chunks[ci]] = (proj(chunks[ci]) * _sigmoid(proj(chunks[ci + half]))
                                    ).astype(BF16)
            o_ref[:, chunks[ci + half]] = jnp.zeros((tm, IN_CHUNK_N), BF16)

    def silu_tile():
        for cols in chunks:
            o_ref[:, cols] = _silu(proj(cols)).astype(BF16)

    if not with_q:
        plain_tile()
    else:
        pl.when(j == 0)(kv_tile if rope else plain_tile)
        pl.when(j == GLU_TILE)(glu_tile)
        pl.when(j == Q_TILE)(q_tile)
        pl.when((j != 0) & (j != GLU_TILE) & (j != Q_TILE))(silu_tile)


def _inproj_call(x2, norm_g, ada4, layer, w_bf, rope_tabs, *, tm, rows_per_batch, ada_row_fn,
                 with_q):
    m, d = x2.shape
    tiles_per_batch = rows_per_batch // tm
    rope = rope_tabs is not None
    n_tiles = D_IN // IN_TILE_N if with_q else 1

    def ada_spec(part):
        return pl.BlockSpec((None, None, 1, d),
                            lambda i, j: (layer, ada_row_fn(i // tiles_per_batch), 0, part))

    in_specs = [
        pl.BlockSpec((tm, d), lambda i, j: (i, 0)),
        pl.BlockSpec((None, 1, d), lambda i, j: (layer, 0, 0)),
        ada_spec(0), ada_spec(1),
        pl.BlockSpec((None, d, IN_TILE_N), lambda i, j: (layer, 0, j)),
    ]
    args = [x2, norm_g, ada4, ada4, w_bf]
    if rope:
        tab_spec = pl.BlockSpec((tm, HEAD_DIM), lambda i, j: (i % tiles_per_batch, 0))
        in_specs += [tab_spec] * 3
        args += list(rope_tabs)
    out_specs = [pl.BlockSpec((tm, IN_TILE_N), lambda i, j: (i, jnp.where(j > Q_TILE, j - 1, j)))]
    out_shape = [jax.ShapeDtypeStruct((m, D_U if with_q else IN_TILE_N), BF16)]
    if with_q:
        out_specs.append(pl.BlockSpec((tm // WINDOW, N_Q_HEADS, WINDOW, HEAD_DIM),
                                      lambda i, j: (i, 0, 0, 0)))
        out_shape.append(jax.ShapeDtypeStruct((m // WINDOW, N_Q_HEADS, WINDOW, HEAD_DIM), BF16))
    outs = pl.pallas_call(
        functools.partial(_inproj_kernel, rope=rope, with_q=with_q, tm=tm),
        grid=(m // tm, n_tiles),
        in_specs=in_specs,
        out_specs=out_specs,
        out_shape=out_shape,
        scratch_shapes=[pltpu.VMEM((tm, d), BF16)],
        compiler_params=pltpu.CompilerParams(
            dimension_semantics=("arbitrary", "arbitrary"), vmem_limit_bytes=VMEM_LIMIT),
        name="in_proj",
    )(*args)
    return (outs[0], outs[1]) if with_q else (outs[0], None)


def _lru_kernel(*refs, reverse, has_acc, tt, units):
    if has_acc:
        (x_ref, h0_ref, acc_ref, perm_ref, unperm_ref, cw_ref, cb_ref, wg_ref, br_ref, bi_ref,
         lam_ref, o_ref, ext3_ref, a3_ref, b3_ref, edge_ref, h_ref) = refs
    else:
        (x_ref, h0_ref, perm_ref, unperm_ref, cw_ref, cb_ref, wg_ref, br_ref, bi_ref,
         lam_ref, o_ref, ext3_ref, a3_ref, b3_ref, edge_ref, h_ref) = refs
        acc_ref = None
    c = pl.program_id(1)
    tu = tt // units
    seg = tu // SUBLANES
    hist = LRU_CONV_WIDTH - 1
    row = lax.broadcasted_iota(jnp.int32, (SUBLANES, D_LRU), 0)

    def blk(g):
        return slice(g * SUBLANES, (g + 1) * SUBLANES)

    @pl.when(c == 0)
    def _():
        edge_ref[...] = jnp.zeros((hist * SUBLANES, D_LRU), F32)
        h_ref[...] = jnp.broadcast_to(h0_ref[...], (SUBLANES, D_LRU))

    lam = lam_ref[...]
    softplus_neg_lam = jnp.maximum(-lam, 0.0) + jnp.log1p(jnp.exp(-jnp.abs(lam)))
    half_k = (-0.5 * LRU_C) * softplus_neg_lam
    half_br = 0.5 * br_ref[...]
    half_bi = 0.5 * bi_ref[...]

    order = list(range(units - 1, -1, -1) if reverse else range(units))
    for u in order:
        _lru_gates(slice(u * tu, (u + 1) * tu), x_ref, perm_ref, cw_ref, cb_ref, wg_ref, half_br,
                   half_bi, half_k, ext3_ref.at[u], a3_ref.at[u], b3_ref.at[u], edge_ref,
                   reverse=reverse, tt=tu, seg=seg, hist=hist, row=row, blk=blk)

    def step(t, carry):
        g_ = (seg - 1 - t) if reverse else t
        rows = pl.ds(pl.multiple_of(g_ * SUBLANES, SUBLANES), SUBLANES)
        new = []
        for u, (h, p) in zip(order, carry):
            a = a3_ref[u, rows, :]
            h = a * h + b3_ref[u, rows, :]
            p = a * p
            b3_ref[u, rows, :] = h
            a3_ref[u, rows, :] = p
            new.append((h, p))
        return tuple(new)

    init = tuple((jnp.zeros((SUBLANES, D_LRU), F32), jnp.ones((SUBLANES, D_LRU), F32))
                 for _ in order)
    finals = lax.fori_loop(0, seg, step, init, unroll=8)

    for u, (h_fin, p_fin) in zip(order, finals):
        _lru_finish(slice(u * tu, (u + 1) * tu), h_fin, p_fin, acc_ref, unperm_ref, o_ref,
                    a3_ref.at[u], b3_ref.at[u], h_ref, reverse=reverse, tt=tu, seg=seg, row=row)


def _lru_gates(rows_u, x_ref, perm_ref, cw_ref, cb_ref, wg_ref, half_br, half_bi, half_k, ext_ref,
               a_ref, b_ref, edge_ref, *, reverse, tt, seg, hist, row, blk):
    base = 0 if reverse else hist
    ext_ref[base * SUBLANES:base * SUBLANES + tt, :] = jnp.dot(
        perm_ref[...], x_ref[rows_u, :], preferred_element_type=F32)
    for j in range(1, hist + 1):
        if reverse:
            rolled = pltpu.roll(ext_ref[blk(j - 1), :], SUBLANES - 1, 0)
            ext_ref[blk(seg + j - 1), :] = jnp.where(row == SUBLANES - 1, edge_ref[blk(j - 1), :],
                                                     rolled)
        else:
            rolled = pltpu.roll(ext_ref[blk(hist + seg - j), :], 1, 0)
            ext_ref[blk(hist - j), :] = jnp.where(row == 0, edge_ref[blk(j - 1), :], rolled)
        edge_ref[blk(j - 1), :] = rolled

    xc = jnp.broadcast_to(cb_ref[...], (tt, D_LRU))
    for k in range(LRU_CONV_WIDTH):
        off = (hist - k) if reverse else k
        xc = xc + cw_ref[k:k + 1, :] * ext_ref[off * SUBLANES:off * SUBLANES + tt, :]

    g = jnp.dot(xc.astype(BF16), wg_ref[...], preferred_element_type=F32)
    t_r = jnp.tanh(g[:, :D_LRU] + half_br)
    t_i = jnp.tanh(g[:, D_LRU:] + half_bi)
    log_a = half_k * t_r + half_k
    a_ref[...] = jnp.exp(log_a)
    th = jnp.tanh(log_a)
    one_minus_a2 = (-2.0 * th) / (1.0 - th)
    half_xc = 0.5 * xc
    b_ref[...] = jnp.sqrt(one_minus_a2) * (half_xc * t_i + half_xc)


def _lru_finish(rows_u, h_fin, p_fin, acc_ref, unperm_ref, o_ref, a_ref, b_ref, h_ref, *, reverse,
                tt, seg, row):
    has_acc = acc_ref is not None
    a, b = p_fin, h_fin
    for d in (1, 2, 4):
        keep = (row < SUBLANES - d) if reverse else (row >= d)
        shift = (SUBLANES - d) if reverse else d
        a_s = jnp.where(keep, pltpu.roll(a, shift, 0), 1.0)
        b_s = jnp.where(keep, pltpu.roll(b, shift, 0), 0.0)
        b = a * b_s + b
        a = a * a_s
    h_in = h_ref[...]
    end = a * h_in + b
    if reverse:
        carry_in = jnp.where(row < SUBLANES - 1, pltpu.roll(end, SUBLANES - 1, 0), h_in)
        h_ref[...] = jnp.broadcast_to(end[0:1, :], (SUBLANES, D_LRU))
    else:
        carry_in = jnp.where(row >= 1, pltpu.roll(end, 1, 0), h_in)
        h_ref[...] = jnp.broadcast_to(end[SUBLANES - 1:SUBLANES, :], (SUBLANES, D_LRU))

    h_all = (b_ref[...].reshape(seg, SUBLANES, D_LRU)
             + a_ref[...].reshape(seg, SUBLANES, D_LRU) * carry_in[None]).reshape(tt, D_LRU)
    out = jnp.dot(unperm_ref[...], h_all.astype(BF16), preferred_element_type=F32)
    if has_acc:
        out = out + acc_ref[rows_u, :].astype(F32)
    o_ref[rows_u, :] = out.astype(o_ref.dtype)


def _lru_call(u2, h0, acc, params, layer, direction, *, rows_per_batch, tt, units):
    cw, cb, wg_half, br, bi, lam = params
    reverse = direction == 1
    m = u2.shape[0]
    nb = m // rows_per_batch
    nc = rows_per_batch // tt
    has_acc = acc is not None
    tu = tt // units

    def chunk(c):
        return (nc - 1 - c) if reverse else c

    row_spec = pl.BlockSpec((tt, D_LRU), lambda b, c: (b * nc + chunk(c), 0))
    full = lambda shape: pl.BlockSpec(shape, lambda b, c: (0,) * len(shape))
    in_specs = [row_spec, pl.BlockSpec((None, 1, D_LRU), lambda b, c: (b, 0, 0))]
    args = [u2, h0]
    if has_acc:
        in_specs.append(row_spec)
        args.append(acc)
    seg = tu // SUBLANES
    src = (np.arange(tu) % SUBLANES) * seg + np.arange(tu) // SUBLANES
    perm = jnp.asarray(src[:, None] == np.arange(tu)[None, :], BF16)
    sel = lambda shape: pl.BlockSpec((None, None) + shape,
                                     lambda b, c: (layer, direction) + (0,) * len(shape))
    in_specs += [full((tu, tu)), full((tu, tu)),
                 sel((LRU_CONV_WIDTH, D_LRU)), sel((1, D_LRU)), sel((D_LRU, 2 * D_LRU)),
                 sel((1, D_LRU)), sel((1, D_LRU)), sel((1, D_LRU))]
    args += [perm, perm.T, cw, cb, wg_half, br, bi, lam]
    edge_rows = (LRU_CONV_WIDTH - 1) * SUBLANES
    return pl.pallas_call(
        functools.partial(_lru_kernel, reverse=reverse, has_acc=has_acc, tt=tt, units=units),
        grid=(nb, nc),
        in_specs=in_specs,
        out_specs=row_spec,
        out_shape=jax.ShapeDtypeStruct((m, D_LRU), BF16),
        scratch_shapes=[pltpu.VMEM((units, tu + edge_rows, D_LRU), F32),
                        pltpu.VMEM((units, tu, D_LRU), F32),
                        pltpu.VMEM((units, tu, D_LRU), F32),
                        pltpu.VMEM((edge_rows, D_LRU), F32),
                        pltpu.VMEM((SUBLANES, D_LRU), F32)],
        compiler_params=pltpu.CompilerParams(
            dimension_semantics=("arbitrary", "arbitrary"), vmem_limit_bytes=VMEM_LIMIT),
        name="rglru_rev" if reverse else "rglru_fwd",
    )(*args)


def _mix_kernel(*refs, names, tq, local, final):
    r = dict(zip(names, refs))
    i = pl.program_id(1)
    nt = pl.num_programs(1)
    ext_ref, ycat_ref = r["ext"], r["ycat"]

    def glu(ref):
        return ref[...].astype(F32)

    ext_ref[CONV_HALO:CONV_HALO + tq, :] = glu(r["glu"])
    zero_halo = jnp.zeros((CONV_HALO, D_CONV), F32)
    if local:
        ext_ref[0:CONV_HALO, :] = jnp.where(i > 0, glu(r["glu_prev"]), zero_halo)
        ext_ref[CONV_HALO + tq:, :] = jnp.where(i < nt - 1, glu(r["glu_next"]), zero_halo)
    else:
        ext_ref[0:CONV_HALO, :] = zero_halo
        ext_ref[CONV_HALO + tq:, :] = zero_halo

    for rc in range(tq // CONV_ROWS):
        t0 = rc * CONV_ROWS
        rows = slice(t0, t0 + CONV_ROWS)
        pieces = []
        for cc in range(D_CONV // CONV_COLS):
            cols = slice(cc * CONV_COLS, (cc + 1) * CONV_COLS)
            acc = jnp.broadcast_to(r["dw_b"][:, cols], (CONV_ROWS, CONV_COLS))
            for b in range(SUBLANES):
                z = None
                for a in range(-(-(CONV_WIDTH + 1) // SUBLANES)):
                    o = SUBLANES * a + b
                    if o < 1 or o > CONV_WIDTH:
                        continue
                    xs = ext_ref[t0 + SUBLANES * a:t0 + SUBLANES * a + CONV_ROWS + SUBLANES, cols]
                    term = (xs.reshape(-1, SUBLANES, CONV_COLS) * r["dw_w"][o - 1, :, cols][None]
                            ).reshape(CONV_ROWS + SUBLANES, CONV_COLS)
                    z = term if z is None else z + term
                acc = acc + z[b:b + CONV_ROWS, :]
            pieces.append(acc)
        acc = jnp.concatenate(pieces, axis=1)
        mu = jnp.mean(acc, axis=-1, keepdims=True)
        cen = acc - mu
        var = jnp.mean(cen * cen, axis=-1, keepdims=True)
        y = cen * lax.rsqrt(var + EPS) * r["ln_g"][...] + r["ln_b"][...]
        y = _silu(y).astype(BF16)
        y = jnp.dot(y, r["pw_w"][...], preferred_element_type=F32) + r["pw_b"][...]
        y = y * r["conv_gate"][rows, :].astype(F32)
        ycat_ref[rows, 0:D_CONV] = y.astype(BF16)

    ylru = r["ylru"][...].astype(F32)
    if "ylru_rev" in r:
        ylru = ylru + r["ylru_rev"][...].astype(F32)
    ycat_ref[:, D_CONV:D_CONV + D_LRU] = (
        ylru * r["lru_gate"][...].astype(F32)).astype(BF16)

    qrows = GQA_GROUP * WINDOW
    qi = lax.broadcasted_iota(jnp.int32, (qrows, WINDOW), 0) % WINDOW
    kj = lax.broadcasted_iota(jnp.int32, (qrows, WINDOW), 1)
    neg_inf = jnp.float32(-jnp.inf)
    dn = (((1,), (1,)), ((), ()))
    n_qb = tq // WINDOW
    lc = r["kc"].shape[0]
    for qb in range(n_qb):
        rows = slice(qb * WINDOW, (qb + 1) * WINDOW)
        for g in range(N_KV_HEADS):
            kv_cols = slice(g * HEAD_DIM, (g + 1) * HEAD_DIM)
            q_st = r["q"][qb, g * GQA_GROUP:(g + 1) * GQA_GROUP].reshape(qrows, HEAD_DIM)
            sink = jnp.concatenate(
                [jnp.broadcast_to(r["sink"][g * GQA_GROUP + hh:g * GQA_GROUP + hh + 1, 0:1],
                                  (WINDOW, 1)) for hh in range(GQA_GROUP)], axis=0) * LOG2E
            blocks = []
            for cb in range(lc // WINDOW):
                crow = slice(cb * WINDOW, (cb + 1) * WINDOW)
                s = lax.dot_general(q_st, r["kc"][crow, kv_cols], dn, preferred_element_type=F32)
                blocks.append((s, r["vc"][crow, kv_cols]))
            if local:
                if qb == 0:
                    k_p, v_p = r["k_prev"][:, kv_cols], r["v_prev"][:, kv_cols]
                    ok_p = i > 0
                else:
                    prows = slice((qb - 1) * WINDOW, qb * WINDOW)
                    k_p, v_p = r["k"][prows, kv_cols], r["v"][prows, kv_cols]
                    ok_p = True
                if qb == n_qb - 1:
                    k_n, v_n = r["k_next"][:, kv_cols], r["v_next"][:, kv_cols]
                    ok_n = i < nt - 1
                else:
                    nrows = slice((qb + 1) * WINDOW, (qb + 2) * WINDOW)
                    k_n, v_n = r["k"][nrows, kv_cols], r["v"][nrows, kv_cols]
                    ok_n = True
                s_p = lax.dot_general(q_st, k_p, dn, preferred_element_type=F32)
                s_c = lax.dot_general(q_st, r["k"][rows, kv_cols], dn, preferred_element_type=F32)
                s_n = lax.dot_general(q_st, k_n, dn, preferred_element_type=F32)
                s_p = jnp.where((kj >= qi) & ok_p, s_p, neg_inf)
                s_n = jnp.where((kj <= qi) & ok_n, s_n, neg_inf)
                blocks += [(s_p, v_p), (s_c, r["v"][rows, kv_cols]), (s_n, v_n)]
            m_el = blocks[0][0]
            for s, _ in blocks[1:]:
                m_el = jnp.maximum(m_el, s)
            m = jnp.maximum(jnp.max(m_el, axis=-1, keepdims=True), sink)
            den_el = None
            o = None
            for s, v in blocks:
                p = jnp.exp2(s - m)
                den_el = p if den_el is None else den_el + p
                pv = jnp.dot(p.astype(BF16), v, preferred_element_type=F32)
                o = pv if o is None else o + pv
            den = jnp.sum(den_el, axis=-1, keepdims=True) + jnp.exp2(sink - m)
            o = o / den
            for hh in range(GQA_GROUP):
                c0 = (g * GQA_GROUP + hh) * HEAD_DIM
                gate = r["attn_gate"][rows, c0:c0 + HEAD_DIM].astype(F32)
                ycat_ref[rows, D_CONV + D_LRU + c0:D_CONV + D_LRU + c0 + HEAD_DIM] = (
                    o[hh * WINDOW:(hh + 1) * WINDOW, :] * gate).astype(BF16)

    out_ref = r["out"]
    for c0 in range(0, out_ref.shape[1], OUT_CHUNK_N):
        cols = slice(c0, c0 + OUT_CHUNK_N)
        y = jnp.dot(ycat_ref[...], r["w_out"][:, cols], preferred_element_type=F32)
        out_ref[:, cols] = r["x"][:, cols] + r["gate"][:, cols] * y
    if final:
        xn = out_ref[...]
        ms = jnp.mean(xn * xn, axis=-1, keepdims=True)
        out_ref[...] = xn * lax.rsqrt(ms + EPS) * r["final_g"][...]


def _mix_call(x2, u2, q4, uc2, ylru, ada4, layer, ada_row_fn, wts, *, tq, rows_per_batch,
              ctx_rows, local, final):
    m, d = x2.shape
    nt = rows_per_batch // tq
    nb = m // rows_per_batch
    names, specs, args = [], [], []

    def add(name, arr, spec):
        names.append(name)
        specs.append(spec)
        args.append(arr)

    def rowblk(width, col_off):
        return pl.BlockSpec((tq, width), lambda b, i: (b * nt + i, col_off // width))

    def halo(rows, width, col_off, nxt):
        per = tq // rows
        last = m // rows - 1
        if nxt:
            fn = lambda b, i: (jnp.minimum((b * nt + i + 1) * per, last), col_off // width)
        else:
            fn = lambda b, i: (jnp.maximum((b * nt + i) * per - 1, 0), col_off // width)
        return pl.BlockSpec((rows, width), fn)

    full = lambda shape: pl.BlockSpec(shape, lambda b, i: (0,) * len(shape),
                                      pipeline_mode=pl.Buffered(1))

    add("x", x2, pl.BlockSpec((tq, d), lambda b, i: (b * nt + i, 0)))
    add("gate", ada4, pl.BlockSpec((None, None, 1, d), lambda b, i: (layer, ada_row_fn(b), 0, 2)))
    add("glu", u2, rowblk(D_CONV, OFF_CONV_GLU))
    if local:
        add("glu_prev", u2, halo(CONV_HALO, D_CONV, OFF_CONV_GLU, False))
        add("glu_next", u2, halo(CONV_HALO, D_CONV, OFF_CONV_GLU, True))
    add("conv_gate", u2, rowblk(D_CONV, OFF_CONV_GATE))
    add("lru_gate", u2, rowblk(D_LRU, OFF_LRU_GATE))
    ylru_spec = pl.BlockSpec((tq, D_LRU), lambda b, i: (b * nt + i, 0))
    if isinstance(ylru, tuple):
        add("ylru", ylru[0], ylru_spec)
        add("ylru_rev", ylru[1], ylru_spec)
    else:
        add("ylru", ylru, ylru_spec)
    add("q", q4, pl.BlockSpec((tq // WINDOW, N_Q_HEADS, WINDOW, HEAD_DIM),
                              lambda b, i: (b * nt + i, 0, 0, 0)))
    add("attn_gate", u2, rowblk(D_ATTN, U_ATTN_GATE))
    if local:
        add("k", u2, rowblk(D_KV, OFF_K))
        add("v", u2, rowblk(D_KV, OFF_V))
        add("k_prev", u2, halo(WINDOW, D_KV, OFF_K, False))
        add("v_prev", u2, halo(WINDOW, D_KV, OFF_V, False))
        add("k_next", u2, halo(WINDOW, D_KV, OFF_K, True))
        add("v_next", u2, halo(WINDOW, D_KV, OFF_V, True))
    add("kc", uc2, pl.BlockSpec((ctx_rows, D_KV), lambda b, i: (b, OFF_K // D_KV)))
    add("vc", uc2, pl.BlockSpec((ctx_rows, D_KV), lambda b, i: (b, OFF_V // D_KV)))
    for name in ("dw_w", "dw_b", "ln_g", "ln_b", "pw_w", "pw_b", "sink", "w_out"):
        shape = wts[name].shape[1:]
        add(name, wts[name], pl.BlockSpec((None,) + shape,
                                          lambda b, i, nd=len(shape): (layer,) + (0,) * nd,
                                          pipeline_mode=pl.Buffered(1)))
    if final:
        add("final_g", wts["final_g"], full(wts["final_g"].shape))
    names += ["out", "ext", "ycat"]
    return pl.pallas_call(
        functools.partial(_mix_kernel, names=tuple(names), tq=tq, local=local, final=final),
        grid=(nb, nt),
        in_specs=specs,
        out_specs=pl.BlockSpec((tq, d), lambda b, i: (b * nt + i, 0)),
        out_shape=jax.ShapeDtypeStruct((m, d), F32),
        scratch_shapes=[pltpu.VMEM((tq + 2 * CONV_HALO, D_CONV), F32),
                        pltpu.VMEM((tq, D_MODEL), BF16)],
        compiler_params=pltpu.CompilerParams(
            dimension_semantics=("arbitrary", "arbitrary"), vmem_limit_bytes=VMEM_LIMIT),
        name="mix_lat" if local else "mix_ctx",
    )(*args)


def _rope_tables(seq):
    rows = seq // GRID_W
    row = np.repeat(np.arange(rows, dtype=np.float64), GRID_W)
    col = np.tile(np.arange(GRID_W, dtype=np.float64), rows)
    half = HEAD_DIM // 2
    inv = ROPE_BASE ** (-np.arange(0, half, 2, dtype=np.float64) / half)
    ang_r = row[:, None] * inv[None, :]
    ang_c = col[:, None] * inv[None, :]
    ang = np.concatenate([ang_r, ang_r, ang_c, ang_c], axis=-1)
    cos, sin = np.cos(ang), np.sin(ang)
    first = (np.arange(HEAD_DIM) % half) < (half // 2)
    return tuple(jnp.asarray(t, F32)
                 for t in (cos, np.where(first, -sin, 0.0), np.where(first, 0.0, sin)))


def _block_diag(w):
    nblk, blk = w.shape[-3], w.shape[-2]
    eye = jnp.eye(nblk, dtype=w.dtype)
    dense = w[..., :, :, None, :] * eye[:, None, :, None]
    return dense.reshape(w.shape[:-3] + (nblk * blk, nblk * blk))


def _pick_tile(n, pref):
    t = min(n, pref)
    while n % t:
        t //= 2
    return t


def kernel(x, c, ctx, c_ctx, norm_g, w_ada, b_ada, w_in, dw_w, dw_b, ln_g, ln_b, pw_w, pw_b,
           lru_conv_w, lru_conv_b, lru_w_r, lru_b_r, lru_w_i, lru_b_i, lru_lam, attn_sink,
           w_out, final_g):
    nb, seq, d = x.shape
    lc = ctx.shape[1]
    depth = w_in.shape[0]
    ctx_row = nb

    c_rows = jnp.concatenate([c, c_ctx[None, :], jnp.zeros((ADA_ROWS - nb - 1, d), F32)], axis=0)
    ada = _ada_call(c_rows, w_ada, b_ada)
    ada4 = ada.reshape(depth, ADA_ROWS, 1, 3 * d)
    rope_tabs = _rope_tables(seq)
    w_bf = w_in
    norm_g3 = norm_g.reshape(depth, 1, d)
    vec4 = lambda p: p.reshape(depth, 2, 1, D_LRU)
    wg_half = (0.5 * jnp.concatenate([_block_diag(lru_w_r), _block_diag(lru_w_i)], axis=-1)
               ).astype(BF16)
    lru_params = (lru_conv_w, vec4(lru_conv_b), wg_half, vec4(lru_b_r), vec4(lru_b_i),
                  vec4(lru_lam))
    vec3 = lambda p: p.reshape(depth, 1, D_CONV)
    wts = {
        "dw_w": jnp.broadcast_to(dw_w[:, :, None, :], (depth, CONV_WIDTH, SUBLANES, D_CONV)),
        "dw_b": vec3(dw_b), "ln_g": vec3(ln_g), "ln_b": vec3(ln_b),
        "pw_w": pw_w.astype(BF16), "pw_b": vec3(pw_b),
        "sink": jnp.broadcast_to(attn_sink[:, :, None], (depth, N_Q_HEADS, HEAD_DIM)),
        "w_out": w_out.astype(BF16), "final_g": final_g.reshape(1, d),
    }

    x2 = x.reshape(nb * seq, d)
    xc2 = ctx.reshape(nb * lc, d)
    tm = _pick_tile(seq, 1024)
    tq = _pick_tile(seq, 512)
    tt = _pick_tile(seq, 1024)
    tu = _pick_tile(tt, 512)
    zeros_h0 = jnp.zeros((nb, 1, D_LRU), F32)

    for l in range(depth):
        last = l == depth - 1
        u2, q4 = _inproj_call(x2, norm_g3, ada4, l, w_bf, rope_tabs, tm=tm, rows_per_batch=seq,
                              ada_row_fn=lambda b: b, with_q=True)
        uc2, qc4 = _inproj_call(xc2, norm_g3, ada4, l, w_bf, None, tm=lc, rows_per_batch=lc,
                                ada_row_fn=lambda b: ctx_row, with_q=not last)

        ylru, ylru_c = None, ()
        for dr in range(2):
            h_ctx = _lru_call(uc2, zeros_h0, None, lru_params, l, dr, rows_per_batch=lc, tt=lc,
                              units=1)
            edge = 0 if dr == 1 else lc - 1
            h0 = h_ctx.reshape(nb, lc, D_LRU)[:, edge:edge + 1, :].astype(F32)
            ylru = _lru_call(u2, h0, ylru, lru_params, l, dr, rows_per_batch=seq, tt=tt,
                             units=tt // tu)
            ylru_c += (h_ctx,)

        x2_new = _mix_call(x2, u2, q4, uc2, ylru, ada4, l, lambda b: b, wts, tq=tq,
                           rows_per_batch=seq, ctx_rows=lc, local=True, final=last)
        if not last:
            xc2 = _mix_call(xc2, uc2, qc4, uc2, ylru_c, ada4, l, lambda b: ctx_row, wts, tq=lc,
                            rows_per_batch=lc, ctx_rows=lc, local=False, final=False)
        x2 = x2_new
    return x2.reshape(nb, seq, d)
```

```python
import functools
import math

import jax
import jax.numpy as jnp
import numpy as np
from jax import lax
from jax.experimental import pallas as pl
from jax.experimental.pallas import tpu as pltpu

F32 = jnp.float32
BF16 = jnp.bfloat16

D_MODEL = 2048
D_CONV = 512
D_LRU = 512
HEAD_DIM = 128
N_Q_HEADS = 8
N_KV_HEADS = 2
GQA_GROUP = N_Q_HEADS // N_KV_HEADS
D_ATTN = N_Q_HEADS * HEAD_DIM
D_KV = N_KV_HEADS * HEAD_DIM
GRID_W = 64
CONV_WIDTH = 31
CONV_PAD = (CONV_WIDTH - 1) // 2
LRU_CONV_WIDTH = 4
LRU_BLOCKS = 8
LRU_C = 8.0
WINDOW = 128
ROPE_BASE = 10000.0
EPS = 1e-6

OFF_K = D_LRU
OFF_V = OFF_K + D_KV
OFF_MEM_END = OFF_V + D_KV
OFF_CONV_GLU = OFF_MEM_END
OFF_CONV_GATE = OFF_CONV_GLU + 2 * D_CONV
OFF_LRU_GATE = OFF_CONV_GATE + D_CONV
OFF_Q = OFF_LRU_GATE + D_LRU
OFF_ATTN_GATE = OFF_Q + D_ATTN
D_IN = OFF_ATTN_GATE + D_ATTN
U_ATTN_GATE = OFF_Q
D_U = D_IN - D_ATTN

ADA_ROWS = 8
IN_TILE_N = 1024
IN_CHUNK_N = 256
Q_TILE = OFF_Q // IN_TILE_N
GLU_TILE = OFF_CONV_GLU // IN_TILE_N
OUT_CHUNK_N = 256
CONV_HALO = 16
CONV_ROWS = 64
CONV_COLS = 256
SUBLANES = 8
LANES = 128
VMEM_LIMIT = 56 * 1024 * 1024
LOG2E = math.log2(math.e)
Q_SCALE = HEAD_DIM ** -0.5 * LOG2E

assert OFF_Q % IN_TILE_N == 0 and D_ATTN == IN_TILE_N and OFF_MEM_END == IN_TILE_N
assert OFF_CONV_GLU % IN_TILE_N == 0 and 2 * D_CONV == IN_TILE_N


def _sigmoid(x):
    return 0.5 * jnp.tanh(0.5 * x) + 0.5


def _silu(x):
    hx = 0.5 * x
    return hx * jnp.tanh(hx) + hx


def _ada_kernel(c_ref, w_ref, b_ref, o_ref):
    ca = _silu(c_ref[...])
    o_ref[...] = jnp.dot(ca.astype(BF16), w_ref[...].astype(BF16),
                         preferred_element_type=F32) + b_ref[...]


def _ada_call(c_rows, w_ada, b_ada):
    depth, d, n = w_ada.shape
    tn = 1024
    return pl.pallas_call(
        _ada_kernel,
        grid=(depth, n // tn),
        in_specs=[
            pl.BlockSpec((ADA_ROWS, d), lambda l, j: (0, 0)),
            pl.BlockSpec((None, d, tn), lambda l, j: (l, 0, j)),
            pl.BlockSpec((None, 1, tn), lambda l, j: (l, 0, j)),
        ],
        out_specs=pl.BlockSpec((None, ADA_ROWS, tn), lambda l, j: (l, 0, j)),
        out_shape=jax.ShapeDtypeStruct((depth, ADA_ROWS, n), F32),
        compiler_params=pltpu.CompilerParams(
            dimension_semantics=("arbitrary", "arbitrary"), vmem_limit_bytes=VMEM_LIMIT),
        name="ada_proj",
    )(c_rows, w_ada, b_ada.reshape(depth, 1, n))


def _rope(t, cos, sin_a, sin_b):
    return (t * cos + pltpu.roll(t, HEAD_DIM - 32, 1) * sin_a + pltpu.roll(t, 32, 1) * sin_b)


def _inproj_kernel(*refs, rope, with_q, tm):
    refs = list(refs)
    x_ref, g_ref, shift_ref, scale_ref, w_ref = refs[:5]
    del refs[:5]
    if rope:
        cos_ref, sa_ref, sb_ref = refs[:3]
        del refs[:3]
    o_ref = refs.pop(0)
    q_ref = refs.pop(0) if with_q else None
    h_ref = refs.pop(0)
    j = pl.program_id(1)

    @pl.when(j == 0)
    def _():
        x = x_ref[...]
        ms = jnp.mean(x * x, axis=-1, keepdims=True)
        gain = g_ref[...] * (1.0 + scale_ref[...])
        h_ref[...] = (x * lax.rsqrt(ms + EPS) * gain + shift_ref[...]).astype(BF16)

    chunks = [slice(c0, c0 + IN_CHUNK_N) for c0 in range(0, IN_TILE_N, IN_CHUNK_N)]
    heads_per_chunk = IN_CHUNK_N // HEAD_DIM

    def proj(cols):
        return jnp.dot(h_ref[...], w_ref[:, cols], preferred_element_type=F32)

    def rope_heads(u):
        if not rope:
            return [u[:, hh * HEAD_DIM:(hh + 1) * HEAD_DIM] for hh in range(heads_per_chunk)]
        return [_rope(u[:, hh * HEAD_DIM:(hh + 1) * HEAD_DIM], cos_ref[...], sa_ref[...],
                      sb_ref[...]) for hh in range(heads_per_chunk)]

    def plain_tile():
        for cols in chunks:
            o_ref[:, cols] = proj(cols).astype(BF16)

    def q_tile():
        for ci, cols in enumerate(chunks):
            for hh, t in enumerate(rope_heads(proj(cols))):
                t = (t * Q_SCALE).astype(BF16)
                for qb in range(tm // WINDOW):
                    q_ref[qb, ci * heads_per_chunk + hh] = t[qb * WINDOW:(qb + 1) * WINDOW, :]

    def kv_tile():
        for cols in chunks:
            u = proj(cols)
            if OFF_K <= cols.start < OFF_V:
                u = jnp.concatenate(rope_heads(u), axis=1)
            o_ref[:, cols] = u.astype(BF16)

    def glu_tile():
        half = len(chunks) // 2
        for ci in range(half):
            o_ref[:, chunks[ci]] = (proj(chunks[ci]) * _sigmoid(proj(chunks[ci + half]))
                                    ).astype(BF16)
            o_ref[:, chunks[ci + half]] = jnp.zeros((tm, IN_CHUNK_N), BF16)

    def silu_tile():
        for cols in chunks:
            o_ref[:, cols] = _silu(proj(cols)).astype(BF16)

    if not with_q:
        plain_tile()
    else:
        pl.when(j == 0)(kv_tile if rope else plain_tile)
        pl.when(j == GLU_TILE)(glu_tile)
        pl.when(j == Q_TILE)(q_tile)
        pl.when((j != 0) & (j != GLU_TILE) & (j != Q_TILE))(silu_tile)


def _inproj_call(x2, norm_g, ada4, layer, w_bf, rope_tabs, *, tm, rows_per_batch, ada_row_fn,
                 with_q):
    m, d = x2.shape
    tiles_per_batch = rows_per_batch // tm
    rope = rope_tabs is not None
    n_tiles = D_IN // IN_TILE_N if with_q else 1

    def ada_spec(part):
        return pl.BlockSpec((None, None, 1, d),
                            lambda i, j: (layer, ada_row_fn(i // tiles_per_batch), 0, part))

    in_specs = [
        pl.BlockSpec((tm, d), lambda i, j: (i, 0)),
        pl.BlockSpec((None, 1, d), lambda i, j: (layer, 0, 0)),
        ada_spec(0), ada_spec(1),
        pl.BlockSpec((None, d, IN_TILE_N), lambda i, j: (layer, 0, j)),
    ]
    args = [x2, norm_g, ada4, ada4, w_bf]
    if rope:
        tab_spec = pl.BlockSpec((tm, HEAD_DIM), lambda i, j: (i % tiles_per_batch, 0))
        in_specs += [tab_spec] * 3
        args += list(rope_tabs)
    out_specs = [pl.BlockSpec((tm, IN_TILE_N), lambda i, j: (i, jnp.where(j > Q_TILE, j - 1, j)))]
    out_shape = [jax.ShapeDtypeStruct((m, D_U if with_q else IN_TILE_N), BF16)]
    if with_q:
        out_specs.append(pl.BlockSpec((tm // WINDOW, N_Q_HEADS, WINDOW, HEAD_DIM),
                                      lambda i, j: (i, 0, 0, 0)))
        out_shape.append(jax.ShapeDtypeStruct((m // WINDOW, N_Q_HEADS, WINDOW, HEAD_DIM), BF16))
    outs = pl.pallas_call(
        functools.partial(_inproj_kernel, rope=rope, with_q=with_q, tm=tm),
        grid=(m // tm, n_tiles),
        in_specs=in_specs,
        out_specs=out_specs,
        out_shape=out_shape,
        scratch_shapes=[pltpu.VMEM((tm, d), BF16)],
        compiler_params=pltpu.CompilerParams(
            dimension_semantics=("arbitrary", "arbitrary"), vmem_limit_bytes=VMEM_LIMIT),
        name="in_proj",
    )(*args)
    return (outs[0], outs[1]) if with_q else (outs[0], None)


def _lru_kernel(*refs, reverse, tt, units):
    if reverse:
        (x_ref, h0_ref, acc_ref, pmat_ref, cw_ref, cb_ref, wg_ref, br_ref, bi_ref, lam_ref,
         o_ref, state_ref, ext3_ref, a3_ref, b3_ref, edge_ref, h_ref) = refs
        xp_ref = None
    else:
        (x_ref, h0_ref, pmat_ref, cw_ref, cb_ref, wg_ref, br_ref, bi_ref, lam_ref,
         o_ref, xp_ref, state_ref, ext3_ref, a3_ref, b3_ref, edge_ref, h_ref) = refs
        acc_ref = None
    c = pl.program_id(1)
    tu = tt // units
    seg = tu // SUBLANES
    hist = LRU_CONV_WIDTH - 1
    row = lax.broadcasted_iota(jnp.int32, (SUBLANES, D_LRU), 0)

    def blk(g):
        return slice(g * SUBLANES, (g + 1) * SUBLANES)

    @pl.when(c == 0)
    def _():
        edge_ref[...] = jnp.zeros((hist * SUBLANES, D_LRU), F32)
        h_ref[...] = h0_ref[...]

    lam = lam_ref[...]
    softplus_neg_lam = jnp.maximum(-lam, 0.0) + jnp.log1p(jnp.exp(-jnp.abs(lam)))
    half_k = (-0.5 * LRU_C) * softplus_neg_lam
    half_br = 0.5 * br_ref[...]
    half_bi = 0.5 * bi_ref[...]

    order = list(range(units - 1, -1, -1) if reverse else range(units))
    for u in order:
        _lru_gates(slice(u * tu, (u + 1) * tu), x_ref, xp_ref, pmat_ref, cw_ref, cb_ref, wg_ref,
                   half_br, half_bi, half_k, ext3_ref.at[u], a3_ref.at[u], b3_ref.at[u], edge_ref,
                   reverse=reverse, tt=tu, seg=seg, hist=hist, row=row, blk=blk)

    def step(t, carry):
        g_ = (seg - 1 - t) if reverse else t
        rows = pl.ds(pl.multiple_of(g_ * SUBLANES, SUBLANES), SUBLANES)
        new = []
        for u, (h, p) in zip(order, carry):
            a = a3_ref[u, rows, :]
            h = a * h + b3_ref[u, rows, :]
            p = a * p
            b3_ref[u, rows, :] = h
            a3_ref[u, rows, :] = p
            new.append((h, p))
        return tuple(new)

    init = tuple((jnp.zeros((SUBLANES, D_LRU), F32), jnp.ones((SUBLANES, D_LRU), F32))
                 for _ in order)
    finals = lax.fori_loop(0, seg, step, init, unroll=8)

    for u, (h_fin, p_fin) in zip(order, finals):
        _lru_finish(slice(u * tu, (u + 1) * tu), h_fin, p_fin, acc_ref, pmat_ref, o_ref,
                    a3_ref.at[u], b3_ref.at[u], h_ref, reverse=reverse, tt=tu, seg=seg, row=row)
    state_ref[...] = h_ref[...]


def _lru_gates(rows_u, x_ref, xp_ref, pmat_ref, cw_ref, cb_ref, wg_ref, half_br, half_bi, half_k,
               ext_ref, a_ref, b_ref, edge_ref, *, reverse, tt, seg, hist, row, blk):
    base = 0 if reverse else hist
    if reverse:
        xp = x_ref[rows_u, :].astype(F32)
    else:
        xp = jnp.dot(pmat_ref[...], x_ref[rows_u, :], preferred_element_type=F32)
        xp_ref[rows_u, :] = xp.astype(BF16)
    ext_ref[base * SUBLANES:base * SUBLANES + tt, :] = xp
    for j in range(1, hist + 1):
        if reverse:
            rolled = pltpu.roll(ext_ref[blk(j - 1), :], SUBLANES - 1, 0)
            ext_ref[blk(seg + j - 1), :] = jnp.where(row == SUBLANES - 1, edge_ref[blk(j - 1), :],
                                                     rolled)
        else:
            rolled = pltpu.roll(ext_ref[blk(hist + seg - j), :], 1, 0)
            ext_ref[blk(hist - j), :] = jnp.where(row == 0, edge_ref[blk(j - 1), :], rolled)
        edge_ref[blk(j - 1), :] = rolled

    xc = jnp.broadcast_to(cb_ref[...], (tt, D_LRU))
    for k in range(LRU_CONV_WIDTH):
        off = (hist - k) if reverse else k
        xc = xc + cw_ref[k:k + 1, :] * ext_ref[off * SUBLANES:off * SUBLANES + tt, :]

    g = jnp.dot(xc.astype(BF16), wg_ref[...], preferred_element_type=F32)
    t_r = jnp.tanh(g[:, :D_LRU] + half_br)
    t_i = jnp.tanh(g[:, D_LRU:] + half_bi)
    log_a = half_k * t_r + half_k
    a_ref[...] = jnp.exp(log_a)
    th = jnp.tanh(log_a)
    one_minus_a2 = (-2.0 * th) / (1.0 - th)
    half_xc = 0.5 * xc
    b_ref[...] = jnp.sqrt(one_minus_a2) * (half_xc * t_i + half_xc)


def _lru_finish(rows_u, h_fin, p_fin, acc_ref, pmat_ref, o_ref, a_ref, b_ref, h_ref, *, reverse,
                tt, seg, row):
    a, b = p_fin, h_fin
    for d in (1, 2, 4):
        keep = (row < SUBLANES - d) if reverse else (row >= d)
        shift = (SUBLANES - d) if reverse else d
        a_s = jnp.where(keep, pltpu.roll(a, shift, 0), 1.0)
        b_s = jnp.where(keep, pltpu.roll(b, shift, 0), 0.0)
        b = a * b_s + b
        a = a * a_s
    h_in = h_ref[...]
    end = a * h_in + b
    if reverse:
        carry_in = jnp.where(row < SUBLANES - 1, pltpu.roll(end, SUBLANES - 1, 0), h_in)
        h_ref[...] = jnp.broadcast_to(end[0:1, :], (SUBLANES, D_LRU))
    else:
        carry_in = jnp.where(row >= 1, pltpu.roll(end, 1, 0), h_in)
        h_ref[...] = jnp.broadcast_to(end[SUBLANES - 1:SUBLANES, :], (SUBLANES, D_LRU))

    h_all = (b_ref[...].reshape(seg, SUBLANES, D_LRU)
             + a_ref[...].reshape(seg, SUBLANES, D_LRU) * carry_in[None]).reshape(tt, D_LRU)
    if reverse:
        h_sum = (h_all + acc_ref[rows_u, :].astype(F32)).astype(BF16)
        o_ref[rows_u, :] = jnp.dot(pmat_ref[...], h_sum, preferred_element_type=F32
                                   ).astype(o_ref.dtype)
    else:
        o_ref[rows_u, :] = h_all.astype(o_ref.dtype)


def _lru_call(x_rows, h0, fwd, params, layer, direction, *, rows_per_batch, tt, units):
    cw, cb, wg_half, br, bi, lam = params
    reverse = direction == 1
    m = x_rows.shape[0] if not reverse else fwd[0].shape[0]
    nb = m // rows_per_batch
    nc = rows_per_batch // tt
    tu = tt // units

    def chunk(c):
        return (nc - 1 - c) if reverse else c

    row_spec = pl.BlockSpec((tt, D_LRU), lambda b, c: (b * nc + chunk(c), 0))
    full = lambda shape: pl.BlockSpec(shape, lambda b, c: (0,) * len(shape))
    h0_spec = pl.BlockSpec((None, SUBLANES, D_LRU), lambda b, c: (b, 0, 0))
    seg = tu // SUBLANES
    src = (np.arange(tu) % SUBLANES) * seg + np.arange(tu) // SUBLANES
    perm = src[:, None] == np.arange(tu)[None, :]
    if reverse:
        in_specs = [row_spec, h0_spec, row_spec]
        args = [fwd[0], h0, fwd[1], jnp.asarray(perm.T, BF16)]
    else:
        in_specs = [row_spec, h0_spec]
        args = [x_rows, h0, jnp.asarray(perm, BF16)]
    sel = lambda shape: pl.BlockSpec((None, None) + shape,
                                     lambda b, c: (layer, direction) + (0,) * len(shape))
    in_specs += [full((tu, tu)),
                 sel((LRU_CONV_WIDTH, D_LRU)), sel((1, D_LRU)), sel((D_LRU, 2 * D_LRU)),
                 sel((1, D_LRU)), sel((1, D_LRU)), sel((1, D_LRU))]
    args += [cw, cb, wg_half, br, bi, lam]
    edge_rows = (LRU_CONV_WIDTH - 1) * SUBLANES
    rows_out = jax.ShapeDtypeStruct((m, D_LRU), BF16)
    state_spec = pl.BlockSpec((None, SUBLANES, D_LRU), lambda b, c: (b, 0, 0))
    state_out = jax.ShapeDtypeStruct((nb, SUBLANES, D_LRU), F32)
    return pl.pallas_call(
        functools.partial(_lru_kernel, reverse=reverse, tt=tt, units=units),
        grid=(nb, nc),
        in_specs=in_specs,
        out_specs=[row_spec, state_spec] if reverse else [row_spec, row_spec, state_spec],
        out_shape=[rows_out, state_out] if reverse else [rows_out, rows_out, state_out],
        scratch_shapes=[pltpu.VMEM((units, tu + edge_rows, D_LRU), F32),
                        pltpu.VMEM((units, tu, D_LRU), F32),
                        pltpu.VMEM((units, tu, D_LRU), F32),
                        pltpu.VMEM((edge_rows, D_LRU), F32),
                        pltpu.VMEM((SUBLANES, D_LRU), F32)],
        compiler_params=pltpu.CompilerParams(
            dimension_semantics=("arbitrary", "arbitrary"), vmem_limit_bytes=VMEM_LIMIT),
        name="rglru_rev" if reverse else "rglru_fwd",
    )(*args)


def _mix_kernel(*refs, names, tq, local, final):
    r = dict(zip(names, refs))
    i = pl.program_id(1)
    nt = pl.num_programs(1)
    ext_ref, ycat_ref = r["ext"], r["ycat"]

    def glu(ref):
        return ref[...].astype(F32)

    ext_ref[CONV_HALO:CONV_HALO + tq, :] = glu(r["glu"])
    zero_halo = jnp.zeros((CONV_HALO, D_CONV), F32)
    if local:
        ext_ref[0:CONV_HALO, :] = jnp.where(i > 0, glu(r["glu_prev"]), zero_halo)
        ext_ref[CONV_HALO + tq:, :] = jnp.where(i < nt - 1, glu(r["glu_next"]), zero_halo)
    else:
        ext_ref[0:CONV_HALO, :] = zero_halo
        ext_ref[CONV_HALO + tq:, :] = zero_halo

    for rc in range(tq // CONV_ROWS):
        t0 = rc * CONV_ROWS
        rows = slice(t0, t0 + CONV_ROWS)
        pieces = []
        for cc in range(D_CONV // CONV_COLS):
            cols = slice(cc * CONV_COLS, (cc + 1) * CONV_COLS)
            acc = jnp.broadcast_to(r["dw_b"][:, cols], (CONV_ROWS, CONV_COLS))
            for b in range(SUBLANES):
                z = None
                for a in range(-(-(CONV_WIDTH + 1) // SUBLANES)):
                    o = SUBLANES * a + b
                    if o < 1 or o > CONV_WIDTH:
                        continue
                    xs = ext_ref[t0 + SUBLANES * a:t0 + SUBLANES * a + CONV_ROWS + SUBLANES, cols]
                    term = (xs.reshape(-1, SUBLANES, CONV_COLS) * r["dw_w"][o - 1, :, cols][None]
                            ).reshape(CONV_ROWS + SUBLANES, CONV_COLS)
                    z = term if z is None else z + term
                acc = acc + z[b:b + CONV_ROWS, :]
            pieces.append(acc)
        acc = jnp.concatenate(pieces, axis=1)
        mu = jnp.mean(acc, axis=-1, keepdims=True)
        cen = acc - mu
        var = jnp.mean(cen * cen, axis=-1, keepdims=True)
        y = cen * lax.rsqrt(var + EPS) * r["ln_g"][...] + r["ln_b"][...]
        y = _silu(y).astype(BF16)
        y = jnp.dot(y, r["pw_w"][...], preferred_element_type=F32) + r["pw_b"][...]
        y = y * r["conv_gate"][rows, :].astype(F32)
        ycat_ref[rows, 0:D_CONV] = y.astype(BF16)

    ylru = r["ylru"][...].astype(F32)
    if "ylru_rev" in r:
        ylru = ylru + r["ylru_rev"][...].astype(F32)
    ycat_ref[:, D_CONV:D_CONV + D_LRU] = (
        ylru * r["lru_gate"][...].astype(F32)).astype(BF16)

    qrows = GQA_GROUP * WINDOW
    qi = lax.broadcasted_iota(jnp.int32, (qrows, WINDOW), 0) % WINDOW
    kj = lax.broadcasted_iota(jnp.int32, (qrows, WINDOW), 1)
    neg_inf = jnp.float32(-jnp.inf)
    dn = (((1,), (1,)), ((), ()))
    n_qb = tq // WINDOW
    lc = r["kc"].shape[0]
    for qb in range(n_qb):
        rows = slice(qb * WINDOW, (qb + 1) * WINDOW)
        for g in range(N_KV_HEADS):
            kv_cols = slice(g * HEAD_DIM, (g + 1) * HEAD_DIM)
            q_st = r["q"][qb, g * GQA_GROUP:(g + 1) * GQA_GROUP].reshape(qrows, HEAD_DIM)
            sink = jnp.concatenate(
                [jnp.broadcast_to(r["sink"][g * GQA_GROUP + hh:g * GQA_GROUP + hh + 1, 0:1],
                                  (WINDOW, 1)) for hh in range(GQA_GROUP)], axis=0) * LOG2E
            blocks = []
            for cb in range(lc // WINDOW):
                crow = slice(cb * WINDOW, (cb + 1) * WINDOW)
                s = lax.dot_general(q_st, r["kc"][crow, kv_cols], dn, preferred_element_type=F32)
                blocks.append((s, r["vc"][crow, kv_cols]))
            if local:
                if qb == 0:
                    k_p, v_p = r["k_prev"][:, kv_cols], r["v_prev"][:, kv_cols]
                    ok_p = i > 0
                else:
                    prows = slice((qb - 1) * WINDOW, qb * WINDOW)
                    k_p, v_p = r["k"][prows, kv_cols], r["v"][prows, kv_cols]
                    ok_p = True
                if qb == n_qb - 1:
                    k_n, v_n = r["k_next"][:, kv_cols], r["v_next"][:, kv_cols]
                    ok_n = i < nt - 1
                else:
                    nrows = slice((qb + 1) * WINDOW, (qb + 2) * WINDOW)
                    k_n, v_n = r["k"][nrows, kv_cols], r["v"][nrows, kv_cols]
                    ok_n = True
                s_p = lax.dot_general(q_st, k_p, dn, preferred_element_type=F32)
                s_c = lax.dot_general(q_st, r["k"][rows, kv_cols], dn, preferred_element_type=F32)
                s_n = lax.dot_general(q_st, k_n, dn, preferred_element_type=F32)
                s_p = jnp.where((kj >= qi) & ok_p, s_p, neg_inf)
                s_n = jnp.where((kj <= qi) & ok_n, s_n, neg_inf)
                blocks += [(s_p, v_p), (s_c, r["v"][rows, kv_cols]), (s_n, v_n)]
            m_el = blocks[0][0]
            for s, _ in blocks[1:]:
                m_el = jnp.maximum(m_el, s)
            m = jnp.maximum(jnp.max(m_el, axis=-1, keepdims=True), sink)
            den_el = None
            o = None
            for s, v in blocks:
                p = jnp.exp2(s - m)
                den_el = p if den_el is None else den_el + p
                pv = jnp.dot(p.astype(BF16), v, preferred_element_type=F32)
                o = pv if o is None else o + pv
            den = jnp.sum(den_el, axis=-1, keepdims=True) + jnp.exp2(sink - m)
            o = o / den
            for hh in range(GQA_GROUP):
                c0 = (g * GQA_GROUP + hh) * HEAD_DIM
                gate = r["attn_gate"][rows, c0:c0 + HEAD_DIM].astype(F32)
                ycat_ref[rows, D_CONV + D_LRU + c0:D_CONV + D_LRU + c0 + HEAD_DIM] = (
                    o[hh * WINDOW:(hh + 1) * WINDOW, :] * gate).astype(BF16)

    out_ref = r["out"]
    for c0 in range(0, out_ref.shape[1], OUT_CHUNK_N):
        cols = slice(c0, c0 + OUT_CHUNK_N)
        y = jnp.dot(ycat_ref[...], r["w_out"][:, cols], preferred_element_type=F32)
        out_ref[:, cols] = r["x"][:, cols] + r["gate"][:, cols] * y
    if final:
        xn = out_ref[...]
        ms = jnp.mean(xn * xn, axis=-1, keepdims=True)
        out_ref[...] = xn * lax.rsqrt(ms + EPS) * r["final_g"][...]


def _mix_call(x2, u2, q4, uc2, ylru, ada4, layer, ada_row_fn, wts, *, tq, rows_per_batch,
              ctx_rows, local, final):
    m, d = x2.shape
    nt = rows_per_batch // tq
    nb = m // rows_per_batch
    names, specs, args = [], [], []

    def add(name, arr, spec):
        names.append(name)
        specs.append(spec)
        args.append(arr)

    def rowblk(width, col_off):
        return pl.BlockSpec((tq, width), lambda b, i: (b * nt + i, col_off // width))

    def halo(rows, width, col_off, nxt):
        per = tq // rows
        last = m // rows - 1
        if nxt:
            fn = lambda b, i: (jnp.minimum((b * nt + i + 1) * per, last), col_off // width)
        else:
            fn = lambda b, i: (jnp.maximum((b * nt + i) * per - 1, 0), col_off // width)
        return pl.BlockSpec((rows, width), fn)

    full = lambda shape: pl.BlockSpec(shape, lambda b, i: (0,) * len(shape),
                                      pipeline_mode=pl.Buffered(1))

    add("x", x2, pl.BlockSpec((tq, d), lambda b, i: (b * nt + i, 0)))
    add("gate", ada4, pl.BlockSpec((None, None, 1, d), lambda b, i: (layer, ada_row_fn(b), 0, 2)))
    add("glu", u2, rowblk(D_CONV, OFF_CONV_GLU))
    if local:
        add("glu_prev", u2, halo(CONV_HALO, D_CONV, OFF_CONV_GLU, False))
        add("glu_next", u2, halo(CONV_HALO, D_CONV, OFF_CONV_GLU, True))
    add("conv_gate", u2, rowblk(D_CONV, OFF_CONV_GATE))
    add("lru_gate", u2, rowblk(D_LRU, OFF_LRU_GATE))
    ylru_spec = pl.BlockSpec((tq, D_LRU), lambda b, i: (b * nt + i, 0))
    if isinstance(ylru, tuple):
        add("ylru", ylru[0], ylru_spec)
        add("ylru_rev", ylru[1], ylru_spec)
    else:
        add("ylru", ylru, ylru_spec)
    add("q", q4, pl.BlockSpec((tq // WINDOW, N_Q_HEADS, WINDOW, HEAD_DIM),
                              lambda b, i: (b * nt + i, 0, 0, 0)))
    add("attn_gate", u2, rowblk(D_ATTN, U_ATTN_GATE))
    if local:
        add("k", u2, rowblk(D_KV, OFF_K))
        add("v", u2, rowblk(D_KV, OFF_V))
        add("k_prev", u2, halo(WINDOW, D_KV, OFF_K, False))
        add("v_prev", u2, halo(WINDOW, D_KV, OFF_V, False))
        add("k_next", u2, halo(WINDOW, D_KV, OFF_K, True))
        add("v_next", u2, halo(WINDOW, D_KV, OFF_V, True))
    add("kc", uc2, pl.BlockSpec((ctx_rows, D_KV), lambda b, i: (b, OFF_K // D_KV)))
    add("vc", uc2, pl.BlockSpec((ctx_rows, D_KV), lambda b, i: (b, OFF_V // D_KV)))
    for name in ("dw_w", "dw_b", "ln_g", "ln_b", "pw_w", "pw_b", "sink", "w_out"):
        shape = wts[name].shape[1:]
        add(name, wts[name], pl.BlockSpec((None,) + shape,
                                          lambda b, i, nd=len(shape): (layer,) + (0,) * nd,
                                          pipeline_mode=pl.Buffered(1)))
    if final:
        add("final_g", wts["final_g"], full(wts["final_g"].shape))
    names += ["out", "ext", "ycat"]
    return pl.pallas_call(
        functools.partial(_mix_kernel, names=tuple(names), tq=tq, local=local, final=final),
        grid=(nb, nt),
        in_specs=specs,
        out_specs=pl.BlockSpec((tq, d), lambda b, i: (b * nt + i, 0)),
        out_shape=jax.ShapeDtypeStruct((m, d), F32),
        scratch_shapes=[pltpu.VMEM((tq + 2 * CONV_HALO, D_CONV), F32),
                        pltpu.VMEM((tq, D_MODEL), BF16)],
        compiler_params=pltpu.CompilerParams(
            dimension_semantics=("arbitrary", "arbitrary"), vmem_limit_bytes=VMEM_LIMIT),
        name="mix_lat" if local else "mix_ctx",
    )(*args)


def _rope_tables(seq):
    rows = seq // GRID_W
    row = np.repeat(np.arange(rows, dtype=np.float64), GRID_W)
    col = np.tile(np.arange(GRID_W, dtype=np.float64), rows)
    half = HEAD_DIM // 2
    inv = ROPE_BASE ** (-np.arange(0, half, 2, dtype=np.float64) / half)
    ang_r = row[:, None] * inv[None, :]
    ang_c = col[:, None] * inv[None, :]
    ang = np.concatenate([ang_r, ang_r, ang_c, ang_c], axis=-1)
    cos, sin = np.cos(ang), np.sin(ang)
    first = (np.arange(HEAD_DIM) % half) < (half // 2)
    return tuple(jnp.asarray(t, F32)
                 for t in (cos, np.where(first, -sin, 0.0), np.where(first, 0.0, sin)))


def _block_diag(w):
    nblk, blk = w.shape[-3], w.shape[-2]
    eye = jnp.eye(nblk, dtype=w.dtype)
    dense = w[..., :, :, None, :] * eye[:, None, :, None]
    return dense.reshape(w.shape[:-3] + (nblk * blk, nblk * blk))


def _pick_tile(n, pref):
    t = min(n, pref)
    while n % t:
        t //= 2
    return t


def kernel(x, c, ctx, c_ctx, norm_g, w_ada, b_ada, w_in, dw_w, dw_b, ln_g, ln_b, pw_w, pw_b,
           lru_conv_w, lru_conv_b, lru_w_r, lru_b_r, lru_w_i, lru_b_i, lru_lam, attn_sink,
           w_out, final_g):
    nb, seq, d = x.shape
    lc = ctx.shape[1]
    depth = w_in.shape[0]
    ctx_row = nb

    c_rows = jnp.concatenate([c, c_ctx[None, :], jnp.zeros((ADA_ROWS - nb - 1, d), F32)], axis=0)
    ada = _ada_call(c_rows, w_ada, b_ada)
    ada4 = ada.reshape(depth, ADA_ROWS, 1, 3 * d)
    rope_tabs = _rope_tables(seq)
    w_bf = w_in.astype(BF16)
    norm_g3 = norm_g.reshape(depth, 1, d)
    vec4 = lambda p: p.reshape(depth, 2, 1, D_LRU)
    wg_half = (0.5 * jnp.concatenate([_block_diag(lru_w_r), _block_diag(lru_w_i)], axis=-1)
               ).astype(BF16)
    lru_params = (lru_conv_w, vec4(lru_conv_b), wg_half, vec4(lru_b_r), vec4(lru_b_i),
                  vec4(lru_lam))
    vec3 = lambda p: p.reshape(depth, 1, D_CONV)
    wts = {
        "dw_w": jnp.broadcast_to(dw_w[:, :, None, :], (depth, CONV_WIDTH, SUBLANES, D_CONV)),
        "dw_b": vec3(dw_b), "ln_g": vec3(ln_g), "ln_b": vec3(ln_b),
        "pw_w": pw_w.astype(BF16), "pw_b": vec3(pw_b),
        "sink": jnp.broadcast_to(attn_sink[:, :, None], (depth, N_Q_HEADS, HEAD_DIM)),
        "w_out": w_out.astype(BF16), "final_g": final_g.reshape(1, d),
    }

    x2 = x.reshape(nb * seq, d)
    xc2 = ctx.reshape(nb * lc, d)
    tm = _pick_tile(seq, 1024)
    tmc = _pick_tile(nb * lc, 1024)
    tq = _pick_tile(seq, 512)
    tt = _pick_tile(seq, 1024)
    tu = _pick_tile(tt, 512)
    zeros_h0 = jnp.zeros((nb, SUBLANES, D_LRU), F32)

    for l in range(depth):
        last = l == depth - 1
        u2, q4 = _inproj_call(x2, norm_g3, ada4, l, w_bf, rope_tabs, tm=tm, rows_per_batch=seq,
                              ada_row_fn=lambda b: b, with_q=True)
        uc2, qc4 = _inproj_call(xc2, norm_g3, ada4, l, w_bf, None, tm=tmc, rows_per_batch=tmc,
                                ada_row_fn=lambda b: ctx_row, with_q=not last)

        ctx_kw = dict(rows_per_batch=lc, tt=lc, units=1)
        lat_kw = dict(rows_per_batch=seq, tt=tt, units=tt // tu)
        hp_c, xp_c, st_f = _lru_call(uc2, zeros_h0, None, lru_params, l, 0, **ctx_kw)
        ylru_c, st_r = _lru_call(None, zeros_h0, (xp_c, hp_c), lru_params, l, 1, **ctx_kw)
        hp, xp, _ = _lru_call(u2, st_f, None, lru_params, l, 0, **lat_kw)
        ylru, _ = _lru_call(None, st_r, (xp, hp), lru_params, l, 1, **lat_kw)

        x2_new = _mix_call(x2, u2, q4, uc2, ylru, ada4, l, lambda b: b, wts, tq=tq,
                           rows_per_batch=seq, ctx_rows=lc, local=True, final=last)
        if not last:
            xc2 = _mix_call(xc2, uc2, qc4, uc2, ylru_c, ada4, l, lambda b: ctx_row, wts, tq=lc,
                            rows_per_batch=lc, ctx_rows=lc, local=False, final=False)
        x2 = x2_new
    return x2.reshape(nb, seq, d)
```

```python
import functools
import math

import jax
import jax.numpy as jnp
import numpy as np
from jax import lax
from jax.experimental import pallas as pl
from jax.experimental.pallas import tpu as pltpu

F32 = jnp.float32
BF16 = jnp.bfloat16

D_MODEL = 2048
D_CONV = 512
D_LRU = 512
HEAD_DIM = 128
N_Q_HEADS = 8
N_KV_HEADS = 2
GQA_GROUP = N_Q_HEADS // N_KV_HEADS
D_ATTN = N_Q_HEADS * HEAD_DIM
D_KV = N_KV_HEADS * HEAD_DIM
GRID_W = 64
CONV_WIDTH = 31
CONV_PAD = (CONV_WIDTH - 1) // 2
LRU_CONV_WIDTH = 4
LRU_BLOCKS = 8
LRU_C = 8.0
WINDOW = 128
ROPE_BASE = 10000.0
EPS = 1e-6

OFF_K = D_LRU
OFF_V = OFF_K + D_KV
OFF_MEM_END = OFF_V + D_KV
OFF_CONV_GLU = OFF_MEM_END
OFF_CONV_GATE = OFF_CONV_GLU + 2 * D_CONV
OFF_LRU_GATE = OFF_CONV_GATE + D_CONV
OFF_Q = OFF_LRU_GATE + D_LRU
OFF_ATTN_GATE = OFF_Q + D_ATTN
D_IN = OFF_ATTN_GATE + D_ATTN
U_ATTN_GATE = OFF_Q
D_U = D_IN - D_ATTN

ADA_ROWS = 8
IN_TILE_N = 1024
IN_CHUNK_N = 256
Q_TILE = OFF_Q // IN_TILE_N
GLU_TILE = OFF_CONV_GLU // IN_TILE_N
OUT_CHUNK_N = 256
CONV_HALO = 16
CONV_ROWS = 64
CONV_COLS = 256
SUBLANES = 8
LANES = 128
VMEM_LIMIT = 56 * 1024 * 1024
LOG2E = math.log2(math.e)
Q_SCALE = HEAD_DIM ** -0.5 * LOG2E

assert OFF_Q % IN_TILE_N == 0 and D_ATTN == IN_TILE_N and OFF_MEM_END == IN_TILE_N
assert OFF_CONV_GLU % IN_TILE_N == 0 and 2 * D_CONV == IN_TILE_N


def _sigmoid(x):
    return 0.5 * jnp.tanh(0.5 * x) + 0.5


def _silu(x):
    hx = 0.5 * x
    return hx * jnp.tanh(hx) + hx


def _ada_kernel(c_ref, w_ref, b_ref, o_ref):
    ca = _silu(c_ref[...])
    o_ref[...] = jnp.dot(ca.astype(BF16), w_ref[...].astype(BF16),
                         preferred_element_type=F32) + b_ref[...]


def _ada_call(c_rows, w_ada, b_ada):
    depth, d, n = w_ada.shape
    tn = 1024
    return pl.pallas_call(
        _ada_kernel,
        grid=(depth, n // tn),
        in_specs=[
            pl.BlockSpec((ADA_ROWS, d), lambda l, j: (0, 0)),
            pl.BlockSpec((None, d, tn), lambda l, j: (l, 0, j)),
            pl.BlockSpec((None, 1, tn), lambda l, j: (l, 0, j)),
        ],
        out_specs=pl.BlockSpec((None, ADA_ROWS, tn), lambda l, j: (l, 0, j)),
        out_shape=jax.ShapeDtypeStruct((depth, ADA_ROWS, n), F32),
        compiler_params=pltpu.CompilerParams(
            dimension_semantics=("arbitrary", "arbitrary"), vmem_limit_bytes=VMEM_LIMIT),
        name="ada_proj",
    )(c_rows, w_ada, b_ada.reshape(depth, 1, n))


def _rope(t, cos, sin_a, sin_b):
    return (t * cos + pltpu.roll(t, HEAD_DIM - 32, 1) * sin_a + pltpu.roll(t, 32, 1) * sin_b)


def _inproj_kernel(*refs, rope, with_q, tm):
    refs = list(refs)
    x_ref, g_ref, shift_ref, scale_ref, w_ref = refs[:5]
    del refs[:5]
    if rope:
        cos_ref, sa_ref, sb_ref = refs[:3]
        del refs[:3]
    o_ref = refs.pop(0)
    q_ref = refs.pop(0) if with_q else None
    h_ref = refs.pop(0)
    j = pl.program_id(1)

    @pl.when(j == 0)
    def _():
        x = x_ref[...]
        ms = jnp.mean(x * x, axis=-1, keepdims=True)
        gain = g_ref[...] * (1.0 + scale_ref[...])
        h_ref[...] = (x * lax.rsqrt(ms + EPS) * gain + shift_ref[...]).astype(BF16)

    chunks = [slice(c0, c0 + IN_CHUNK_N) for c0 in range(0, IN_TILE_N, IN_CHUNK_N)]
    heads_per_chunk = IN_CHUNK_N // HEAD_DIM

    def proj(cols):
        return jnp.dot(h_ref[...], w_ref[:, cols], preferred_element_type=F32)

    def rope_heads(u):
        if not rope:
            return [u[:, hh * HEAD_DIM:(hh + 1) * HEAD_DIM] for hh in range(heads_per_chunk)]
        return [_rope(u[:, hh * HEAD_DIM:(hh + 1) * HEAD_DIM], cos_ref[...], sa_ref[...],
                      sb_ref[...]) for hh in range(heads_per_chunk)]

    def plain_tile():
        for cols in chunks:
            o_ref[:, cols] = proj(cols).astype(BF16)

    def q_tile():
        for ci, cols in enumerate(chunks):
            for hh, t in enumerate(rope_heads(proj(cols))):
                t = (t * Q_SCALE).astype(BF16)
                for qb in range(tm // WINDOW):
                    q_ref[qb, ci * heads_per_chunk + hh] = t[qb * WINDOW:(qb + 1) * WINDOW, :]

    def kv_tile():
        for cols in chunks:
            u = proj(cols)
            if OFF_K <= cols.start < OFF_V:
                u = jnp.concatenate(rope_heads(u), axis=1)
            o_ref[:, cols] = u.astype(BF16)

    def glu_tile():
        half = len(chunks) // 2
        for ci in range(half):
            o_ref[:, chunks[ci]] = (proj(chunks[ci]) * _sigmoid(proj(chunks[ci + half]))
                                    ).astype(BF16)
            o_ref[:, chunks[ci + half]] = jnp.zeros((tm, IN_CHUNK_N), BF16)

    def silu_tile():
        for cols in chunks:
            o_ref[:, cols] = _silu(proj(cols)).astype(BF16)

    if not with_q:
        plain_tile()
    else:
        pl.when(j == 0)(kv_tile if rope else plain_tile)
        pl.when(j == GLU_TILE)(glu_tile)
        pl.when(j == Q_TILE)(q_tile)
        pl.when((j != 0) & (j != GLU_TILE) & (j != Q_TILE))(silu_tile)


def _inproj_call(x2, norm_g, ada4, layer, w_bf, rope_tabs, *, tm, rows_per_batch, ada_row_fn,
                 with_q):
    m, d = x2.shape
    tiles_per_batch = rows_per_batch // tm
    rope = rope_tabs is not None
    n_tiles = D_IN // IN_TILE_N if with_q else 1

    def ada_spec(part):
        return pl.BlockSpec((None, None, 1, d),
                            lambda i, j: (layer, ada_row_fn(i // tiles_per_batch), 0, part))

    in_specs = [
        pl.BlockSpec((tm, d), lambda i, j: (i, 0)),
        pl.BlockSpec((None, 1, d), lambda i, j: (layer, 0, 0)),
        ada_spec(0), ada_spec(1),
        pl.BlockSpec((None, d, IN_TILE_N), lambda i, j: (layer, 0, j)),
    ]
    args = [x2, norm_g, ada4, ada4, w_bf]
    if rope:
        tab_spec = pl.BlockSpec((tm, HEAD_DIM), lambda i, j: (i % tiles_per_batch, 0))
        in_specs += [tab_spec] * 3
        args += list(rope_tabs)
    out_specs = [pl.BlockSpec((tm, IN_TILE_N), lambda i, j: (i, jnp.where(j > Q_TILE, j - 1, j)))]
    out_shape = [jax.ShapeDtypeStruct((m, D_U if with_q else IN_TILE_N), BF16)]
    if with_q:
        out_specs.append(pl.BlockSpec((tm // WINDOW, N_Q_HEADS, WINDOW, HEAD_DIM),
                                      lambda i, j: (i, 0, 0, 0)))
        out_shape.append(jax.ShapeDtypeStruct((m // WINDOW, N_Q_HEADS, WINDOW, HEAD_DIM), BF16))
    outs = pl.pallas_call(
        functools.partial(_inproj_kernel, rope=rope, with_q=with_q, tm=tm),
        grid=(m // tm, n_tiles),
        in_specs=in_specs,
        out_specs=out_specs,
        out_shape=out_shape,
        scratch_shapes=[pltpu.VMEM((tm, d), BF16)],
        compiler_params=pltpu.CompilerParams(
            dimension_semantics=("arbitrary", "arbitrary"), vmem_limit_bytes=VMEM_LIMIT),
        name="in_proj",
    )(*args)
    return (outs[0], outs[1]) if with_q else (outs[0], None)


def _lru_kernel(*refs, reverse, tt, units):
    if reverse:
        (x_ref, h0_ref, acc_ref, pmat_ref, cw_ref, cb_ref, wg_ref, br_ref, bi_ref, lam_ref,
         o_ref, state_ref, ext3_ref, a3_ref, b3_ref, edge_ref, h_ref) = refs
        xp_ref = None
    else:
        (x_ref, h0_ref, pmat_ref, cw_ref, cb_ref, wg_ref, br_ref, bi_ref, lam_ref,
         o_ref, xp_ref, state_ref, ext3_ref, a3_ref, b3_ref, edge_ref, h_ref) = refs
        acc_ref = None
    c = pl.program_id(1)
    tu = tt // units
    seg = tu // SUBLANES
    hist = LRU_CONV_WIDTH - 1
    row = lax.broadcasted_iota(jnp.int32, (SUBLANES, D_LRU), 0)

    def blk(g):
        return slice(g * SUBLANES, (g + 1) * SUBLANES)

    @pl.when(c == 0)
    def _():
        edge_ref[...] = jnp.zeros((hist * SUBLANES, D_LRU), F32)
        h_ref[...] = h0_ref[...]

    lam = lam_ref[...]
    softplus_neg_lam = jnp.maximum(-lam, 0.0) + jnp.log1p(jnp.exp(-jnp.abs(lam)))
    half_k = (-0.5 * LRU_C) * softplus_neg_lam
    half_br = 0.5 * br_ref[...]
    half_bi = 0.5 * bi_ref[...]

    order = list(range(units - 1, -1, -1) if reverse else range(units))
    for u in order:
        _lru_gates(slice(u * tu, (u + 1) * tu), x_ref, xp_ref, pmat_ref, cw_ref, cb_ref, wg_ref,
                   half_br, half_bi, half_k, ext3_ref.at[u], a3_ref.at[u], b3_ref.at[u], edge_ref,
                   reverse=reverse, tt=tu, seg=seg, hist=hist, row=row, blk=blk)

    def step(t, carry):
        g_ = (seg - 1 - t) if reverse else t
        rows = pl.ds(pl.multiple_of(g_ * SUBLANES, SUBLANES), SUBLANES)
        new = []
        for u, (h, p) in zip(order, carry):
            a = a3_ref[u, rows, :]
            h = a * h + b3_ref[u, rows, :]
            p = a * p
            b3_ref[u, rows, :] = h
            a3_ref[u, rows, :] = p
            new.append((h, p))
        return tuple(new)

    init = tuple((jnp.zeros((SUBLANES, D_LRU), F32), jnp.ones((SUBLANES, D_LRU), F32))
                 for _ in order)
    finals = lax.fori_loop(0, seg, step, init, unroll=8)

    for u, (h_fin, p_fin) in zip(order, finals):
        _lru_finish(slice(u * tu, (u + 1) * tu), h_fin, p_fin, acc_ref, pmat_ref, o_ref,
                    a3_ref.at[u], b3_ref.at[u], h_ref, reverse=reverse, tt=tu, seg=seg, row=row)
    state_ref[...] = h_ref[...]


def _lru_gates(rows_u, x_ref, xp_ref, pmat_ref, cw_ref, cb_ref, wg_ref, half_br, half_bi, half_k,
               ext_ref, a_ref, b_ref, edge_ref, *, reverse, tt, seg, hist, row, blk):
    base = 0 if reverse else hist
    if reverse:
        xp = x_ref[rows_u, :].astype(F32)
    else:
        xp = jnp.dot(pmat_ref[...], x_ref[rows_u, :], preferred_element_type=F32)
        xp_ref[rows_u, :] = xp.astype(BF16)
    ext_ref[base * SUBLANES:base * SUBLANES + tt, :] = xp
    for j in range(1, hist + 1):
        if reverse:
            rolled = pltpu.roll(ext_ref[blk(j - 1), :], SUBLANES - 1, 0)
            ext_ref[blk(seg + j - 1), :] = jnp.where(row == SUBLANES - 1, edge_ref[blk(j - 1), :],
                                                     rolled)
        else:
            rolled = pltpu.roll(ext_ref[blk(hist + seg - j), :], 1, 0)
            ext_ref[blk(hist - j), :] = jnp.where(row == 0, edge_ref[blk(j - 1), :], rolled)
        edge_ref[blk(j - 1), :] = rolled

    xc = jnp.broadcast_to(cb_ref[...], (tt, D_LRU))
    for k in range(LRU_CONV_WIDTH):
        off = (hist - k) if reverse else k
        xc = xc + cw_ref[k:k + 1, :] * ext_ref[off * SUBLANES:off * SUBLANES + tt, :]

    g = jnp.dot(xc.astype(BF16), wg_ref[...], preferred_element_type=F32)
    t_r = jnp.tanh(g[:, :D_LRU] + half_br)
    t_i = jnp.tanh(g[:, D_LRU:] + half_bi)
    log_a = half_k * t_r + half_k
    a_ref[...] = jnp.exp(log_a)
    th = jnp.tanh(log_a)
    one_minus_a2 = (-2.0 * th) / (1.0 - th)
    half_xc = 0.5 * xc
    b_ref[...] = jnp.sqrt(one_minus_a2) * (half_xc * t_i + half_xc)


def _lru_finish(rows_u, h_fin, p_fin, acc_ref, pmat_ref, o_ref, a_ref, b_ref, h_ref, *, reverse,
                tt, seg, row):
    a, b = p_fin, h_fin
    for d in (1, 2, 4):
        keep = (row < SUBLANES - d) if reverse else (row >= d)
        shift = (SUBLANES - d) if reverse else d
        a_s = jnp.where(keep, pltpu.roll(a, shift, 0), 1.0)
        b_s = jnp.where(keep, pltpu.roll(b, shift, 0), 0.0)
        b = a * b_s + b
        a = a * a_s
    h_in = h_ref[...]
    end = a * h_in + b
    if reverse:
        carry_in = jnp.where(row < SUBLANES - 1, pltpu.roll(end, SUBLANES - 1, 0), h_in)
        h_ref[...] = jnp.broadcast_to(end[0:1, :], (SUBLANES, D_LRU))
    else:
        carry_in = jnp.where(row >= 1, pltpu.roll(end, 1, 0), h_in)
        h_ref[...] = jnp.broadcast_to(end[SUBLANES - 1:SUBLANES, :], (SUBLANES, D_LRU))

    h_all = (b_ref[...].reshape(seg, SUBLANES, D_LRU)
             + a_ref[...].reshape(seg, SUBLANES, D_LRU) * carry_in[None]).reshape(tt, D_LRU)
    if reverse:
        h_sum = (h_all + acc_ref[rows_u, :].astype(F32)).astype(BF16)
        o_ref[rows_u, :] = jnp.dot(pmat_ref[...], h_sum, preferred_element_type=F32
                                   ).astype(o_ref.dtype)
    else:
        o_ref[rows_u, :] = h_all.astype(o_ref.dtype)


def _lru_call(x_rows, h0, fwd, params, layer, direction, *, rows_per_batch, tt, units):
    cw, cb, wg_half, br, bi, lam = params
    reverse = direction == 1
    m = x_rows.shape[0] if not reverse else fwd[0].shape[0]
    nb = m // rows_per_batch
    nc = rows_per_batch // tt
    tu = tt // units

    def chunk(c):
        return (nc - 1 - c) if reverse else c

    row_spec = pl.BlockSpec((tt, D_LRU), lambda b, c: (b * nc + chunk(c), 0))
    full = lambda shape: pl.BlockSpec(shape, lambda b, c: (0,) * len(shape))
    h0_spec = pl.BlockSpec((None, SUBLANES, D_LRU), lambda b, c: (b, 0, 0))
    seg = tu // SUBLANES
    src = (np.arange(tu) % SUBLANES) * seg + np.arange(tu) // SUBLANES
    perm = src[:, None] == np.arange(tu)[None, :]
    if reverse:
        in_specs = [row_spec, h0_spec, row_spec]
        args = [fwd[0], h0, fwd[1], jnp.asarray(perm.T, BF16)]
    else:
        in_specs = [row_spec, h0_spec]
        args = [x_rows, h0, jnp.asarray(perm, BF16)]
    sel = lambda shape: pl.BlockSpec((None, None) + shape,
                                     lambda b, c: (layer, direction) + (0,) * len(shape))
    in_specs += [full((tu, tu)),
                 sel((LRU_CONV_WIDTH, D_LRU)), sel((1, D_LRU)), sel((D_LRU, 2 * D_LRU)),
                 sel((1, D_LRU)), sel((1, D_LRU)), sel((1, D_LRU))]
    args += [cw, cb, wg_half, br, bi, lam]
    edge_rows = (LRU_CONV_WIDTH - 1) * SUBLANES
    rows_out = jax.ShapeDtypeStruct((m, D_LRU), BF16)
    state_spec = pl.BlockSpec((None, SUBLANES, D_LRU), lambda b, c: (b, 0, 0))
    state_out = jax.ShapeDtypeStruct((nb, SUBLANES, D_LRU), F32)
    return pl.pallas_call(
        functools.partial(_lru_kernel, reverse=reverse, tt=tt, units=units),
        grid=(nb, nc),
        in_specs=in_specs,
        out_specs=[row_spec, state_spec] if reverse else [row_spec, row_spec, state_spec],
        out_shape=[rows_out, state_out] if reverse else [rows_out, rows_out, state_out],
        scratch_shapes=[pltpu.VMEM((units, tu + edge_rows, D_LRU), F32),
                        pltpu.VMEM((units, tu, D_LRU), F32),
                        pltpu.VMEM((units, tu, D_LRU), F32),
                        pltpu.VMEM((edge_rows, D_LRU), F32),
                        pltpu.VMEM((SUBLANES, D_LRU), F32)],
        compiler_params=pltpu.CompilerParams(
            dimension_semantics=("arbitrary", "arbitrary"), vmem_limit_bytes=VMEM_LIMIT),
        name="rglru_rev" if reverse else "rglru_fwd",
    )(*args)


def _mix_kernel(*refs, names, tq, local, final):
    r = dict(zip(names, refs))
    i = pl.program_id(1)
    nt = pl.num_programs(1)
    ext_ref, ycat_ref = r["ext"], r["ycat"]

    def glu(ref):
        return ref[...].astype(F32)

    ext_ref[CONV_HALO:CONV_HALO + tq, :] = glu(r["glu"])
    zero_halo = jnp.zeros((CONV_HALO, D_CONV), F32)
    if local:
        ext_ref[0:CONV_HALO, :] = jnp.where(i > 0, glu(r["glu_prev"]), zero_halo)
        ext_ref[CONV_HALO + tq:, :] = jnp.where(i < nt - 1, glu(r["glu_next"]), zero_halo)
    else:
        ext_ref[0:CONV_HALO, :] = zero_halo
        ext_ref[CONV_HALO + tq:, :] = zero_halo

    for rc in range(tq // CONV_ROWS):
        t0 = rc * CONV_ROWS
        rows = slice(t0, t0 + CONV_ROWS)
        pieces = []
        for cc in range(D_CONV // CONV_COLS):
            cols = slice(cc * CONV_COLS, (cc + 1) * CONV_COLS)
            acc = jnp.broadcast_to(r["dw_b"][:, cols], (CONV_ROWS, CONV_COLS))
            for b in range(SUBLANES):
                z = None
                for a in range(-(-(CONV_WIDTH + 1) // SUBLANES)):
                    o = SUBLANES * a + b
                    if o < 1 or o > CONV_WIDTH:
                        continue
                    xs = ext_ref[t0 + SUBLANES * a:t0 + SUBLANES * a + CONV_ROWS + SUBLANES, cols]
                    term = (xs.reshape(-1, SUBLANES, CONV_COLS) * r["dw_w"][o - 1, :, cols][None]
                            ).reshape(CONV_ROWS + SUBLANES, CONV_COLS)
                    z = term if z is None else z + term
                acc = acc + z[b:b + CONV_ROWS, :]
            pieces.append(acc)
        acc = jnp.concatenate(pieces, axis=1)
        mu = jnp.mean(acc, axis=-1, keepdims=True)
        cen = acc - mu
        var = jnp.mean(cen * cen, axis=-1, keepdims=True)
        y = cen * lax.rsqrt(var + EPS) * r["ln_g"][...] + r["ln_b"][...]
        y = _silu(y).astype(BF16)
        y = jnp.dot(y, r["pw_w"][...], preferred_element_type=F32) + r["pw_b"][...]
        y = y * r["gates"][rows, :D_CONV].astype(F32)
        ycat_ref[rows, 0:D_CONV] = y.astype(BF16)

    ylru = r["ylru"][...].astype(F32)
    if "ylru_rev" in r:
        ylru = ylru + r["ylru_rev"][...].astype(F32)
    ycat_ref[:, D_CONV:D_CONV + D_LRU] = (
        ylru * r["gates"][:, D_CONV:].astype(F32)).astype(BF16)

    qrows = GQA_GROUP * WINDOW
    qi = lax.broadcasted_iota(jnp.int32, (qrows, WINDOW), 0) % WINDOW
    kj = lax.broadcasted_iota(jnp.int32, (qrows, WINDOW), 1)
    neg_inf = jnp.float32(-jnp.inf)
    dn = (((1,), (1,)), ((), ()))
    n_qb = tq // WINDOW
    lc = r["kvc"].shape[0]
    for qb in range(n_qb):
        rows = slice(qb * WINDOW, (qb + 1) * WINDOW)
        for g in range(N_KV_HEADS):
            k_cols = slice(g * HEAD_DIM, (g + 1) * HEAD_DIM)
            v_cols = slice(D_KV + g * HEAD_DIM, D_KV + (g + 1) * HEAD_DIM)
            q_st = r["q"][qb, g * GQA_GROUP:(g + 1) * GQA_GROUP].reshape(qrows, HEAD_DIM)
            sink = jnp.concatenate(
                [jnp.broadcast_to(r["sink"][g * GQA_GROUP + hh:g * GQA_GROUP + hh + 1, 0:1],
                                  (WINDOW, 1)) for hh in range(GQA_GROUP)], axis=0) * LOG2E
            blocks = []
            for cb in range(lc // WINDOW):
                crow = slice(cb * WINDOW, (cb + 1) * WINDOW)
                s = lax.dot_general(q_st, r["kvc"][crow, k_cols], dn, preferred_element_type=F32)
                blocks.append((s, r["kvc"][crow, v_cols]))
            if local:
                if qb == 0:
                    k_p, v_p = r["kv_prev"][:, k_cols], r["kv_prev"][:, v_cols]
                    ok_p = i > 0
                else:
                    prows = slice((qb - 1) * WINDOW, qb * WINDOW)
                    k_p, v_p = r["kv"][prows, k_cols], r["kv"][prows, v_cols]
                    ok_p = True
                if qb == n_qb - 1:
                    k_n, v_n = r["kv_next"][:, k_cols], r["kv_next"][:, v_cols]
                    ok_n = i < nt - 1
                else:
                    nrows = slice((qb + 1) * WINDOW, (qb + 2) * WINDOW)
                    k_n, v_n = r["kv"][nrows, k_cols], r["kv"][nrows, v_cols]
                    ok_n = True
                s_p = lax.dot_general(q_st, k_p, dn, preferred_element_type=F32)
                s_c = lax.dot_general(q_st, r["kv"][rows, k_cols], dn, preferred_element_type=F32)
                s_n = lax.dot_general(q_st, k_n, dn, preferred_element_type=F32)
                s_p = jnp.where((kj >= qi) & ok_p, s_p, neg_inf)
                s_n = jnp.where((kj <= qi) & ok_n, s_n, neg_inf)
                blocks += [(s_p, v_p), (s_c, r["kv"][rows, v_cols]), (s_n, v_n)]
            m_el = blocks[0][0]
            for s, _ in blocks[1:]:
                m_el = jnp.maximum(m_el, s)
            m = jnp.maximum(jnp.max(m_el, axis=-1, keepdims=True), sink)
            den_el = None
            o = None
            for s, v in blocks:
                p = jnp.exp2(s - m)
                den_el = p if den_el is None else den_el + p
                pv = jnp.dot(p.astype(BF16), v, preferred_element_type=F32)
                o = pv if o is None else o + pv
            den = jnp.sum(den_el, axis=-1, keepdims=True) + jnp.exp2(sink - m)
            o = o / den
            for hh in range(GQA_GROUP):
                c0 = (g * GQA_GROUP + hh) * HEAD_DIM
                gate = r["attn_gate"][rows, c0:c0 + HEAD_DIM].astype(F32)
                ycat_ref[rows, D_CONV + D_LRU + c0:D_CONV + D_LRU + c0 + HEAD_DIM] = (
                    o[hh * WINDOW:(hh + 1) * WINDOW, :] * gate).astype(BF16)

    out_ref = r["out"]
    for c0 in range(0, out_ref.shape[1], OUT_CHUNK_N):
        cols = slice(c0, c0 + OUT_CHUNK_N)
        y = jnp.dot(ycat_ref[...], r["w_out"][:, cols], preferred_element_type=F32)
        out_ref[:, cols] = r["x"][:, cols] + r["gate"][:, cols] * y
    if final:
        xn = out_ref[...]
        ms = jnp.mean(xn * xn, axis=-1, keepdims=True)
        out_ref[...] = xn * lax.rsqrt(ms + EPS) * r["final_g"][...]


def _mix_call(x2, u2, q4, uc2, ylru, ada4, layer, ada_row_fn, wts, *, tq, rows_per_batch,
              ctx_rows, local, final):
    m, d = x2.shape
    nt = rows_per_batch // tq
    nb = m // rows_per_batch
    names, specs, args = [], [], []

    def add(name, arr, spec):
        names.append(name)
        specs.append(spec)
        args.append(arr)

    def rowblk(width, col_off):
        return pl.BlockSpec((tq, width), lambda b, i: (b * nt + i, col_off // width))

    def halo(rows, width, col_off, nxt):
        per = tq // rows
        last = m // rows - 1
        if nxt:
            fn = lambda b, i: (jnp.minimum((b * nt + i + 1) * per, last), col_off // width)
        else:
            fn = lambda b, i: (jnp.maximum((b * nt + i) * per - 1, 0), col_off // width)
        return pl.BlockSpec((rows, width), fn)

    full = lambda shape: pl.BlockSpec(shape, lambda b, i: (0,) * len(shape),
                                      pipeline_mode=pl.Buffered(1))

    add("x", x2, pl.BlockSpec((tq, d), lambda b, i: (b * nt + i, 0)))
    add("gate", ada4, pl.BlockSpec((None, None, 1, d), lambda b, i: (layer, ada_row_fn(b), 0, 2)))
    add("glu", u2, rowblk(D_CONV, OFF_CONV_GLU))
    if local:
        add("glu_prev", u2, halo(CONV_HALO, D_CONV, OFF_CONV_GLU, False))
        add("glu_next", u2, halo(CONV_HALO, D_CONV, OFF_CONV_GLU, True))
    add("gates", u2, rowblk(D_CONV + D_LRU, OFF_CONV_GATE))
    ylru_spec = pl.BlockSpec((tq, D_LRU), lambda b, i: (b * nt + i, 0))
    if isinstance(ylru, tuple):
        add("ylru", ylru[0], ylru_spec)
        add("ylru_rev", ylru[1], ylru_spec)
    else:
        add("ylru", ylru, ylru_spec)
    add("q", q4, pl.BlockSpec((tq // WINDOW, N_Q_HEADS, WINDOW, HEAD_DIM),
                              lambda b, i: (b * nt + i, 0, 0, 0)))
    add("attn_gate", u2, rowblk(D_ATTN, U_ATTN_GATE))
    if local:
        add("kv", u2, rowblk(2 * D_KV, OFF_K))
        add("kv_prev", u2, halo(WINDOW, 2 * D_KV, OFF_K, False))
        add("kv_next", u2, halo(WINDOW, 2 * D_KV, OFF_K, True))
    add("kvc", uc2, pl.BlockSpec((ctx_rows, 2 * D_KV), lambda b, i: (b, OFF_K // (2 * D_KV))))
    for name in ("dw_w", "dw_b", "ln_g", "ln_b", "pw_w", "pw_b", "sink", "w_out"):
        shape = wts[name].shape[1:]
        add(name, wts[name], pl.BlockSpec((None,) + shape,
                                          lambda b, i, nd=len(shape): (layer,) + (0,) * nd,
                                          pipeline_mode=pl.Buffered(1)))
    if final:
        add("final_g", wts["final_g"], full(wts["final_g"].shape))
    names += ["out", "ext", "ycat"]
    return pl.pallas_call(
        functools.partial(_mix_kernel, names=tuple(names), tq=tq, local=local, final=final),
        grid=(nb, nt),
        in_specs=specs,
        out_specs=pl.BlockSpec((tq, d), lambda b, i: (b * nt + i, 0)),
        out_shape=jax.ShapeDtypeStruct((m, d), F32),
        scratch_shapes=[pltpu.VMEM((tq + 2 * CONV_HALO, D_CONV), F32),
                        pltpu.VMEM((tq, D_MODEL), BF16)],
        compiler_params=pltpu.CompilerParams(
            dimension_semantics=("arbitrary", "arbitrary"), vmem_limit_bytes=VMEM_LIMIT),
        name="mix_lat" if local else "mix_ctx",
    )(*args)


def _rope_tables(seq):
    rows = seq // GRID_W
    row = np.repeat(np.arange(rows, dtype=np.float64), GRID_W)
    col = np.tile(np.arange(GRID_W, dtype=np.float64), rows)
    half = HEAD_DIM // 2
    inv = ROPE_BASE ** (-np.arange(0, half, 2, dtype=np.float64) / half)
    ang_r = row[:, None] * inv[None, :]
    ang_c = col[:, None] * inv[None, :]
    ang = np.concatenate([ang_r, ang_r, ang_c, ang_c], axis=-1)
    cos, sin = np.cos(ang), np.sin(ang)
    first = (np.arange(HEAD_DIM) % half) < (half // 2)
    return tuple(jnp.asarray(t, F32)
                 for t in (cos, np.where(first, -sin, 0.0), np.where(first, 0.0, sin)))


def _block_diag(w):
    nblk, blk = w.shape[-3], w.shape[-2]
    n = nblk * blk
    rows = w.reshape(w.shape[:-3] + (n, blk))
    spread = jnp.asarray(np.arange(n)[None, :] % blk == np.arange(blk)[:, None], w.dtype)
    mask = jnp.asarray(np.arange(n)[:, None] // blk == np.arange(n)[None, :] // blk, w.dtype)
    return jnp.matmul(rows, spread, precision=lax.Precision.HIGHEST) * mask


def _pick_tile(n, pref):
    t = min(n, pref)
    while n % t:
        t //= 2
    return t


def kernel(x, c, ctx, c_ctx, norm_g, w_ada, b_ada, w_in, dw_w, dw_b, ln_g, ln_b, pw_w, pw_b,
           lru_conv_w, lru_conv_b, lru_w_r, lru_b_r, lru_w_i, lru_b_i, lru_lam, attn_sink,
           w_out, final_g):
    nb, seq, d = x.shape
    lc = ctx.shape[1]
    depth = w_in.shape[0]
    ctx_row = nb

    c_rows = jnp.concatenate([c, c_ctx[None, :], jnp.zeros((ADA_ROWS - nb - 1, d), F32)], axis=0)
    ada = _ada_call(c_rows, w_ada, b_ada)
    ada4 = ada.reshape(depth, ADA_ROWS, 1, 3 * d)
    rope_tabs = _rope_tables(seq)
    w_bf = w_in.astype(BF16)
    norm_g3 = norm_g.reshape(depth, 1, d)
    vec4 = lambda p: p.reshape(depth, 2, 1, D_LRU)
    wg_half = (0.5 * jnp.concatenate([_block_diag(lru_w_r), _block_diag(lru_w_i)], axis=-1)
               ).astype(BF16)
    lru_params = (lru_conv_w, vec4(lru_conv_b), wg_half, vec4(lru_b_r), vec4(lru_b_i),
                  vec4(lru_lam))
    vec3 = lambda p: p.reshape(depth, 1, D_CONV)
    wts = {
        "dw_w": jnp.broadcast_to(dw_w[:, :, None, :], (depth, CONV_WIDTH, SUBLANES, D_CONV)),
        "dw_b": vec3(dw_b), "ln_g": vec3(ln_g), "ln_b": vec3(ln_b),
        "pw_w": pw_w.astype(BF16), "pw_b": vec3(pw_b),
        "sink": jnp.broadcast_to(attn_sink[:, :, None], (depth, N_Q_HEADS, HEAD_DIM)),
        "w_out": w_out.astype(BF16), "final_g": final_g.reshape(1, d),
    }

    x2 = x.reshape(nb * seq, d)
    xc2 = ctx.reshape(nb * lc, d)
    tm = _pick_tile(seq, 1024)
    tmc = _pick_tile(nb * lc, 1024)
    tq = _pick_tile(seq, 512)
    tt = _pick_tile(seq, 2048)
    tu = _pick_tile(tt, 512)
    zeros_h0 = jnp.zeros((nb, SUBLANES, D_LRU), F32)

    for l in range(depth):
        last = l == depth - 1
        u2, q4 = _inproj_call(x2, norm_g3, ada4, l, w_bf, rope_tabs, tm=tm, rows_per_batch=seq,
                              ada_row_fn=lambda b: b, with_q=True)
        uc2, qc4 = _inproj_call(xc2, norm_g3, ada4, l, w_bf, None, tm=tmc, rows_per_batch=tmc,
                                ada_row_fn=lambda b: ctx_row, with_q=not last)

        ctx_kw = dict(rows_per_batch=lc, tt=lc, units=1)
        lat_kw = dict(rows_per_batch=seq, tt=tt, units=tt // tu)
        hp_c, xp_c, st_f = _lru_call(uc2, zeros_h0, None, lru_params, l, 0, **ctx_kw)
        ylru_c, st_r = _lru_call(None, zeros_h0, (xp_c, hp_c), lru_params, l, 1, **ctx_kw)
        hp, xp, _ = _lru_call(u2, st_f, None, lru_params, l, 0, **lat_kw)
        ylru, _ = _lru_call(None, st_r, (xp, hp), lru_params, l, 1, **lat_kw)

        x2_new = _mix_call(x2, u2, q4, uc2, ylru, ada4, l, lambda b: b, wts, tq=tq,
                           rows_per_batch=seq, ctx_rows=lc, local=True, final=last)
        if not last:
            xc2 = _mix_call(xc2, uc2, qc4, uc2, ylru_c, ada4, l, lambda b: ctx_row, wts, tq=lc,
                            rows_per_batch=lc, ctx_rows=lc, local=False, final=False)
        x2 = x2_new
    return x2.reshape(nb, seq, d)
```

```python
import functools
import math

import jax
import jax.numpy as jnp
import numpy as np
from jax import lax
from jax.experimental import pallas as pl
from jax.experimental.pallas import tpu as pltpu

F32 = jnp.float32
BF16 = jnp.bfloat16

D_MODEL = 2048
D_CONV = 512
D_LRU = 512
HEAD_DIM = 128
N_Q_HEADS = 8
N_KV_HEADS = 2
GQA_GROUP = N_Q_HEADS // N_KV_HEADS
D_ATTN = N_Q_HEADS * HEAD_DIM
D_KV = N_KV_HEADS * HEAD_DIM
GRID_W = 64
CONV_WIDTH = 31
CONV_PAD = (CONV_WIDTH - 1) // 2
LRU_CONV_WIDTH = 4
LRU_BLOCKS = 8
LRU_C = 8.0
WINDOW = 128
ROPE_BASE = 10000.0
EPS = 1e-6

OFF_K = D_LRU
OFF_V = OFF_K + D_KV
OFF_MEM_END = OFF_V + D_KV
OFF_CONV_GLU = OFF_MEM_END
OFF_CONV_GATE = OFF_CONV_GLU + 2 * D_CONV
OFF_LRU_GATE = OFF_CONV_GATE + D_CONV
OFF_Q = OFF_LRU_GATE + D_LRU
OFF_ATTN_GATE = OFF_Q + D_ATTN
D_IN = OFF_ATTN_GATE + D_ATTN
U_ATTN_GATE = OFF_Q
D_U = D_IN - D_ATTN

ADA_ROWS = 8
IN_TILE_N = 1024
IN_CHUNK_N = 256
Q_TILE = OFF_Q // IN_TILE_N
GLU_TILE = OFF_CONV_GLU // IN_TILE_N
OUT_CHUNK_N = 256
CONV_HALO = 16
CONV_ROWS = 64
CONV_COLS = 256
SUBLANES = 8
LANES = 128
VMEM_LIMIT = 56 * 1024 * 1024
F32_TINY = float(np.finfo(np.float32).tiny)
LOG2E = math.log2(math.e)
Q_SCALE = HEAD_DIM ** -0.5 * LOG2E

assert OFF_Q % IN_TILE_N == 0 and D_ATTN == IN_TILE_N and OFF_MEM_END == IN_TILE_N
assert OFF_CONV_GLU % IN_TILE_N == 0 and 2 * D_CONV == IN_TILE_N


def _sigmoid(x):
    return 0.5 * jnp.tanh(0.5 * x) + 0.5


def _silu(x):
    hx = 0.5 * x
    return hx * jnp.tanh(hx) + hx


def _ada_kernel(c_ref, w_ref, b_ref, o_ref):
    ca = _silu(c_ref[...])
    o_ref[...] = jnp.dot(ca.astype(BF16), w_ref[...].astype(BF16),
                         preferred_element_type=F32) + b_ref[...]


def _ada_call(c_rows, w_ada, b_ada):
    depth, d, n = w_ada.shape
    tn = 1024
    return pl.pallas_call(
        _ada_kernel,
        grid=(depth, n // tn),
        in_specs=[
            pl.BlockSpec((ADA_ROWS, d), lambda l, j: (0, 0)),
            pl.BlockSpec((None, d, tn), lambda l, j: (l, 0, j)),
            pl.BlockSpec((None, 1, tn), lambda l, j: (l, 0, j)),
        ],
        out_specs=pl.BlockSpec((None, ADA_ROWS, tn), lambda l, j: (l, 0, j)),
        out_shape=jax.ShapeDtypeStruct((depth, ADA_ROWS, n), F32),
        compiler_params=pltpu.CompilerParams(
            dimension_semantics=("arbitrary", "arbitrary"), vmem_limit_bytes=VMEM_LIMIT),
        name="ada_proj",
    )(c_rows, w_ada, b_ada.reshape(depth, 1, n))


def _rope(t, cos, sin_a, sin_b):
    return (t * cos + pltpu.roll(t, HEAD_DIM - 32, 1) * sin_a + pltpu.roll(t, 32, 1) * sin_b)


def _inproj_kernel(*refs, rope, with_q, tm):
    refs = list(refs)
    x_ref, g_ref, shift_ref, scale_ref, w_ref = refs[:5]
    del refs[:5]
    if rope:
        cos_ref, sa_ref, sb_ref = refs[:3]
        del refs[:3]
    o_ref = refs.pop(0)
    q_ref = refs.pop(0) if with_q else None
    h_ref = refs.pop(0)
    j = pl.program_id(1)

    @pl.when(j == 0)
    def _():
        x = x_ref[...]
        ms = jnp.mean(x * x, axis=-1, keepdims=True)
        gain = g_ref[...] * (1.0 + scale_ref[...])
        h_ref[...] = (x * lax.rsqrt(ms + EPS) * gain + shift_ref[...]).astype(BF16)

    chunks = [slice(c0, c0 + IN_CHUNK_N) for c0 in range(0, IN_TILE_N, IN_CHUNK_N)]
    heads_per_chunk = IN_CHUNK_N // HEAD_DIM

    def proj(cols):
        return jnp.dot(h_ref[...], w_ref[:, cols], preferred_element_type=F32)

    def rope_heads(u):
        if not rope:
            return [u[:, hh * HEAD_DIM:(hh + 1) * HEAD_DIM] for hh in range(heads_per_chunk)]
        return [_rope(u[:, hh * HEAD_DIM:(hh + 1) * HEAD_DIM], cos_ref[...], sa_ref[...],
                      sb_ref[...]) for hh in range(heads_per_chunk)]

    def plain_tile():
        for cols in chunks:
            o_ref[:, cols] = proj(cols).astype(BF16)

    def q_tile():
        for ci, cols in enumerate(chunks):
            for hh, t in enumerate(rope_heads(proj(cols))):
                t = (t * Q_SCALE).astype(BF16)
                for qb in range(tm // WINDOW):
                    q_ref[qb, ci * heads_per_chunk + hh] = t[qb * WINDOW:(qb + 1) * WINDOW, :]

    def kv_tile():
        for cols in chunks:
            u = proj(cols)
            if OFF_K <= cols.start < OFF_V:
                u = jnp.concatenate(rope_heads(u), axis=1)
            o_ref[:, cols] = u.astype(BF16)

    def glu_tile():
        half = len(chunks) // 2
        for ci in range(half):
            o_ref[:, chunks[ci]] = (proj(chunks[ci]) * _sigmoid(proj(chunks[ci + half]))
                                    ).astype(BF16)
            o_ref[:, chunks[ci + half]] = jnp.zeros((tm, IN_CHUNK_N), BF16)

    def silu_tile():
        for cols in chunks:
            o_ref[:, cols] = _silu(proj(cols)).astype(BF16)

    if not with_q:
        plain_tile()
    else:
        pl.when(j == 0)(kv_tile if rope else plain_tile)
        pl.when(j == GLU_TILE)(glu_tile)
        pl.when(j == Q_TILE)(q_tile)
        pl.when((j != 0) & (j != GLU_TILE) & (j != Q_TILE))(silu_tile)


def _inproj_call(x2, norm_g, ada4, layer, w_bf, rope_tabs, *, tm, rows_per_batch, ada_row_fn,
                 with_q):
    m, d = x2.shape
    tiles_per_batch = rows_per_batch // tm
    rope = rope_tabs is not None
    n_tiles = D_IN // IN_TILE_N if with_q else 1

    def ada_spec(part):
        return pl.BlockSpec((None, None, 1, d),
                            lambda i, j: (layer, ada_row_fn(i // tiles_per_batch), 0, part))

    in_specs = [
        pl.BlockSpec((tm, d), lambda i, j: (i, 0)),
        pl.BlockSpec((None, 1, d), lambda i, j: (layer, 0, 0)),
        ada_spec(0), ada_spec(1),
        pl.BlockSpec((None, d, IN_TILE_N), lambda i, j: (layer, 0, j)),
    ]
    args = [x2, norm_g, ada4, ada4, w_bf]
    if rope:
        tab_spec = pl.BlockSpec((tm, HEAD_DIM), lambda i, j: (i % tiles_per_batch, 0))
        in_specs += [tab_spec] * 3
        args += list(rope_tabs)
    out_specs = [pl.BlockSpec((tm, IN_TILE_N), lambda i, j: (i, jnp.where(j > Q_TILE, j - 1, j)))]
    out_shape = [jax.ShapeDtypeStruct((m, D_U if with_q else IN_TILE_N), BF16)]
    if with_q:
        out_specs.append(pl.BlockSpec((tm // WINDOW, N_Q_HEADS, WINDOW, HEAD_DIM),
                                      lambda i, j: (i, 0, 0, 0)))
        out_shape.append(jax.ShapeDtypeStruct((m // WINDOW, N_Q_HEADS, WINDOW, HEAD_DIM), BF16))
    outs = pl.pallas_call(
        functools.partial(_inproj_kernel, rope=rope, with_q=with_q, tm=tm),
        grid=(m // tm, n_tiles),
        in_specs=in_specs,
        out_specs=out_specs,
        out_shape=out_shape,
        scratch_shapes=[pltpu.VMEM((tm, d), BF16)],
        compiler_params=pltpu.CompilerParams(
            dimension_semantics=("arbitrary", "arbitrary"), vmem_limit_bytes=VMEM_LIMIT),
        name="in_proj",
    )(*args)
    return (outs[0], outs[1]) if with_q else (outs[0], None)


def _lru_kernel(*refs, reverse, tt, units):
    if reverse:
        (x_ref, h0_ref, acc_ref, pmat_ref, cw_ref, cb_ref, wg_ref, br_ref, bi_ref, lam_ref,
         o_ref, state_ref, ext3_ref, a3_ref, b3_ref, edge_ref, h_ref) = refs
        xp_ref = None
    else:
        (x_ref, h0_ref, pmat_ref, cw_ref, cb_ref, wg_ref, br_ref, bi_ref, lam_ref,
         o_ref, xp_ref, state_ref, ext3_ref, a3_ref, b3_ref, edge_ref, h_ref) = refs
        acc_ref = None
    c = pl.program_id(1)
    tu = tt // units
    seg = tu // SUBLANES
    hist = LRU_CONV_WIDTH - 1
    row = lax.broadcasted_iota(jnp.int32, (SUBLANES, D_LRU), 0)

    def blk(g):
        return slice(g * SUBLANES, (g + 1) * SUBLANES)

    @pl.when(c == 0)
    def _():
        edge_ref[...] = jnp.zeros((hist * SUBLANES, D_LRU), F32)
        h_ref[...] = h0_ref[...]

    lam = lam_ref[...]
    softplus_neg_lam = jnp.maximum(-lam, 0.0) + jnp.log1p(jnp.exp(-jnp.abs(lam)))
    half_k = (-0.5 * LRU_C) * softplus_neg_lam
    half_br = 0.5 * br_ref[...]
    half_bi = 0.5 * bi_ref[...]

    order = list(range(units - 1, -1, -1) if reverse else range(units))
    for u in order:
        _lru_gates(slice(u * tu, (u + 1) * tu), x_ref, xp_ref, pmat_ref, cw_ref, cb_ref, wg_ref,
                   half_br, half_bi, half_k, ext3_ref.at[u], a3_ref.at[u], b3_ref.at[u], edge_ref,
                   reverse=reverse, tt=tu, seg=seg, hist=hist, row=row, blk=blk)

    def step(t, carry):
        g_ = (seg - 1 - t) if reverse else t
        rows = pl.ds(pl.multiple_of(g_ * SUBLANES, SUBLANES), SUBLANES)
        new = []
        for u, (h, p) in zip(order, carry):
            a = a3_ref[u, rows, :]
            h = a * h + b3_ref[u, rows, :]
            p = a * p
            b3_ref[u, rows, :] = h
            a3_ref[u, rows, :] = p
            new.append((h, p))
        return tuple(new)

    init = tuple((jnp.zeros((SUBLANES, D_LRU), F32), jnp.ones((SUBLANES, D_LRU), F32))
                 for _ in order)
    finals = lax.fori_loop(0, seg, step, init, unroll=8)

    for u, (h_fin, p_fin) in zip(order, finals):
        _lru_finish(slice(u * tu, (u + 1) * tu), h_fin, p_fin, acc_ref, pmat_ref, o_ref,
                    a3_ref.at[u], b3_ref.at[u], h_ref, reverse=reverse, tt=tu, seg=seg, row=row)
    state_ref[...] = h_ref[...]


def _lru_gates(rows_u, x_ref, xp_ref, pmat_ref, cw_ref, cb_ref, wg_ref, half_br, half_bi, half_k,
               ext_ref, a_ref, b_ref, edge_ref, *, reverse, tt, seg, hist, row, blk):
    base = 0 if reverse else hist
    if reverse:
        xp = x_ref[rows_u, :].astype(F32)
    else:
        xp = jnp.dot(pmat_ref[...], x_ref[rows_u, :], preferred_element_type=F32)
        xp_ref[rows_u, :] = xp.astype(BF16)
    ext_ref[base * SUBLANES:base * SUBLANES + tt, :] = xp
    for j in range(1, hist + 1):
        if reverse:
            rolled = pltpu.roll(ext_ref[blk(j - 1), :], SUBLANES - 1, 0)
            ext_ref[blk(seg + j - 1), :] = jnp.where(row == SUBLANES - 1, edge_ref[blk(j - 1), :],
                                                     rolled)
        else:
            rolled = pltpu.roll(ext_ref[blk(hist + seg - j), :], 1, 0)
            ext_ref[blk(hist - j), :] = jnp.where(row == 0, edge_ref[blk(j - 1), :], rolled)
        edge_ref[blk(j - 1), :] = rolled

    xc = jnp.broadcast_to(cb_ref[...], (tt, D_LRU))
    for k in range(LRU_CONV_WIDTH):
        off = (hist - k) if reverse else k
        xc = xc + cw_ref[k:k + 1, :] * ext_ref[off * SUBLANES:off * SUBLANES + tt, :]

    xcb = xc.astype(BF16)
    g_r, g_i = [], []
    for j in range(D_LRU // LANES):
        gj = jnp.dot(xcb[:, j * LANES:(j + 1) * LANES], wg_ref[j], preferred_element_type=F32)
        g_r.append(gj[:, :LANES])
        g_i.append(gj[:, LANES:])
    t_r = jnp.tanh(jnp.concatenate(g_r, axis=1) + half_br)
    t_i = jnp.tanh(jnp.concatenate(g_i, axis=1) + half_bi)
    log_a = half_k * t_r + half_k
    a_ref[...] = jnp.exp(log_a)
    th = jnp.tanh(log_a)
    one_minus_a2 = (-2.0 * th) / (1.0 - th)
    root = one_minus_a2 * lax.rsqrt(jnp.maximum(one_minus_a2, F32_TINY))
    half_xc = 0.5 * xc
    b_ref[...] = root * (half_xc * t_i + half_xc)


def _lru_finish(rows_u, h_fin, p_fin, acc_ref, pmat_ref, o_ref, a_ref, b_ref, h_ref, *, reverse,
                tt, seg, row):
    a, b = p_fin, h_fin
    for d in (1, 2, 4):
        keep = (row < SUBLANES - d) if reverse else (row >= d)
        shift = (SUBLANES - d) if reverse else d
        a_s = jnp.where(keep, pltpu.roll(a, shift, 0), 1.0)
        b_s = jnp.where(keep, pltpu.roll(b, shift, 0), 0.0)
        b = a * b_s + b
        a = a * a_s
    h_in = h_ref[...]
    end = a * h_in + b
    if reverse:
        carry_in = jnp.where(row < SUBLANES - 1, pltpu.roll(end, SUBLANES - 1, 0), h_in)
        h_ref[...] = jnp.broadcast_to(end[0:1, :], (SUBLANES, D_LRU))
    else:
        carry_in = jnp.where(row >= 1, pltpu.roll(end, 1, 0), h_in)
        h_ref[...] = jnp.broadcast_to(end[SUBLANES - 1:SUBLANES, :], (SUBLANES, D_LRU))

    h_all = (b_ref[...].reshape(seg, SUBLANES, D_LRU)
             + a_ref[...].reshape(seg, SUBLANES, D_LRU) * carry_in[None]).reshape(tt, D_LRU)
    if reverse:
        h_sum = (h_all + acc_ref[rows_u, :].astype(F32)).astype(BF16)
        o_ref[rows_u, :] = jnp.dot(pmat_ref[...], h_sum, preferred_element_type=F32
                                   ).astype(o_ref.dtype)
    else:
        o_ref[rows_u, :] = h_all.astype(o_ref.dtype)


def _lru_call(x_rows, h0, fwd, params, layer, direction, *, rows_per_batch, tt, units):
    cw, cb, wg_half, br, bi, lam = params
    reverse = direction == 1
    m = x_rows.shape[0] if not reverse else fwd[0].shape[0]
    nb = m // rows_per_batch
    nc = rows_per_batch // tt
    tu = tt // units

    def chunk(c):
        return (nc - 1 - c) if reverse else c

    row_spec = pl.BlockSpec((tt, D_LRU), lambda b, c: (b * nc + chunk(c), 0))
    full = lambda shape: pl.BlockSpec(shape, lambda b, c: (0,) * len(shape))
    h0_spec = pl.BlockSpec((None, SUBLANES, D_LRU), lambda b, c: (b, 0, 0))
    seg = tu // SUBLANES
    src = (np.arange(tu) % SUBLANES) * seg + np.arange(tu) // SUBLANES
    perm = src[:, None] == np.arange(tu)[None, :]
    if reverse:
        in_specs = [row_spec, h0_spec, row_spec]
        args = [fwd[0], h0, fwd[1], jnp.asarray(perm.T, BF16)]
    else:
        in_specs = [row_spec, h0_spec]
        args = [x_rows, h0, jnp.asarray(perm, BF16)]
    sel = lambda shape: pl.BlockSpec((None, None) + shape,
                                     lambda b, c: (layer, direction) + (0,) * len(shape))
    in_specs += [full((tu, tu)),
                 sel((LRU_CONV_WIDTH, D_LRU)), sel((1, D_LRU)), sel(wg_half.shape[2:]),
                 sel((1, D_LRU)), sel((1, D_LRU)), sel((1, D_LRU))]
    args += [cw, cb, wg_half, br, bi, lam]
    edge_rows = (LRU_CONV_WIDTH - 1) * SUBLANES
    rows_out = jax.ShapeDtypeStruct((m, D_LRU), BF16)
    state_spec = pl.BlockSpec((None, SUBLANES, D_LRU), lambda b, c: (b, 0, 0))
    state_out = jax.ShapeDtypeStruct((nb, SUBLANES, D_LRU), F32)
    return pl.pallas_call(
        functools.partial(_lru_kernel, reverse=reverse, tt=tt, units=units),
        grid=(nb, nc),
        in_specs=in_specs,
        out_specs=[row_spec, state_spec] if reverse else [row_spec, row_spec, state_spec],
        out_shape=[rows_out, state_out] if reverse else [rows_out, rows_out, state_out],
        scratch_shapes=[pltpu.VMEM((units, tu + edge_rows, D_LRU), F32),
                        pltpu.VMEM((units, tu, D_LRU), F32),
                        pltpu.VMEM((units, tu, D_LRU), F32),
                        pltpu.VMEM((edge_rows, D_LRU), F32),
                        pltpu.VMEM((SUBLANES, D_LRU), F32)],
        compiler_params=pltpu.CompilerParams(
            dimension_semantics=("arbitrary", "arbitrary"), vmem_limit_bytes=VMEM_LIMIT),
        name="rglru_rev" if reverse else "rglru_fwd",
    )(*args)


def _mix_kernel(*refs, names, tq, local, final):
    r = dict(zip(names, refs))
    i = pl.program_id(1)
    nt = pl.num_programs(1)
    ext_ref, ycat_ref = r["ext"], r["ycat"]

    def glu(ref):
        return ref[...].astype(F32)

    ext_ref[CONV_HALO:CONV_HALO + tq, :] = glu(r["glu"])
    zero_halo = jnp.zeros((CONV_HALO, D_CONV), F32)
    if local:
        ext_ref[0:CONV_HALO, :] = jnp.where(i > 0, glu(r["glu_prev"]), zero_halo)
        ext_ref[CONV_HALO + tq:, :] = jnp.where(i < nt - 1, glu(r["glu_next"]), zero_halo)
    else:
        ext_ref[0:CONV_HALO, :] = zero_halo
        ext_ref[CONV_HALO + tq:, :] = zero_halo

    for rc in range(tq // CONV_ROWS):
        t0 = rc * CONV_ROWS
        rows = slice(t0, t0 + CONV_ROWS)
        pieces = []
        for cc in range(D_CONV // CONV_COLS):
            cols = slice(cc * CONV_COLS, (cc + 1) * CONV_COLS)
            acc = jnp.broadcast_to(r["dw_b"][:, cols], (CONV_ROWS, CONV_COLS))
            for b in range(SUBLANES):
                z = None
                for a in range(-(-(CONV_WIDTH + 1) // SUBLANES)):
                    o = SUBLANES * a + b
                    if o < 1 or o > CONV_WIDTH:
                        continue
                    xs = ext_ref[t0 + SUBLANES * a:t0 + SUBLANES * a + CONV_ROWS + SUBLANES, cols]
                    term = (xs.reshape(-1, SUBLANES, CONV_COLS) * r["dw_w"][o - 1, :, cols][None]
                            ).reshape(CONV_ROWS + SUBLANES, CONV_COLS)
                    z = term if z is None else z + term
                acc = acc + z[b:b + CONV_ROWS, :]
            pieces.append(acc)
        acc = jnp.concatenate(pieces, axis=1)
        mu = jnp.mean(acc, axis=-1, keepdims=True)
        cen = acc - mu
        var = jnp.mean(cen * cen, axis=-1, keepdims=True)
        y = cen * lax.rsqrt(var + EPS) * r["ln_g"][...] + r["ln_b"][...]
        y = _silu(y).astype(BF16)
        y = jnp.dot(y, r["pw_w"][...], preferred_element_type=F32) + r["pw_b"][...]
        y = y * r["gates"][rows, :D_CONV].astype(F32)
        ycat_ref[rows, 0:D_CONV] = y.astype(BF16)

    ylru = r["ylru"][...].astype(F32)
    if "ylru_rev" in r:
        ylru = ylru + r["ylru_rev"][...].astype(F32)
    ycat_ref[:, D_CONV:D_CONV + D_LRU] = (
        ylru * r["gates"][:, D_CONV:].astype(F32)).astype(BF16)

    qrows = GQA_GROUP * WINDOW
    qi = lax.broadcasted_iota(jnp.int32, (qrows, WINDOW), 0) % WINDOW
    kj = lax.broadcasted_iota(jnp.int32, (qrows, WINDOW), 1)
    neg_inf = jnp.float32(-jnp.inf)
    dn = (((1,), (1,)), ((), ()))
    n_qb = tq // WINDOW
    lc = r["kvc"].shape[0]
    for qb in range(n_qb):
        rows = slice(qb * WINDOW, (qb + 1) * WINDOW)
        for g in range(N_KV_HEADS):
            k_cols = slice(g * HEAD_DIM, (g + 1) * HEAD_DIM)
            v_cols = slice(D_KV + g * HEAD_DIM, D_KV + (g + 1) * HEAD_DIM)
            q_st = r["q"][qb, g * GQA_GROUP:(g + 1) * GQA_GROUP].reshape(qrows, HEAD_DIM)
            sink = jnp.concatenate(
                [jnp.broadcast_to(r["sink"][g * GQA_GROUP + hh:g * GQA_GROUP + hh + 1, 0:1],
                                  (WINDOW, 1)) for hh in range(GQA_GROUP)], axis=0) * LOG2E
            blocks = []
            for cb in range(lc // WINDOW):
                crow = slice(cb * WINDOW, (cb + 1) * WINDOW)
                s = lax.dot_general(q_st, r["kvc"][crow, k_cols], dn, preferred_element_type=F32)
                blocks.append((s, r["kvc"][crow, v_cols]))
            if local:
                if qb == 0:
                    k_p, v_p = r["kv_prev"][:, k_cols], r["kv_prev"][:, v_cols]
                    ok_p = i > 0
                else:
                    prows = slice((qb - 1) * WINDOW, qb * WINDOW)
                    k_p, v_p = r["kv"][prows, k_cols], r["kv"][prows, v_cols]
                    ok_p = True
                if qb == n_qb - 1:
                    k_n, v_n = r["kv_next"][:, k_cols], r["kv_next"][:, v_cols]
                    ok_n = i < nt - 1
                else:
                    nrows = slice((qb + 1) * WINDOW, (qb + 2) * WINDOW)
                    k_n, v_n = r["kv"][nrows, k_cols], r["kv"][nrows, v_cols]
                    ok_n = True
                s_p = lax.dot_general(q_st, k_p, dn, preferred_element_type=F32)
                s_c = lax.dot_general(q_st, r["kv"][rows, k_cols], dn, preferred_element_type=F32)
                s_n = lax.dot_general(q_st, k_n, dn, preferred_element_type=F32)
                s_p = jnp.where((kj >= qi) & ok_p, s_p, neg_inf)
                s_n = jnp.where((kj <= qi) & ok_n, s_n, neg_inf)
                blocks += [(s_p, v_p), (s_c, r["kv"][rows, v_cols]), (s_n, v_n)]
            m_el = blocks[0][0]
            for s, _ in blocks[1:]:
                m_el = jnp.maximum(m_el, s)
            m = jnp.maximum(jnp.max(m_el, axis=-1, keepdims=True), sink)
            den_el = None
            o = None
            for s, v in blocks:
                p = jnp.exp2(s - m)
                den_el = p if den_el is None else den_el + p
                pv = jnp.dot(p.astype(BF16), v, preferred_element_type=F32)
                o = pv if o is None else o + pv
            den = jnp.sum(den_el, axis=-1, keepdims=True) + jnp.exp2(sink - m)
            o = o / den
            for hh in range(GQA_GROUP):
                c0 = (g * GQA_GROUP + hh) * HEAD_DIM
                gate = r["attn_gate"][rows, c0:c0 + HEAD_DIM].astype(F32)
                ycat_ref[rows, D_CONV + D_LRU + c0:D_CONV + D_LRU + c0 + HEAD_DIM] = (
                    o[hh * WINDOW:(hh + 1) * WINDOW, :] * gate).astype(BF16)

    out_ref = r["out"]
    for c0 in range(0, out_ref.shape[1], OUT_CHUNK_N):
        cols = slice(c0, c0 + OUT_CHUNK_N)
        y = jnp.dot(ycat_ref[...], r["w_out"][:, cols], preferred_element_type=F32)
        out_ref[:, cols] = r["x"][:, cols] + r["gate"][:, cols] * y
    if final:
        xn = out_ref[...]
        ms = jnp.mean(xn * xn, axis=-1, keepdims=True)
        out_ref[...] = xn * lax.rsqrt(ms + EPS) * r["final_g"][...]


def _mix_call(x2, u2, q4, uc2, ylru, ada4, layer, ada_row_fn, wts, *, tq, rows_per_batch,
              ctx_rows, local, final):
    m, d = x2.shape
    nt = rows_per_batch // tq
    nb = m // rows_per_batch
    names, specs, args = [], [], []

    def add(name, arr, spec):
        names.append(name)
        specs.append(spec)
        args.append(arr)

    def rowblk(width, col_off):
        return pl.BlockSpec((tq, width), lambda b, i: (b * nt + i, col_off // width))

    def halo(rows, width, col_off, nxt):
        per = tq // rows
        last = m // rows - 1
        if nxt:
            fn = lambda b, i: (jnp.minimum((b * nt + i + 1) * per, last), col_off // width)
        else:
            fn = lambda b, i: (jnp.maximum((b * nt + i) * per - 1, 0), col_off // width)
        return pl.BlockSpec((rows, width), fn)

    full = lambda shape: pl.BlockSpec(shape, lambda b, i: (0,) * len(shape),
                                      pipeline_mode=pl.Buffered(1))

    add("x", x2, pl.BlockSpec((tq, d), lambda b, i: (b * nt + i, 0)))
    add("gate", ada4, pl.BlockSpec((None, None, 1, d), lambda b, i: (layer, ada_row_fn(b), 0, 2)))
    add("glu", u2, rowblk(D_CONV, OFF_CONV_GLU))
    if local:
        add("glu_prev", u2, halo(CONV_HALO, D_CONV, OFF_CONV_GLU, False))
        add("glu_next", u2, halo(CONV_HALO, D_CONV, OFF_CONV_GLU, True))
    add("gates", u2, rowblk(D_CONV + D_LRU, OFF_CONV_GATE))
    ylru_spec = pl.BlockSpec((tq, D_LRU), lambda b, i: (b * nt + i, 0))
    if isinstance(ylru, tuple):
        add("ylru", ylru[0], ylru_spec)
        add("ylru_rev", ylru[1], ylru_spec)
    else:
        add("ylru", ylru, ylru_spec)
    add("q", q4, pl.BlockSpec((tq // WINDOW, N_Q_HEADS, WINDOW, HEAD_DIM),
                              lambda b, i: (b * nt + i, 0, 0, 0)))
    add("attn_gate", u2, rowblk(D_ATTN, U_ATTN_GATE))
    if local:
        add("kv", u2, rowblk(2 * D_KV, OFF_K))
        add("kv_prev", u2, halo(WINDOW, 2 * D_KV, OFF_K, False))
        add("kv_next", u2, halo(WINDOW, 2 * D_KV, OFF_K, True))
    add("kvc", uc2, pl.BlockSpec((ctx_rows, 2 * D_KV), lambda b, i: (b, OFF_K // (2 * D_KV))))
    for name in ("dw_w", "dw_b", "ln_g", "ln_b", "pw_w", "pw_b", "sink", "w_out"):
        shape = wts[name].shape[1:]
        add(name, wts[name], pl.BlockSpec((None,) + shape,
                                          lambda b, i, nd=len(shape): (layer,) + (0,) * nd,
                                          pipeline_mode=pl.Buffered(1)))
    if final:
        add("final_g", wts["final_g"], full(wts["final_g"].shape))
    names += ["out", "ext", "ycat"]
    return pl.pallas_call(
        functools.partial(_mix_kernel, names=tuple(names), tq=tq, local=local, final=final),
        grid=(nb, nt),
        in_specs=specs,
        out_specs=pl.BlockSpec((tq, d), lambda b, i: (b * nt + i, 0)),
        out_shape=jax.ShapeDtypeStruct((m, d), F32),
        scratch_shapes=[pltpu.VMEM((tq + 2 * CONV_HALO, D_CONV), F32),
                        pltpu.VMEM((tq, D_MODEL), BF16)],
        compiler_params=pltpu.CompilerParams(
            dimension_semantics=("arbitrary", "arbitrary"), vmem_limit_bytes=VMEM_LIMIT),
        name="mix_lat" if local else "mix_ctx",
    )(*args)


def _rope_tables(seq):
    rows = seq // GRID_W
    row = np.repeat(np.arange(rows, dtype=np.float64), GRID_W)
    col = np.tile(np.arange(GRID_W, dtype=np.float64), rows)
    half = HEAD_DIM // 2
    inv = ROPE_BASE ** (-np.arange(0, half, 2, dtype=np.float64) / half)
    ang_r = row[:, None] * inv[None, :]
    ang_c = col[:, None] * inv[None, :]
    ang = np.concatenate([ang_r, ang_r, ang_c, ang_c], axis=-1)
    cos, sin = np.cos(ang), np.sin(ang)
    first = (np.arange(HEAD_DIM) % half) < (half // 2)
    return tuple(jnp.asarray(t, F32)
                 for t in (cos, np.where(first, -sin, 0.0), np.where(first, 0.0, sin)))


def _block_diag(w):
    nblk, blk = w.shape[-3], w.shape[-2]
    n = nblk * blk
    rows = w.reshape(w.shape[:-3] + (n, blk))
    spread = jnp.asarray(np.arange(n)[None, :] % blk == np.arange(blk)[:, None], w.dtype)
    mask = jnp.asarray(np.arange(n)[:, None] // blk == np.arange(n)[None, :] // blk, w.dtype)
    return jnp.matmul(rows, spread, precision=lax.Precision.HIGHEST) * mask


def _pick_tile(n, pref):
    t = min(n, pref)
    while n % t:
        t //= 2
    return t


def kernel(x, c, ctx, c_ctx, norm_g, w_ada, b_ada, w_in, dw_w, dw_b, ln_g, ln_b, pw_w, pw_b,
           lru_conv_w, lru_conv_b, lru_w_r, lru_b_r, lru_w_i, lru_b_i, lru_lam, attn_sink,
           w_out, final_g):
    nb, seq, d = x.shape
    lc = ctx.shape[1]
    depth = w_in.shape[0]
    ctx_row = nb

    c_rows = jnp.concatenate([c, c_ctx[None, :], jnp.zeros((ADA_ROWS - nb - 1, d), F32)], axis=0)
    ada = _ada_call(c_rows, w_ada, b_ada)
    ada4 = ada.reshape(depth, ADA_ROWS, 1, 3 * d)
    rope_tabs = _rope_tables(seq)
    w_bf = w_in.astype(BF16)
    norm_g3 = norm_g.reshape(depth, 1, d)
    vec4 = lambda p: p.reshape(depth, 2, 1, D_LRU)
    per_group = LANES // lru_w_r.shape[-1]
    grp = lambda w: _block_diag(w.reshape(w.shape[:2] + (-1, per_group) + w.shape[-2:]))
    wg_half = (0.5 * jnp.concatenate([grp(lru_w_r), grp(lru_w_i)], axis=-1)).astype(BF16)
    lru_params = (lru_conv_w, vec4(lru_conv_b), wg_half, vec4(lru_b_r), vec4(lru_b_i),
                  vec4(lru_lam))
    vec3 = lambda p: p.reshape(depth, 1, D_CONV)
    wts = {
        "dw_w": jnp.broadcast_to(dw_w[:, :, None, :], (depth, CONV_WIDTH, SUBLANES, D_CONV)),
        "dw_b": vec3(dw_b), "ln_g": vec3(ln_g), "ln_b": vec3(ln_b),
        "pw_w": pw_w.astype(BF16), "pw_b": vec3(pw_b),
        "sink": jnp.broadcast_to(attn_sink[:, :, None], (depth, N_Q_HEADS, HEAD_DIM)),
        "w_out": w_out.astype(BF16), "final_g": final_g.reshape(1, d),
    }

    x2 = x.reshape(nb * seq, d)
    xc2 = ctx.reshape(nb * lc, d)
    tm = _pick_tile(seq, 1024)
    tmc = _pick_tile(nb * lc, 1024)
    tq = _pick_tile(seq, 512)
    tt = _pick_tile(seq, 2048)
    tu = _pick_tile(tt, 512)
    zeros_h0 = jnp.zeros((nb, SUBLANES, D_LRU), F32)

    for l in range(depth):
        last = l == depth - 1
        u2, q4 = _inproj_call(x2, norm_g3, ada4, l, w_bf, rope_tabs, tm=tm, rows_per_batch=seq,
                              ada_row_fn=lambda b: b, with_q=True)
        uc2, qc4 = _inproj_call(xc2, norm_g3, ada4, l, w_bf, None, tm=tmc, rows_per_batch=tmc,
                                ada_row_fn=lambda b: ctx_row, with_q=not last)

        ctx_kw = dict(rows_per_batch=lc, tt=lc, units=1)
        lat_kw = dict(rows_per_batch=seq, tt=tt, units=tt // tu)
        hp_c, xp_c, st_f = _lru_call(uc2, zeros_h0, None, lru_params, l, 0, **ctx_kw)
        ylru_c, st_r = _lru_call(None, zeros_h0, (xp_c, hp_c), lru_params, l, 1, **ctx_kw)
        hp, xp, _ = _lru_call(u2, st_f, None, lru_params, l, 0, **lat_kw)
        ylru, _ = _lru_call(None, st_r, (xp, hp), lru_params, l, 1, **lat_kw)

        x2_new = _mix_call(x2, u2, q4, uc2, ylru, ada4, l, lambda b: b, wts, tq=tq,
                           rows_per_batch=seq, ctx_rows=lc, local=True, final=last)
        if not last:
            xc2 = _mix_call(xc2, uc2, qc4, uc2, ylru_c, ada4, l, lambda b: ctx_row, wts, tq=lc,
                            rows_per_batch=lc, ctx_rows=lc, local=False, final=False)
        x2 = x2_new
    return x2.reshape(nb, seq, d)
```

```python
import functools
import math

import jax
import jax.numpy as jnp
import numpy as np
from jax import lax
from jax.experimental import pallas as pl
from jax.experimental.pallas import tpu as pltpu

F32 = jnp.float32
BF16 = jnp.bfloat16

D_MODEL = 2048
D_CONV = 512
D_LRU = 512
HEAD_DIM = 128
N_Q_HEADS = 8
N_KV_HEADS = 2
GQA_GROUP = N_Q_HEADS // N_KV_HEADS
D_ATTN = N_Q_HEADS * HEAD_DIM
D_KV = N_KV_HEADS * HEAD_DIM
GRID_W = 64
CONV_WIDTH = 31
CONV_PAD = (CONV_WIDTH - 1) // 2
LRU_CONV_WIDTH = 4
LRU_BLOCKS = 8
LRU_C = 8.0
WINDOW = 128
ROPE_BASE = 10000.0
EPS = 1e-6

OFF_K = D_LRU
OFF_V = OFF_K + D_KV
OFF_MEM_END = OFF_V + D_KV
OFF_CONV_GLU = OFF_MEM_END
OFF_CONV_GATE = OFF_CONV_GLU + 2 * D_CONV
OFF_LRU_GATE = OFF_CONV_GATE + D_CONV
OFF_Q = OFF_LRU_GATE + D_LRU
OFF_ATTN_GATE = OFF_Q + D_ATTN
D_IN = OFF_ATTN_GATE + D_ATTN
U_ATTN_GATE = OFF_Q
D_U = D_IN - D_ATTN

ADA_ROWS = 8
IN_TILE_N = 1024
IN_CHUNK_N = 256
Q_TILE = OFF_Q // IN_TILE_N
GLU_TILE = OFF_CONV_GLU // IN_TILE_N
OUT_CHUNK_N = 256
CONV_HALO = 16
CONV_ROWS = 64
CONV_COLS = 256
SUBLANES = 8
BF16_SUBLANES = 16
LANES = 128
VMEM_LIMIT = 56 * 1024 * 1024
F32_TINY = float(np.finfo(np.float32).tiny)
LOG2E = math.log2(math.e)
Q_SCALE = HEAD_DIM ** -0.5 * LOG2E

assert OFF_Q % IN_TILE_N == 0 and D_ATTN == IN_TILE_N and OFF_MEM_END == IN_TILE_N
assert OFF_CONV_GLU % IN_TILE_N == 0 and 2 * D_CONV == IN_TILE_N


def _sigmoid(x):
    return 0.5 * jnp.tanh(0.5 * x) + 0.5


def _silu(x):
    hx = 0.5 * x
    return hx * jnp.tanh(hx) + hx


def _ada_kernel(c_ref, w_ref, b_ref, o_ref):
    ca = _silu(c_ref[...])
    o_ref[...] = jnp.dot(ca.astype(BF16), w_ref[...].astype(BF16),
                         preferred_element_type=F32) + b_ref[...]


def _ada_call(c_rows, w_ada, b_ada):
    depth, d, n = w_ada.shape
    tn = 1024
    return pl.pallas_call(
        _ada_kernel,
        grid=(depth, n // tn),
        in_specs=[
            pl.BlockSpec((ADA_ROWS, d), lambda l, j: (0, 0)),
            pl.BlockSpec((None, d, tn), lambda l, j: (l, 0, j)),
            pl.BlockSpec((None, 1, tn), lambda l, j: (l, 0, j)),
        ],
        out_specs=pl.BlockSpec((None, ADA_ROWS, tn), lambda l, j: (l, 0, j)),
        out_shape=jax.ShapeDtypeStruct((depth, ADA_ROWS, n), F32),
        compiler_params=pltpu.CompilerParams(
            dimension_semantics=("arbitrary", "arbitrary"), vmem_limit_bytes=VMEM_LIMIT),
        name="ada_proj",
    )(c_rows, w_ada, b_ada.reshape(depth, 1, n))


def _rope(t, cos, sin_a, sin_b):
    return (t * cos + pltpu.roll(t, HEAD_DIM - 32, 1) * sin_a + pltpu.roll(t, 32, 1) * sin_b)


def _inproj_kernel(*refs, rope, with_q, tm, n_side):
    refs = list(refs)
    x_ref, g_ref, shift_ref, scale_ref, w_ref = refs[:5]
    del refs[:5]
    if rope:
        cos_ref, sa_ref, sb_ref = refs[:3]
        del refs[:3]
    side_src = refs[:n_side]
    del refs[:n_side]
    o_ref = refs.pop(0)
    q_ref = refs.pop(0) if with_q else None
    side_dst = refs[:n_side]
    del refs[:n_side]
    h_ref = refs.pop(0)
    j = pl.program_id(1)

    for src, dst in zip(side_src, side_dst):
        dst[...] = src[...].astype(BF16)

    @pl.when(j == 0)
    def _():
        x = x_ref[...]
        ms = jnp.mean(x * x, axis=-1, keepdims=True)
        gain = g_ref[...] * (1.0 + scale_ref[...])
        h_ref[...] = (x * lax.rsqrt(ms + EPS) * gain + shift_ref[...]).astype(BF16)

    chunks = [slice(c0, c0 + IN_CHUNK_N) for c0 in range(0, IN_TILE_N, IN_CHUNK_N)]
    heads_per_chunk = IN_CHUNK_N // HEAD_DIM

    def proj(cols):
        return jnp.dot(h_ref[...], w_ref[:, cols], preferred_element_type=F32)

    def rope_heads(u):
        if not rope:
            return [u[:, hh * HEAD_DIM:(hh + 1) * HEAD_DIM] for hh in range(heads_per_chunk)]
        return [_rope(u[:, hh * HEAD_DIM:(hh + 1) * HEAD_DIM], cos_ref[...], sa_ref[...],
                      sb_ref[...]) for hh in range(heads_per_chunk)]

    def plain_tile():
        for cols in chunks:
            o_ref[:, cols] = proj(cols).astype(BF16)

    def q_tile():
        for ci, cols in enumerate(chunks):
            for hh, t in enumerate(rope_heads(proj(cols))):
                t = (t * Q_SCALE).astype(BF16)
                for qb in range(tm // WINDOW):
                    q_ref[qb, ci * heads_per_chunk + hh] = t[qb * WINDOW:(qb + 1) * WINDOW, :]

    def kv_tile():
        for cols in chunks:
            u = proj(cols)
            if OFF_K <= cols.start < OFF_V:
                u = jnp.concatenate(rope_heads(u), axis=1)
            o_ref[:, cols] = u.astype(BF16)

    def glu_tile():
        half = len(chunks) // 2
        for ci in range(half):
            o_ref[:, chunks[ci]] = (proj(chunks[ci]) * _sigmoid(proj(chunks[ci + half]))
                                    ).astype(BF16)
            o_ref[:, chunks[ci + half]] = jnp.zeros((tm, IN_CHUNK_N), BF16)

    def silu_tile():
        for cols in chunks:
            o_ref[:, cols] = _silu(proj(cols)).astype(BF16)

    if not with_q:
        plain_tile()
    else:
        pl.when(j == 0)(kv_tile if rope else plain_tile)
        pl.when(j == GLU_TILE)(glu_tile)
        pl.when(j == Q_TILE)(q_tile)
        pl.when((j != 0) & (j != GLU_TILE) & (j != Q_TILE))(silu_tile)


def _inproj_call(x2, norm_g, ada4, layer, w_bf, rope_tabs, *, tm, rows_per_batch, ada_row_fn,
                 with_q, side_casts=()):
    m, d = x2.shape
    tiles_per_batch = rows_per_batch // tm
    rope = rope_tabs is not None
    n_tiles = D_IN // IN_TILE_N if with_q else 1
    n_steps = (m // tm) * n_tiles

    def ada_spec(part):
        return pl.BlockSpec((None, None, 1, d),
                            lambda i, j: (layer, ada_row_fn(i // tiles_per_batch), 0, part))

    in_specs = [
        pl.BlockSpec((tm, d), lambda i, j: (i, 0)),
        pl.BlockSpec((None, 1, d), lambda i, j: (layer, 0, 0)),
        ada_spec(0), ada_spec(1),
        pl.BlockSpec((d, IN_TILE_N), lambda i, j: (0, j)),
    ]
    args = [x2, norm_g, ada4, ada4, w_bf]
    if rope:
        tab_spec = pl.BlockSpec((tm, HEAD_DIM), lambda i, j: (i % tiles_per_batch, 0))
        in_specs += [tab_spec] * 3
        args += list(rope_tabs)
    out_specs = [pl.BlockSpec((tm, IN_TILE_N), lambda i, j: (i, jnp.where(j > Q_TILE, j - 1, j)))]
    out_shape = [jax.ShapeDtypeStruct((m, D_U if with_q else IN_TILE_N), BF16)]
    if with_q:
        out_specs.append(pl.BlockSpec((tm // WINDOW, N_Q_HEADS, WINDOW, HEAD_DIM),
                                      lambda i, j: (i, 0, 0, 0)))
        out_shape.append(jax.ShapeDtypeStruct((m // WINDOW, N_Q_HEADS, WINDOW, HEAD_DIM), BF16))
    for src, src_layer in side_casts:
        rows, cols = src.shape[1:]
        piece = BF16_SUBLANES
        while rows // piece > n_steps:
            piece *= 2
        n_pieces = rows // piece
        step_piece = lambda i, j: jnp.minimum(i * n_tiles + j, n_pieces - 1)
        in_specs.append(pl.BlockSpec((None, piece, cols),
                                     lambda i, j, sl=src_layer, f=step_piece: (sl, f(i, j), 0)))
        args.append(src)
        out_specs.append(pl.BlockSpec((piece, cols), lambda i, j, f=step_piece: (f(i, j), 0)))
        out_shape.append(jax.ShapeDtypeStruct((rows, cols), BF16))
    outs = pl.pallas_call(
        functools.partial(_inproj_kernel, rope=rope, with_q=with_q, tm=tm,
                          n_side=len(side_casts)),
        grid=(m // tm, n_tiles),
        in_specs=in_specs,
        out_specs=out_specs,
        out_shape=out_shape,
        scratch_shapes=[pltpu.VMEM((tm, d), BF16)],
        compiler_params=pltpu.CompilerParams(
            dimension_semantics=("arbitrary", "arbitrary"), vmem_limit_bytes=VMEM_LIMIT),
        name="in_proj",
    )(*args)
    n_main = 2 if with_q else 1
    return outs[0], (outs[1] if with_q else None), list(outs[n_main:])


def _lru_kernel(*refs, reverse, tt, units):
    if reverse:
        (x_ref, h0_ref, acc_ref, pmat_ref, cw_ref, cb_ref, wg_ref, br_ref, bi_ref, lam_ref,
         o_ref, state_ref, ext3_ref, a3_ref, b3_ref, edge_ref, h_ref) = refs
        xp_ref = None
    else:
        (x_ref, h0_ref, pmat_ref, cw_ref, cb_ref, wg_ref, br_ref, bi_ref, lam_ref,
         o_ref, xp_ref, state_ref, ext3_ref, a3_ref, b3_ref, edge_ref, h_ref) = refs
        acc_ref = None
    c = pl.program_id(1)
    tu = tt // units
    seg = tu // SUBLANES
    hist = LRU_CONV_WIDTH - 1
    row = lax.broadcasted_iota(jnp.int32, (SUBLANES, D_LRU), 0)

    def blk(g):
        return slice(g * SUBLANES, (g + 1) * SUBLANES)

    @pl.when(c == 0)
    def _():
        edge_ref[...] = jnp.zeros((hist * SUBLANES, D_LRU), F32)
        h_ref[...] = h0_ref[...]

    lam = lam_ref[...]
    softplus_neg_lam = jnp.maximum(-lam, 0.0) + jnp.log1p(jnp.exp(-jnp.abs(lam)))
    half_k = (-0.5 * LRU_C) * softplus_neg_lam
    half_br = 0.5 * br_ref[...]
    half_bi = 0.5 * bi_ref[...]

    order = list(range(units - 1, -1, -1) if reverse else range(units))
    for u in order:
        _lru_gates(slice(u * tu, (u + 1) * tu), x_ref, xp_ref, pmat_ref, cw_ref, cb_ref, wg_ref,
                   half_br, half_bi, half_k, ext3_ref.at[u], a3_ref.at[u], b3_ref.at[u], edge_ref,
                   reverse=reverse, tt=tu, seg=seg, hist=hist, row=row, blk=blk)

    def step(t, carry):
        g_ = (seg - 1 - t) if reverse else t
        rows = pl.ds(pl.multiple_of(g_ * SUBLANES, SUBLANES), SUBLANES)
        new = []
        for u, (h, p) in zip(order, carry):
            a = a3_ref[u, rows, :]
            h = a * h + b3_ref[u, rows, :]
            p = a * p
            b3_ref[u, rows, :] = h
            a3_ref[u, rows, :] = p
            new.append((h, p))
        return tuple(new)

    init = tuple((jnp.zeros((SUBLANES, D_LRU), F32), jnp.ones((SUBLANES, D_LRU), F32))
                 for _ in order)
    finals = lax.fori_loop(0, seg, step, init, unroll=8)

    for u, (h_fin, p_fin) in zip(order, finals):
        _lru_finish(slice(u * tu, (u + 1) * tu), h_fin, p_fin, acc_ref, pmat_ref, o_ref,
                    a3_ref.at[u], b3_ref.at[u], h_ref, reverse=reverse, tt=tu, seg=seg, row=row)
    state_ref[...] = h_ref[...]


def _lru_gates(rows_u, x_ref, xp_ref, pmat_ref, cw_ref, cb_ref, wg_ref, half_br, half_bi, half_k,
               ext_ref, a_ref, b_ref, edge_ref, *, reverse, tt, seg, hist, row, blk):
    base = 0 if reverse else hist
    if reverse:
        xp = x_ref[rows_u, :].astype(F32)
    else:
        xp = jnp.dot(pmat_ref[...], x_ref[rows_u, :], preferred_element_type=F32)
        xp_ref[rows_u, :] = xp.astype(BF16)
    ext_ref[base * SUBLANES:base * SUBLANES + tt, :] = xp
    for j in range(1, hist + 1):
        if reverse:
            rolled = pltpu.roll(ext_ref[blk(j - 1), :], SUBLANES - 1, 0)
            ext_ref[blk(seg + j - 1), :] = jnp.where(row == SUBLANES - 1, edge_ref[blk(j - 1), :],
                                                     rolled)
        else:
            rolled = pltpu.roll(ext_ref[blk(hist + seg - j), :], 1, 0)
            ext_ref[blk(hist - j), :] = jnp.where(row == 0, edge_ref[blk(j - 1), :], rolled)
        edge_ref[blk(j - 1), :] = rolled

    xc = jnp.broadcast_to(cb_ref[...], (tt, D_LRU))
    for k in range(LRU_CONV_WIDTH):
        off = (hist - k) if reverse else k
        xc = xc + cw_ref[k:k + 1, :] * ext_ref[off * SUBLANES:off * SUBLANES + tt, :]

    xcb = xc.astype(BF16)
    g_r, g_i = [], []
    for j in range(D_LRU // LANES):
        gj = jnp.dot(xcb[:, j * LANES:(j + 1) * LANES], wg_ref[j], preferred_element_type=F32)
        g_r.append(gj[:, :LANES])
        g_i.append(gj[:, LANES:])
    t_r = jnp.tanh(jnp.concatenate(g_r, axis=1) + half_br)
    t_i = jnp.tanh(jnp.concatenate(g_i, axis=1) + half_bi)
    log_a = half_k * t_r + half_k
    a_ref[...] = jnp.exp(log_a)
    th = jnp.tanh(log_a)
    one_minus_a2 = (-2.0 * th) / (1.0 - th)
    root = one_minus_a2 * lax.rsqrt(jnp.maximum(one_minus_a2, F32_TINY))
    half_xc = 0.5 * xc
    b_ref[...] = root * (half_xc * t_i + half_xc)


def _lru_finish(rows_u, h_fin, p_fin, acc_ref, pmat_ref, o_ref, a_ref, b_ref, h_ref, *, reverse,
                tt, seg, row):
    a, b = p_fin, h_fin
    for d in (1, 2, 4):
        keep = (row < SUBLANES - d) if reverse else (row >= d)
        shift = (SUBLANES - d) if reverse else d
        a_s = jnp.where(keep, pltpu.roll(a, shift, 0), 1.0)
        b_s = jnp.where(keep, pltpu.roll(b, shift, 0), 0.0)
        b = a * b_s + b
        a = a * a_s
    h_in = h_ref[...]
    end = a * h_in + b
    if reverse:
        carry_in = jnp.where(row < SUBLANES - 1, pltpu.roll(end, SUBLANES - 1, 0), h_in)
        h_ref[...] = jnp.broadcast_to(end[0:1, :], (SUBLANES, D_LRU))
    else:
        carry_in = jnp.where(row >= 1, pltpu.roll(end, 1, 0), h_in)
        h_ref[...] = jnp.broadcast_to(end[SUBLANES - 1:SUBLANES, :], (SUBLANES, D_LRU))

    h_all = (b_ref[...].reshape(seg, SUBLANES, D_LRU)
             + a_ref[...].reshape(seg, SUBLANES, D_LRU) * carry_in[None]).reshape(tt, D_LRU)
    if reverse:
        h_sum = (h_all + acc_ref[rows_u, :].astype(F32)).astype(BF16)
        o_ref[rows_u, :] = jnp.dot(pmat_ref[...], h_sum, preferred_element_type=F32
                                   ).astype(o_ref.dtype)
    else:
        o_ref[rows_u, :] = h_all.astype(o_ref.dtype)


def _lru_call(x_rows, h0, fwd, params, layer, direction, *, rows_per_batch, tt, units):
    cw, cb, wg_half, br, bi, lam = params
    reverse = direction == 1
    m = x_rows.shape[0] if not reverse else fwd[0].shape[0]
    nb = m // rows_per_batch
    nc = rows_per_batch // tt
    tu = tt // units

    def chunk(c):
        return (nc - 1 - c) if reverse else c

    row_spec = pl.BlockSpec((tt, D_LRU), lambda b, c: (b * nc + chunk(c), 0))
    full = lambda shape: pl.BlockSpec(shape, lambda b, c: (0,) * len(shape))
    h0_spec = pl.BlockSpec((None, SUBLANES, D_LRU), lambda b, c: (b, 0, 0))
    seg = tu // SUBLANES
    src = (np.arange(tu) % SUBLANES) * seg + np.arange(tu) // SUBLANES
    perm = src[:, None] == np.arange(tu)[None, :]
    if reverse:
        in_specs = [row_spec, h0_spec, row_spec]
        args = [fwd[0], h0, fwd[1], jnp.asarray(perm.T, BF16)]
    else:
        in_specs = [row_spec, h0_spec]
        args = [x_rows, h0, jnp.asarray(perm, BF16)]
    sel = lambda shape: pl.BlockSpec((None, None) + shape,
                                     lambda b, c: (layer, direction) + (0,) * len(shape))
    in_specs += [full((tu, tu)),
                 sel((LRU_CONV_WIDTH, D_LRU)), sel((1, D_LRU)), sel(wg_half.shape[2:]),
                 sel((1, D_LRU)), sel((1, D_LRU)), sel((1, D_LRU))]
    args += [cw, cb, wg_half, br, bi, lam]
    edge_rows = (LRU_CONV_WIDTH - 1) * SUBLANES
    rows_out = jax.ShapeDtypeStruct((m, D_LRU), BF16)
    state_spec = pl.BlockSpec((None, SUBLANES, D_LRU), lambda b, c: (b, 0, 0))
    state_out = jax.ShapeDtypeStruct((nb, SUBLANES, D_LRU), F32)
    return pl.pallas_call(
        functools.partial(_lru_kernel, reverse=reverse, tt=tt, units=units),
        grid=(nb, nc),
        in_specs=in_specs,
        out_specs=[row_spec, state_spec] if reverse else [row_spec, row_spec, state_spec],
        out_shape=[rows_out, state_out] if reverse else [rows_out, rows_out, state_out],
        scratch_shapes=[pltpu.VMEM((units, tu + edge_rows, D_LRU), F32),
                        pltpu.VMEM((units, tu, D_LRU), F32),
                        pltpu.VMEM((units, tu, D_LRU), F32),
                        pltpu.VMEM((edge_rows, D_LRU), F32),
                        pltpu.VMEM((SUBLANES, D_LRU), F32)],
        compiler_params=pltpu.CompilerParams(
            dimension_semantics=("arbitrary", "arbitrary"), vmem_limit_bytes=VMEM_LIMIT),
        name="rglru_rev" if reverse else "rglru_fwd",
    )(*args)


def _mix_kernel(*refs, names, tq, local, final):
    r = dict(zip(names, refs))
    i = pl.program_id(1)
    nt = pl.num_programs(1)
    ext_ref, ycat_ref = r["ext"], r["ycat"]

    def glu(ref):
        return ref[...].astype(F32)

    ext_ref[CONV_HALO:CONV_HALO + tq, :] = glu(r["glu"])
    zero_halo = jnp.zeros((CONV_HALO, D_CONV), F32)
    if local:
        ext_ref[0:CONV_HALO, :] = jnp.where(i > 0, glu(r["glu_prev"]), zero_halo)
        ext_ref[CONV_HALO + tq:, :] = jnp.where(i < nt - 1, glu(r["glu_next"]), zero_halo)
    else:
        ext_ref[0:CONV_HALO, :] = zero_halo
        ext_ref[CONV_HALO + tq:, :] = zero_halo

    for rc in range(tq // CONV_ROWS):
        t0 = rc * CONV_ROWS
        rows = slice(t0, t0 + CONV_ROWS)
        pieces = []
        for cc in range(D_CONV // CONV_COLS):
            cols = slice(cc * CONV_COLS, (cc + 1) * CONV_COLS)
            acc = jnp.broadcast_to(r["dw_b"][:, cols], (CONV_ROWS, CONV_COLS))
            for b in range(SUBLANES):
                z = None
                for a in range(-(-(CONV_WIDTH + 1) // SUBLANES)):
                    o = SUBLANES * a + b
                    if o < 1 or o > CONV_WIDTH:
                        continue
                    xs = ext_ref[t0 + SUBLANES * a:t0 + SUBLANES * a + CONV_ROWS + SUBLANES, cols]
                    term = (xs.reshape(-1, SUBLANES, CONV_COLS) * r["dw_w"][o - 1, :, cols][None]
                            ).reshape(CONV_ROWS + SUBLANES, CONV_COLS)
                    z = term if z is None else z + term
                acc = acc + z[b:b + CONV_ROWS, :]
            pieces.append(acc)
        acc = jnp.concatenate(pieces, axis=1)
        mu = jnp.mean(acc, axis=-1, keepdims=True)
        cen = acc - mu
        var = jnp.mean(cen * cen, axis=-1, keepdims=True)
        y = cen * lax.rsqrt(var + EPS) * r["ln_g"][...] + r["ln_b"][...]
        y = _silu(y).astype(BF16)
        y = jnp.dot(y, r["pw_w"][...], preferred_element_type=F32) + r["pw_b"][...]
        y = y * r["gates"][rows, :D_CONV].astype(F32)
        ycat_ref[rows, 0:D_CONV] = y.astype(BF16)

    ylru = r["ylru"][...].astype(F32)
    if "ylru_rev" in r:
        ylru = ylru + r["ylru_rev"][...].astype(F32)
    ycat_ref[:, D_CONV:D_CONV + D_LRU] = (
        ylru * r["gates"][:, D_CONV:].astype(F32)).astype(BF16)

    qrows = GQA_GROUP * WINDOW
    qi = lax.broadcasted_iota(jnp.int32, (qrows, WINDOW), 0) % WINDOW
    kj = lax.broadcasted_iota(jnp.int32, (qrows, WINDOW), 1)
    neg_inf = jnp.float32(-jnp.inf)
    dn = (((1,), (1,)), ((), ()))
    n_qb = tq // WINDOW
    lc = r["kvc"].shape[0]
    for qb in range(n_qb):
        rows = slice(qb * WINDOW, (qb + 1) * WINDOW)
        for g in range(N_KV_HEADS):
            k_cols = slice(g * HEAD_DIM, (g + 1) * HEAD_DIM)
            v_cols = slice(D_KV + g * HEAD_DIM, D_KV + (g + 1) * HEAD_DIM)
            q_st = r["q"][qb, g * GQA_GROUP:(g + 1) * GQA_GROUP].reshape(qrows, HEAD_DIM)
            sink = jnp.concatenate(
                [jnp.broadcast_to(r["sink"][g * GQA_GROUP + hh:g * GQA_GROUP + hh + 1, 0:1],
                                  (WINDOW, 1)) for hh in range(GQA_GROUP)], axis=0) * LOG2E
            blocks = []
            for cb in range(lc // WINDOW):
                crow = slice(cb * WINDOW, (cb + 1) * WINDOW)
                s = lax.dot_general(q_st, r["kvc"][crow, k_cols], dn, preferred_element_type=F32)
                blocks.append((s, r["kvc"][crow, v_cols]))
            if local:
                if qb == 0:
                    k_p, v_p = r["kv_prev"][:, k_cols], r["kv_prev"][:, v_cols]
                    ok_p = i > 0
                else:
                    prows = slice((qb - 1) * WINDOW, qb * WINDOW)
                    k_p, v_p = r["kv"][prows, k_cols], r["kv"][prows, v_cols]
                    ok_p = True
                if qb == n_qb - 1:
                    k_n, v_n = r["kv_next"][:, k_cols], r["kv_next"][:, v_cols]
                    ok_n = i < nt - 1
                else:
                    nrows = slice((qb + 1) * WINDOW, (qb + 2) * WINDOW)
                    k_n, v_n = r["kv"][nrows, k_cols], r["kv"][nrows, v_cols]
                    ok_n = True
                s_p = lax.dot_general(q_st, k_p, dn, preferred_element_type=F32)
                s_c = lax.dot_general(q_st, r["kv"][rows, k_cols], dn, preferred_element_type=F32)
                s_n = lax.dot_general(q_st, k_n, dn, preferred_element_type=F32)
                s_p = jnp.where((kj >= qi) & ok_p, s_p, neg_inf)
                s_n = jnp.where((kj <= qi) & ok_n, s_n, neg_inf)
                blocks += [(s_p, v_p), (s_c, r["kv"][rows, v_cols]), (s_n, v_n)]
            m_el = blocks[0][0]
            for s, _ in blocks[1:]:
                m_el = jnp.maximum(m_el, s)
            m = jnp.maximum(jnp.max(m_el, axis=-1, keepdims=True), sink)
            den_el = None
            o = None
            for s, v in blocks:
                p = jnp.exp2(s - m)
                den_el = p if den_el is None else den_el + p
                pv = jnp.dot(p.astype(BF16), v, preferred_element_type=F32)
                o = pv if o is None else o + pv
            den = jnp.sum(den_el, axis=-1, keepdims=True) + jnp.exp2(sink - m)
            o = o / den
            for hh in range(GQA_GROUP):
                c0 = (g * GQA_GROUP + hh) * HEAD_DIM
                gate = r["attn_gate"][rows, c0:c0 + HEAD_DIM].astype(F32)
                ycat_ref[rows, D_CONV + D_LRU + c0:D_CONV + D_LRU + c0 + HEAD_DIM] = (
                    o[hh * WINDOW:(hh + 1) * WINDOW, :] * gate).astype(BF16)

    out_ref = r["out"]
    for c0 in range(0, out_ref.shape[1], OUT_CHUNK_N):
        cols = slice(c0, c0 + OUT_CHUNK_N)
        y = jnp.dot(ycat_ref[...], r["w_out"][:, cols], preferred_element_type=F32)
        out_ref[:, cols] = r["x"][:, cols] + r["gate"][:, cols] * y
    if final:
        xn = out_ref[...]
        ms = jnp.mean(xn * xn, axis=-1, keepdims=True)
        out_ref[...] = xn * lax.rsqrt(ms + EPS) * r["final_g"][...]


def _mix_call(x2, u2, q4, uc2, ylru, ada4, layer, ada_row_fn, wts, *, tq, rows_per_batch,
              ctx_rows, local, final):
    m, d = x2.shape
    nt = rows_per_batch // tq
    nb = m // rows_per_batch
    names, specs, args = [], [], []

    def add(name, arr, spec):
        names.append(name)
        specs.append(spec)
        args.append(arr)

    def rowblk(width, col_off):
        return pl.BlockSpec((tq, width), lambda b, i: (b * nt + i, col_off // width))

    def halo(rows, width, col_off, nxt):
        per = tq // rows
        last = m // rows - 1
        if nxt:
            fn = lambda b, i: (jnp.minimum((b * nt + i + 1) * per, last), col_off // width)
        else:
            fn = lambda b, i: (jnp.maximum((b * nt + i) * per - 1, 0), col_off // width)
        return pl.BlockSpec((rows, width), fn)

    full = lambda shape: pl.BlockSpec(shape, lambda b, i: (0,) * len(shape),
                                      pipeline_mode=pl.Buffered(1))

    add("x", x2, pl.BlockSpec((tq, d), lambda b, i: (b * nt + i, 0)))
    add("gate", ada4, pl.BlockSpec((None, None, 1, d), lambda b, i: (layer, ada_row_fn(b), 0, 2)))
    add("glu", u2, rowblk(D_CONV, OFF_CONV_GLU))
    if local:
        add("glu_prev", u2, halo(CONV_HALO, D_CONV, OFF_CONV_GLU, False))
        add("glu_next", u2, halo(CONV_HALO, D_CONV, OFF_CONV_GLU, True))
    add("gates", u2, rowblk(D_CONV + D_LRU, OFF_CONV_GATE))
    ylru_spec = pl.BlockSpec((tq, D_LRU), lambda b, i: (b * nt + i, 0))
    if isinstance(ylru, tuple):
        add("ylru", ylru[0], ylru_spec)
        add("ylru_rev", ylru[1], ylru_spec)
    else:
        add("ylru", ylru, ylru_spec)
    add("q", q4, pl.BlockSpec((tq // WINDOW, N_Q_HEADS, WINDOW, HEAD_DIM),
                              lambda b, i: (b * nt + i, 0, 0, 0)))
    add("attn_gate", u2, rowblk(D_ATTN, U_ATTN_GATE))
    if local:
        add("kv", u2, rowblk(2 * D_KV, OFF_K))
        add("kv_prev", u2, halo(WINDOW, 2 * D_KV, OFF_K, False))
        add("kv_next", u2, halo(WINDOW, 2 * D_KV, OFF_K, True))
    add("kvc", uc2, pl.BlockSpec((ctx_rows, 2 * D_KV), lambda b, i: (b, OFF_K // (2 * D_KV))))
    for name in ("dw_w", "dw_b", "ln_g", "ln_b", "pw_w", "pw_b", "sink"):
        shape = wts[name].shape[1:]
        add(name, wts[name], pl.BlockSpec((None,) + shape,
                                          lambda b, i, nd=len(shape): (layer,) + (0,) * nd,
                                          pipeline_mode=pl.Buffered(1)))
    add("w_out", wts["w_out"], full(wts["w_out"].shape))
    if final:
        add("final_g", wts["final_g"], full(wts["final_g"].shape))
    names += ["out", "ext", "ycat"]
    return pl.pallas_call(
        functools.partial(_mix_kernel, names=tuple(names), tq=tq, local=local, final=final),
        grid=(nb, nt),
        in_specs=specs,
        out_specs=pl.BlockSpec((tq, d), lambda b, i: (b * nt + i, 0)),
        out_shape=jax.ShapeDtypeStruct((m, d), F32),
        scratch_shapes=[pltpu.VMEM((tq + 2 * CONV_HALO, D_CONV), F32),
                        pltpu.VMEM((tq, D_MODEL), BF16)],
        compiler_params=pltpu.CompilerParams(
            dimension_semantics=("arbitrary", "arbitrary"), vmem_limit_bytes=VMEM_LIMIT),
        name="mix_lat" if local else "mix_ctx",
    )(*args)


def _rope_tables(seq):
    rows = seq // GRID_W
    row = np.repeat(np.arange(rows, dtype=np.float64), GRID_W)
    col = np.tile(np.arange(GRID_W, dtype=np.float64), rows)
    half = HEAD_DIM // 2
    inv = ROPE_BASE ** (-np.arange(0, half, 2, dtype=np.float64) / half)
    ang_r = row[:, None] * inv[None, :]
    ang_c = col[:, None] * inv[None, :]
    ang = np.concatenate([ang_r, ang_r, ang_c, ang_c], axis=-1)
    cos, sin = np.cos(ang), np.sin(ang)
    first = (np.arange(HEAD_DIM) % half) < (half // 2)
    return tuple(jnp.asarray(t, F32)
                 for t in (cos, np.where(first, -sin, 0.0), np.where(first, 0.0, sin)))


def _block_diag(w):
    nblk, blk = w.shape[-3], w.shape[-2]
    n = nblk * blk
    rows = w.reshape(w.shape[:-3] + (n, blk))
    spread = jnp.asarray(np.arange(n)[None, :] % blk == np.arange(blk)[:, None], w.dtype)
    mask = jnp.asarray(np.arange(n)[:, None] // blk == np.arange(n)[None, :] // blk, w.dtype)
    return jnp.matmul(rows, spread, precision=lax.Precision.HIGHEST) * mask


def _pick_tile(n, pref):
    t = min(n, pref)
    while n % t:
        t //= 2
    return t


def kernel(x, c, ctx, c_ctx, norm_g, w_ada, b_ada, w_in, dw_w, dw_b, ln_g, ln_b, pw_w, pw_b,
           lru_conv_w, lru_conv_b, lru_w_r, lru_b_r, lru_w_i, lru_b_i, lru_lam, attn_sink,
           w_out, final_g):
    nb, seq, d = x.shape
    lc = ctx.shape[1]
    depth = w_in.shape[0]
    ctx_row = nb

    c_rows = jnp.concatenate([c, c_ctx[None, :], jnp.zeros((ADA_ROWS - nb - 1, d), F32)], axis=0)
    ada = _ada_call(c_rows, w_ada, b_ada)
    ada4 = ada.reshape(depth, ADA_ROWS, 1, 3 * d)
    rope_tabs = _rope_tables(seq)
    norm_g3 = norm_g.reshape(depth, 1, d)
    vec4 = lambda p: p.reshape(depth, 2, 1, D_LRU)
    per_group = LANES // lru_w_r.shape[-1]
    grp = lambda w: _block_diag(w.reshape(w.shape[:2] + (-1, per_group) + w.shape[-2:]))
    wg_half = (0.5 * jnp.concatenate([grp(lru_w_r), grp(lru_w_i)], axis=-1)).astype(BF16)
    lru_params = (lru_conv_w, vec4(lru_conv_b), wg_half, vec4(lru_b_r), vec4(lru_b_i),
                  vec4(lru_lam))
    vec3 = lambda p: p.reshape(depth, 1, D_CONV)
    wts = {
        "dw_w": jnp.broadcast_to(dw_w[:, :, None, :], (depth, CONV_WIDTH, SUBLANES, D_CONV)),
        "dw_b": vec3(dw_b), "ln_g": vec3(ln_g), "ln_b": vec3(ln_b),
        "pw_w": pw_w.astype(BF16), "pw_b": vec3(pw_b),
        "sink": jnp.broadcast_to(attn_sink[:, :, None], (depth, N_Q_HEADS, HEAD_DIM)),
        "final_g": final_g.reshape(1, d),
    }

    x2 = x.reshape(nb * seq, d)
    xc2 = ctx.reshape(nb * lc, d)
    tm = _pick_tile(seq, 1024)
    tmc = _pick_tile(nb * lc, 1024)
    tq = _pick_tile(seq, 512)
    tt = _pick_tile(seq, 2048)
    tu = _pick_tile(tt, 512)
    zeros_h0 = jnp.zeros((nb, SUBLANES, D_LRU), F32)

    w_bf = w_in[0].astype(BF16)
    for l in range(depth):
        last = l == depth - 1
        side = [(w_out, l)] + ([] if last else [(w_in, l + 1)])
        u2, q4, casts = _inproj_call(x2, norm_g3, ada4, l, w_bf, rope_tabs, tm=tm,
                                     rows_per_batch=seq, ada_row_fn=lambda b: b, with_q=True,
                                     side_casts=side)
        uc2, qc4, _ = _inproj_call(xc2, norm_g3, ada4, l, w_bf, None, tm=tmc, rows_per_batch=tmc,
                                   ada_row_fn=lambda b: ctx_row, with_q=not last)
        wts["w_out"] = casts[0]
        if not last:
            w_bf = casts[1]

        ctx_kw = dict(rows_per_batch=lc, tt=lc, units=1)
        lat_kw = dict(rows_per_batch=seq, tt=tt, units=tt // tu)
        hp_c, xp_c, st_f = _lru_call(uc2, zeros_h0, None, lru_params, l, 0, **ctx_kw)
        ylru_c, st_r = _lru_call(None, zeros_h0, (xp_c, hp_c), lru_params, l, 1, **ctx_kw)
        hp, xp, _ = _lru_call(u2, st_f, None, lru_params, l, 0, **lat_kw)
        ylru, _ = _lru_call(None, st_r, (xp, hp), lru_params, l, 1, **lat_kw)

        x2_new = _mix_call(x2, u2, q4, uc2, ylru, ada4, l, lambda b: b, wts, tq=tq,
                           rows_per_batch=seq, ctx_rows=lc, local=True, final=last)
        if not last:
            xc2 = _mix_call(xc2, uc2, qc4, uc2, ylru_c, ada4, l, lambda b: ctx_row, wts, tq=lc,
                            rows_per_batch=lc, ctx_rows=lc, local=False, final=False)
        x2 = x2_new
    return x2.reshape(nb, seq, d)
```

```python
import functools
import math

import jax
import jax.numpy as jnp
import numpy as np
from jax import lax
from jax.experimental import pallas as pl
from jax.experimental.pallas import tpu as pltpu

F32 = jnp.float32
BF16 = jnp.bfloat16

D_MODEL = 2048
D_CONV = 512
D_LRU = 512
HEAD_DIM = 128
N_Q_HEADS = 8
N_KV_HEADS = 2
GQA_GROUP = N_Q_HEADS // N_KV_HEADS
D_ATTN = N_Q_HEADS * HEAD_DIM
D_KV = N_KV_HEADS * HEAD_DIM
GRID_W = 64
CONV_WIDTH = 31
CONV_PAD = (CONV_WIDTH - 1) // 2
LRU_CONV_WIDTH = 4
LRU_C = 8.0
WINDOW = 128
ROPE_BASE = 10000.0
EPS = 1e-6

OFF_K = D_LRU
OFF_V = OFF_K + D_KV
OFF_MEM_END = OFF_V + D_KV
OFF_CONV_GLU = OFF_MEM_END
OFF_CONV_GATE = OFF_CONV_GLU + 2 * D_CONV
OFF_LRU_GATE = OFF_CONV_GATE + D_CONV
OFF_Q = OFF_LRU_GATE + D_LRU
OFF_ATTN_GATE = OFF_Q + D_ATTN
D_IN = OFF_ATTN_GATE + D_ATTN
U_ATTN_GATE = OFF_Q
D_U = D_IN - D_ATTN

ADA_ROWS = 8
ADA_TILE_N = 1024
IN_TILE_N = 1024
IN_CHUNK_N = 256
Q_TILE = OFF_Q // IN_TILE_N
GLU_TILE = OFF_CONV_GLU // IN_TILE_N
OUT_CHUNK_N = 256
CONV_HALO = 16
CONV_ROWS = 64
CONV_COLS = 256
SUBLANES = 8
BF16_SUBLANES = 16
LANES = 128
VMEM_LIMIT = 56 * 1024 * 1024
F32_TINY = float(np.finfo(np.float32).tiny)
LOG2E = math.log2(math.e)
Q_SCALE = HEAD_DIM ** -0.5 * LOG2E

assert OFF_Q % IN_TILE_N == 0 and D_ATTN == IN_TILE_N and OFF_MEM_END == IN_TILE_N
assert OFF_CONV_GLU % IN_TILE_N == 0 and 2 * D_CONV == IN_TILE_N
assert CONV_HALO >= CONV_PAD and CONV_HALO % BF16_SUBLANES == 0


def _sigmoid(x):
    return 0.5 * jnp.tanh(0.5 * x) + 0.5


def _silu(x):
    hx = 0.5 * x
    return hx * jnp.tanh(hx) + hx


def _ada_kernel(c_ref, w_ref, b_ref, o_ref):
    ca = _silu(c_ref[...])
    o_ref[...] = jnp.dot(ca.astype(BF16), w_ref[...].astype(BF16),
                         preferred_element_type=F32) + b_ref[...]


def _ada_call(c_rows, w_ada, b_ada):
    depth, d, n = w_ada.shape
    tn = ADA_TILE_N
    return pl.pallas_call(
        _ada_kernel,
        grid=(depth, n // tn),
        in_specs=[
            pl.BlockSpec((ADA_ROWS, d), lambda l, j: (0, 0)),
            pl.BlockSpec((None, d, tn), lambda l, j: (l, 0, j)),
            pl.BlockSpec((None, 1, tn), lambda l, j: (l, 0, j)),
        ],
        out_specs=pl.BlockSpec((None, ADA_ROWS, tn), lambda l, j: (l, 0, j)),
        out_shape=jax.ShapeDtypeStruct((depth, ADA_ROWS, n), F32),
        compiler_params=pltpu.CompilerParams(
            dimension_semantics=("arbitrary", "arbitrary"), vmem_limit_bytes=VMEM_LIMIT),
        name="ada_proj",
    )(c_rows, w_ada, b_ada.reshape(depth, 1, n))


def _rope(t, cos, sin_a, sin_b):
    quarter = HEAD_DIM // 4
    return (t * cos + pltpu.roll(t, HEAD_DIM - quarter, 1) * sin_a
            + pltpu.roll(t, quarter, 1) * sin_b)


def _inproj_kernel(*refs, rope, with_q, tm, n_side):
    refs = list(refs)
    x_ref, g_ref, shift_ref, scale_ref, w_ref = refs[:5]
    del refs[:5]
    if rope:
        cos_ref, sa_ref, sb_ref = refs[:3]
        del refs[:3]
    side_src = refs[:n_side]
    del refs[:n_side]
    o_ref = refs.pop(0)
    q_ref = refs.pop(0) if with_q else None
    side_dst = refs[:n_side]
    del refs[:n_side]
    h_ref = refs.pop(0)
    j = pl.program_id(1)

    for src, dst in zip(side_src, side_dst):
        dst[...] = src[...].astype(BF16)

    @pl.when(j == 0)
    def _():
        x = x_ref[...]
        ms = jnp.mean(x * x, axis=-1, keepdims=True)
        gain = g_ref[...] * (1.0 + scale_ref[...])
        h_ref[...] = (x * lax.rsqrt(ms + EPS) * gain + shift_ref[...]).astype(BF16)

    chunks = [slice(c0, c0 + IN_CHUNK_N) for c0 in range(0, IN_TILE_N, IN_CHUNK_N)]
    heads_per_chunk = IN_CHUNK_N // HEAD_DIM

    def proj(cols):
        return jnp.dot(h_ref[...], w_ref[:, cols], preferred_element_type=F32)

    def rope_heads(u):
        if not rope:
            return [u[:, hh * HEAD_DIM:(hh + 1) * HEAD_DIM] for hh in range(heads_per_chunk)]
        return [_rope(u[:, hh * HEAD_DIM:(hh + 1) * HEAD_DIM], cos_ref[...], sa_ref[...],
                      sb_ref[...]) for hh in range(heads_per_chunk)]

    def plain_tile():
        for cols in chunks:
            o_ref[:, cols] = proj(cols).astype(BF16)

    def q_tile():
        for ci, cols in enumerate(chunks):
            for hh, t in enumerate(rope_heads(proj(cols))):
                t = (t * Q_SCALE).astype(BF16)
                for qb in range(tm // WINDOW):
                    q_ref[qb, ci * heads_per_chunk + hh] = t[qb * WINDOW:(qb + 1) * WINDOW, :]

    def kv_tile():
        for cols in chunks:
            u = proj(cols)
            if OFF_K <= cols.start < OFF_V:
                u = jnp.concatenate(rope_heads(u), axis=1)
            o_ref[:, cols] = u.astype(BF16)

    def glu_tile():
        half = len(chunks) // 2
        for ci in range(half):
            o_ref[:, chunks[ci]] = (proj(chunks[ci]) * _sigmoid(proj(chunks[ci + half]))
                                    ).astype(BF16)
            o_ref[:, chunks[ci + half]] = jnp.zeros((tm, IN_CHUNK_N), BF16)

    def silu_tile():
        for cols in chunks:
            o_ref[:, cols] = _silu(proj(cols)).astype(BF16)

    if not with_q:
        plain_tile()
    else:
        pl.when(j == 0)(kv_tile if rope else plain_tile)
        pl.when(j == GLU_TILE)(glu_tile)
        pl.when(j == Q_TILE)(q_tile)
        pl.when((j != 0) & (j != GLU_TILE) & (j != Q_TILE))(silu_tile)


def _inproj_call(x2, norm_g, ada4, layer, w_bf, rope_tabs, *, tm, rows_per_batch, ada_row_fn,
                 with_q, side_casts=()):
    m, d = x2.shape
    tiles_per_batch = rows_per_batch // tm
    rope = rope_tabs is not None
    n_tiles = D_IN // IN_TILE_N if with_q else 1
    n_steps = (m // tm) * n_tiles

    def ada_spec(part):
        return pl.BlockSpec((None, None, 1, d),
                            lambda i, j: (layer, ada_row_fn(i // tiles_per_batch), 0, part))

    in_specs = [
        pl.BlockSpec((tm, d), lambda i, j: (i, 0)),
        pl.BlockSpec((None, 1, d), lambda i, j: (layer, 0, 0)),
        ada_spec(0), ada_spec(1),
        pl.BlockSpec((d, IN_TILE_N), lambda i, j: (0, j)),
    ]
    args = [x2, norm_g, ada4, ada4, w_bf]
    if rope:
        tab_spec = pl.BlockSpec((tm, HEAD_DIM), lambda i, j: (i % tiles_per_batch, 0))
        in_specs += [tab_spec] * 3
        args += list(rope_tabs)
    out_specs = [pl.BlockSpec((tm, IN_TILE_N), lambda i, j: (i, jnp.where(j > Q_TILE, j - 1, j)))]
    out_shape = [jax.ShapeDtypeStruct((m, D_U if with_q else IN_TILE_N), BF16)]
    if with_q:
        out_specs.append(pl.BlockSpec((tm // WINDOW, N_Q_HEADS, WINDOW, HEAD_DIM),
                                      lambda i, j: (i, 0, 0, 0)))
        out_shape.append(jax.ShapeDtypeStruct((m // WINDOW, N_Q_HEADS, WINDOW, HEAD_DIM), BF16))
    for src, src_layer in side_casts:
        rows, cols = src.shape[1:]
        piece = BF16_SUBLANES
        while rows // piece > n_steps:
            piece *= 2
        n_pieces = rows // piece
        assert n_pieces * piece == rows and n_pieces <= n_steps
        step_piece = lambda i, j: jnp.minimum(i * n_tiles + j, n_pieces - 1)
        in_specs.append(pl.BlockSpec((None, piece, cols),
                                     lambda i, j, sl=src_layer, f=step_piece: (sl, f(i, j), 0)))
        args.append(src)
        out_specs.append(pl.BlockSpec((piece, cols), lambda i, j, f=step_piece: (f(i, j), 0)))
        out_shape.append(jax.ShapeDtypeStruct((rows, cols), BF16))
    outs = pl.pallas_call(
        functools.partial(_inproj_kernel, rope=rope, with_q=with_q, tm=tm,
                          n_side=len(side_casts)),
        grid=(m // tm, n_tiles),
        in_specs=in_specs,
        out_specs=out_specs,
        out_shape=out_shape,
        scratch_shapes=[pltpu.VMEM((tm, d), BF16)],
        compiler_params=pltpu.CompilerParams(
            dimension_semantics=("arbitrary", "arbitrary"), vmem_limit_bytes=VMEM_LIMIT),
        name="in_proj",
    )(*args)
    n_main = 2 if with_q else 1
    return outs[0], (outs[1] if with_q else None), list(outs[n_main:])


def _lru_kernel(*refs, reverse, tt, units):
    if reverse:
        (x_ref, h0_ref, acc_ref, pmat_ref, cw_ref, cb_ref, wg_ref, br_ref, bi_ref, lam_ref,
         o_ref, state_ref, ext3_ref, a3_ref, b3_ref, edge_ref, h_ref) = refs
        xp_ref = None
    else:
        (x_ref, h0_ref, pmat_ref, cw_ref, cb_ref, wg_ref, br_ref, bi_ref, lam_ref,
         o_ref, xp_ref, state_ref, ext3_ref, a3_ref, b3_ref, edge_ref, h_ref) = refs
        acc_ref = None
    c = pl.program_id(1)
    tu = tt // units
    seg = tu // SUBLANES
    hist = LRU_CONV_WIDTH - 1
    row = lax.broadcasted_iota(jnp.int32, (SUBLANES, D_LRU), 0)

    def blk(g):
        return slice(g * SUBLANES, (g + 1) * SUBLANES)

    @pl.when(c == 0)
    def _():
        edge_ref[...] = jnp.zeros((hist * SUBLANES, D_LRU), F32)
        h_ref[...] = h0_ref[...]

    lam = lam_ref[...]
    softplus_neg_lam = jnp.maximum(-lam, 0.0) + jnp.log1p(jnp.exp(-jnp.abs(lam)))
    half_k = (-0.5 * LRU_C) * softplus_neg_lam
    half_br = 0.5 * br_ref[...]
    half_bi = 0.5 * bi_ref[...]

    order = list(range(units - 1, -1, -1) if reverse else range(units))
    for u in order:
        _lru_gates(slice(u * tu, (u + 1) * tu), x_ref, xp_ref, pmat_ref, cw_ref, cb_ref, wg_ref,
                   half_br, half_bi, half_k, ext3_ref.at[u], a3_ref.at[u], b3_ref.at[u], edge_ref,
                   reverse=reverse, tt=tu, seg=seg, hist=hist, row=row, blk=blk)

    def step(t, carry):
        g_ = (seg - 1 - t) if reverse else t
        rows = pl.ds(pl.multiple_of(g_ * SUBLANES, SUBLANES), SUBLANES)
        new = []
        for u, (h, p) in zip(order, carry):
            a = a3_ref[u, rows, :]
            h = a * h + b3_ref[u, rows, :]
            p = a * p
            b3_ref[u, rows, :] = h
            a3_ref[u, rows, :] = p
            new.append((h, p))
        return tuple(new)

    init = tuple((jnp.zeros((SUBLANES, D_LRU), F32), jnp.ones((SUBLANES, D_LRU), F32))
                 for _ in order)
    finals = lax.fori_loop(0, seg, step, init, unroll=8)

    for u, (h_fin, p_fin) in zip(order, finals):
        _lru_finish(slice(u * tu, (u + 1) * tu), h_fin, p_fin, acc_ref, pmat_ref, o_ref,
                    a3_ref.at[u], b3_ref.at[u], h_ref, reverse=reverse, tt=tu, seg=seg, row=row)
    state_ref[...] = h_ref[...]


def _lru_gates(rows_u, x_ref, xp_ref, pmat_ref, cw_ref, cb_ref, wg_ref, half_br, half_bi, half_k,
               ext_ref, a_ref, b_ref, edge_ref, *, reverse, tt, seg, hist, row, blk):
    base = 0 if reverse else hist
    if reverse:
        xp = x_ref[rows_u, :].astype(F32)
    else:
        xp = jnp.dot(pmat_ref[...], x_ref[rows_u, :], preferred_element_type=F32)
        xp_ref[rows_u, :] = xp.astype(BF16)
    ext_ref[base * SUBLANES:base * SUBLANES + tt, :] = xp
    for j in range(1, hist + 1):
        if reverse:
            rolled = pltpu.roll(ext_ref[blk(j - 1), :], SUBLANES - 1, 0)
            ext_ref[blk(seg + j - 1), :] = jnp.where(row == SUBLANES - 1, edge_ref[blk(j - 1), :],
                                                     rolled)
        else:
            rolled = pltpu.roll(ext_ref[blk(hist + seg - j), :], 1, 0)
            ext_ref[blk(hist - j), :] = jnp.where(row == 0, edge_ref[blk(j - 1), :], rolled)
        edge_ref[blk(j - 1), :] = rolled

    xc = jnp.broadcast_to(cb_ref[...], (tt, D_LRU))
    for k in range(LRU_CONV_WIDTH):
        off = (hist - k) if reverse else k
        xc = xc + cw_ref[k:k + 1, :] * ext_ref[off * SUBLANES:off * SUBLANES + tt, :]

    xcb = xc.astype(BF16)
    g_r, g_i = [], []
    for j in range(D_LRU // LANES):
        gj = jnp.dot(xcb[:, j * LANES:(j + 1) * LANES], wg_ref[j], preferred_element_type=F32)
        g_r.append(gj[:, :LANES])
        g_i.append(gj[:, LANES:])
    t_r = jnp.tanh(jnp.concatenate(g_r, axis=1) + half_br)
    t_i = jnp.tanh(jnp.concatenate(g_i, axis=1) + half_bi)
    log_a = half_k * t_r + half_k
    a_ref[...] = jnp.exp(log_a)
    th = jnp.tanh(log_a)
    one_minus_a2 = (-2.0 * th) / (1.0 - th)
    root = one_minus_a2 * lax.rsqrt(jnp.maximum(one_minus_a2, F32_TINY))
    half_xc = 0.5 * xc
    b_ref[...] = root * (half_xc * t_i + half_xc)


def _lru_finish(rows_u, h_fin, p_fin, acc_ref, pmat_ref, o_ref, a_ref, b_ref, h_ref, *, reverse,
                tt, seg, row):
    a, b = p_fin, h_fin
    for d in (1, 2, 4):
        keep = (row < SUBLANES - d) if reverse else (row >= d)
        shift = (SUBLANES - d) if reverse else d
        a_s = jnp.where(keep, pltpu.roll(a, shift, 0), 1.0)
        b_s = jnp.where(keep, pltpu.roll(b, shift, 0), 0.0)
        b = a * b_s + b
        a = a * a_s
    h_in = h_ref[...]
    end = a * h_in + b
    if reverse:
        carry_in = jnp.where(row < SUBLANES - 1, pltpu.roll(end, SUBLANES - 1, 0), h_in)
        h_ref[...] = jnp.broadcast_to(end[0:1, :], (SUBLANES, D_LRU))
    else:
        carry_in = jnp.where(row >= 1, pltpu.roll(end, 1, 0), h_in)
        h_ref[...] = jnp.broadcast_to(end[SUBLANES - 1:SUBLANES, :], (SUBLANES, D_LRU))

    h_all = (b_ref[...].reshape(seg, SUBLANES, D_LRU)
             + a_ref[...].reshape(seg, SUBLANES, D_LRU) * carry_in[None]).reshape(tt, D_LRU)
    if reverse:
        h_sum = (h_all + acc_ref[rows_u, :].astype(F32)).astype(BF16)
        o_ref[rows_u, :] = jnp.dot(pmat_ref[...], h_sum, preferred_element_type=F32
                                   ).astype(o_ref.dtype)
    else:
        o_ref[rows_u, :] = h_all.astype(o_ref.dtype)


def _lru_call(x_rows, h0, fwd, params, layer, direction, *, rows_per_batch, tt, units):
    cw, cb, wg_half, br, bi, lam = params
    reverse = direction == 1
    m = x_rows.shape[0] if not reverse else fwd[0].shape[0]
    nb = m // rows_per_batch
    nc = rows_per_batch // tt
    tu = tt // units

    def chunk(c):
        return (nc - 1 - c) if reverse else c

    row_spec = pl.BlockSpec((tt, D_LRU), lambda b, c: (b * nc + chunk(c), 0))
    full = lambda shape: pl.BlockSpec(shape, lambda b, c: (0,) * len(shape))
    h0_spec = pl.BlockSpec((None, SUBLANES, D_LRU), lambda b, c: (b, 0, 0))
    seg = tu // SUBLANES
    src = (np.arange(tu) % SUBLANES) * seg + np.arange(tu) // SUBLANES
    perm = src[:, None] == np.arange(tu)[None, :]
    if reverse:
        in_specs = [row_spec, h0_spec, row_spec]
        args = [fwd[0], h0, fwd[1], jnp.asarray(perm.T, BF16)]
    else:
        in_specs = [row_spec, h0_spec]
        args = [x_rows, h0, jnp.asarray(perm, BF16)]
    sel = lambda shape: pl.BlockSpec((None, None) + shape,
                                     lambda b, c: (layer, direction) + (0,) * len(shape))
    in_specs += [full((tu, tu)),
                 sel((LRU_CONV_WIDTH, D_LRU)), sel((1, D_LRU)), sel(wg_half.shape[2:]),
                 sel((1, D_LRU)), sel((1, D_LRU)), sel((1, D_LRU))]
    args += [cw, cb, wg_half, br, bi, lam]
    edge_rows = (LRU_CONV_WIDTH - 1) * SUBLANES
    rows_out = jax.ShapeDtypeStruct((m, D_LRU), BF16)
    state_spec = pl.BlockSpec((None, SUBLANES, D_LRU), lambda b, c: (b, 0, 0))
    state_out = jax.ShapeDtypeStruct((nb, SUBLANES, D_LRU), F32)
    return pl.pallas_call(
        functools.partial(_lru_kernel, reverse=reverse, tt=tt, units=units),
        grid=(nb, nc),
        in_specs=in_specs,
        out_specs=[row_spec, state_spec] if reverse else [row_spec, row_spec, state_spec],
        out_shape=[rows_out, state_out] if reverse else [rows_out, rows_out, state_out],
        scratch_shapes=[pltpu.VMEM((units, tu + edge_rows, D_LRU), F32),
                        pltpu.VMEM((units, tu, D_LRU), F32),
                        pltpu.VMEM((units, tu, D_LRU), F32),
                        pltpu.VMEM((edge_rows, D_LRU), F32),
                        pltpu.VMEM((SUBLANES, D_LRU), F32)],
        compiler_params=pltpu.CompilerParams(
            dimension_semantics=("arbitrary", "arbitrary"), vmem_limit_bytes=VMEM_LIMIT),
        name="rglru_rev" if reverse else "rglru_fwd",
    )(*args)


def _mix_kernel(*refs, names, tq, local, final):
    r = dict(zip(names, refs))
    i = pl.program_id(1)
    nt = pl.num_programs(1)
    ext_ref, ycat_ref = r["ext"], r["ycat"]

    def glu(ref):
        return ref[...].astype(F32)

    ext_ref[CONV_HALO:CONV_HALO + tq, :] = glu(r["glu"])
    zero_halo = jnp.zeros((CONV_HALO, D_CONV), F32)
    if local:
        ext_ref[0:CONV_HALO, :] = jnp.where(i > 0, glu(r["glu_prev"]), zero_halo)
        ext_ref[CONV_HALO + tq:, :] = jnp.where(i < nt - 1, glu(r["glu_next"]), zero_halo)
    else:
        ext_ref[0:CONV_HALO, :] = zero_halo
        ext_ref[CONV_HALO + tq:, :] = zero_halo

    for rc in range(tq // CONV_ROWS):
        t0 = rc * CONV_ROWS
        rows = slice(t0, t0 + CONV_ROWS)
        pieces = []
        for cc in range(D_CONV // CONV_COLS):
            cols = slice(cc * CONV_COLS, (cc + 1) * CONV_COLS)
            acc = jnp.broadcast_to(r["dw_b"][:, cols], (CONV_ROWS, CONV_COLS))
            for b in range(SUBLANES):
                z = None
                for a in range(-(-(CONV_WIDTH + 1) // SUBLANES)):
                    o = SUBLANES * a + b
                    if o < 1 or o > CONV_WIDTH:
                        continue
                    xs = ext_ref[t0 + SUBLANES * a:t0 + SUBLANES * a + CONV_ROWS + SUBLANES, cols]
                    term = (xs.reshape(-1, SUBLANES, CONV_COLS) * r["dw_w"][o - 1, :, cols][None]
                            ).reshape(CONV_ROWS + SUBLANES, CONV_COLS)
                    z = term if z is None else z + term
                acc = acc + z[b:b + CONV_ROWS, :]
            pieces.append(acc)
        acc = jnp.concatenate(pieces, axis=1)
        mu = jnp.mean(acc, axis=-1, keepdims=True)
        cen = acc - mu
        var = jnp.mean(cen * cen, axis=-1, keepdims=True)
        y = cen * lax.rsqrt(var + EPS) * r["ln_g"][...] + r["ln_b"][...]
        y = _silu(y).astype(BF16)
        y = jnp.dot(y, r["pw_w"][...], preferred_element_type=F32) + r["pw_b"][...]
        y = y * r["gates"][rows, :D_CONV].astype(F32)
        ycat_ref[rows, 0:D_CONV] = y.astype(BF16)

    ycat_ref[:, D_CONV:D_CONV + D_LRU] = (
        r["ylru"][...].astype(F32) * r["gates"][:, D_CONV:].astype(F32)).astype(BF16)

    qrows = GQA_GROUP * WINDOW
    qi = lax.broadcasted_iota(jnp.int32, (qrows, WINDOW), 0) % WINDOW
    kj = lax.broadcasted_iota(jnp.int32, (qrows, WINDOW), 1)
    neg_inf = jnp.float32(-jnp.inf)
    dn = (((1,), (1,)), ((), ()))
    n_qb = tq // WINDOW
    lc = r["kvc"].shape[0]
    for qb in range(n_qb):
        rows = slice(qb * WINDOW, (qb + 1) * WINDOW)
        for g in range(N_KV_HEADS):
            k_cols = slice(g * HEAD_DIM, (g + 1) * HEAD_DIM)
            v_cols = slice(D_KV + g * HEAD_DIM, D_KV + (g + 1) * HEAD_DIM)
            q_st = r["q"][qb, g * GQA_GROUP:(g + 1) * GQA_GROUP].reshape(qrows, HEAD_DIM)
            sink = jnp.concatenate(
                [jnp.broadcast_to(r["sink"][g * GQA_GROUP + hh:g * GQA_GROUP + hh + 1, 0:1],
                                  (WINDOW, 1)) for hh in range(GQA_GROUP)], axis=0) * LOG2E
            blocks = []
            for cb in range(lc // WINDOW):
                crow = slice(cb * WINDOW, (cb + 1) * WINDOW)
                s = lax.dot_general(q_st, r["kvc"][crow, k_cols], dn, preferred_element_type=F32)
                blocks.append((s, r["kvc"][crow, v_cols]))
            if local:
                if qb == 0:
                    k_p, v_p = r["kv_prev"][:, k_cols], r["kv_prev"][:, v_cols]
                    ok_p = i > 0
                else:
                    prows = slice((qb - 1) * WINDOW, qb * WINDOW)
                    k_p, v_p = r["kv"][prows, k_cols], r["kv"][prows, v_cols]
                    ok_p = True
                if qb == n_qb - 1:
                    k_n, v_n = r["kv_next"][:, k_cols], r["kv_next"][:, v_cols]
                    ok_n = i < nt - 1
                else:
                    nrows = slice((qb + 1) * WINDOW, (qb + 2) * WINDOW)
                    k_n, v_n = r["kv"][nrows, k_cols], r["kv"][nrows, v_cols]
                    ok_n = True
                s_p = lax.dot_general(q_st, k_p, dn, preferred_element_type=F32)
                s_c = lax.dot_general(q_st, r["kv"][rows, k_cols], dn, preferred_element_type=F32)
                s_n = lax.dot_general(q_st, k_n, dn, preferred_element_type=F32)
                s_p = jnp.where((kj >= qi) & ok_p, s_p, neg_inf)
                s_n = jnp.where((kj <= qi) & ok_n, s_n, neg_inf)
                blocks += [(s_p, v_p), (s_c, r["kv"][rows, v_cols]), (s_n, v_n)]
            m_el = blocks[0][0]
            for s, _ in blocks[1:]:
                m_el = jnp.maximum(m_el, s)
            m = jnp.maximum(jnp.max(m_el, axis=-1, keepdims=True), sink)
            den_el = None
            o = None
            for s, v in blocks:
                p = jnp.exp2(s - m)
                den_el = p if den_el is None else den_el + p
                pv = jnp.dot(p.astype(BF16), v, preferred_element_type=F32)
                o = pv if o is None else o + pv
            den = jnp.sum(den_el, axis=-1, keepdims=True) + jnp.exp2(sink - m)
            o = o / den
            for hh in range(GQA_GROUP):
                c0 = (g * GQA_GROUP + hh) * HEAD_DIM
                gate = r["attn_gate"][rows, c0:c0 + HEAD_DIM].astype(F32)
                ycat_ref[rows, D_CONV + D_LRU + c0:D_CONV + D_LRU + c0 + HEAD_DIM] = (
                    o[hh * WINDOW:(hh + 1) * WINDOW, :] * gate).astype(BF16)

    out_ref = r["out"]
    for c0 in range(0, out_ref.shape[1], OUT_CHUNK_N):
        cols = slice(c0, c0 + OUT_CHUNK_N)
        y = jnp.dot(ycat_ref[...], r["w_out"][:, cols], preferred_element_type=F32)
        out_ref[:, cols] = r["x"][:, cols] + r["gate"][:, cols] * y
    if final:
        xn = out_ref[...]
        ms = jnp.mean(xn * xn, axis=-1, keepdims=True)
        out_ref[...] = xn * lax.rsqrt(ms + EPS) * r["final_g"][...]


def _mix_call(x2, u2, q4, uc2, ylru, ada4, layer, ada_row_fn, wts, *, tq, rows_per_batch,
              ctx_rows, local, final):
    m, d = x2.shape
    nt = rows_per_batch // tq
    nb = m // rows_per_batch
    names, specs, args = [], [], []

    def add(name, arr, spec):
        names.append(name)
        specs.append(spec)
        args.append(arr)

    def rowblk(width, col_off):
        return pl.BlockSpec((tq, width), lambda b, i: (b * nt + i, col_off // width))

    def halo(rows, width, col_off, nxt):
        per = tq // rows
        last = m // rows - 1
        if nxt:
            fn = lambda b, i: (jnp.minimum((b * nt + i + 1) * per, last), col_off // width)
        else:
            fn = lambda b, i: (jnp.maximum((b * nt + i) * per - 1, 0), col_off // width)
        return pl.BlockSpec((rows, width), fn)

    full = lambda shape: pl.BlockSpec(shape, lambda b, i: (0,) * len(shape),
                                      pipeline_mode=pl.Buffered(1))

    add("x", x2, pl.BlockSpec((tq, d), lambda b, i: (b * nt + i, 0)))
    add("gate", ada4, pl.BlockSpec((None, None, 1, d), lambda b, i: (layer, ada_row_fn(b), 0, 2)))
    add("glu", u2, rowblk(D_CONV, OFF_CONV_GLU))
    if local:
        add("glu_prev", u2, halo(CONV_HALO, D_CONV, OFF_CONV_GLU, False))
        add("glu_next", u2, halo(CONV_HALO, D_CONV, OFF_CONV_GLU, True))
    add("gates", u2, rowblk(D_CONV + D_LRU, OFF_CONV_GATE))
    add("ylru", ylru, pl.BlockSpec((tq, D_LRU), lambda b, i: (b * nt + i, 0)))
    add("q", q4, pl.BlockSpec((tq // WINDOW, N_Q_HEADS, WINDOW, HEAD_DIM),
                              lambda b, i: (b * nt + i, 0, 0, 0)))
    add("attn_gate", u2, rowblk(D_ATTN, U_ATTN_GATE))
    if local:
        add("kv", u2, rowblk(2 * D_KV, OFF_K))
        add("kv_prev", u2, halo(WINDOW, 2 * D_KV, OFF_K, False))
        add("kv_next", u2, halo(WINDOW, 2 * D_KV, OFF_K, True))
    add("kvc", uc2, pl.BlockSpec((ctx_rows, 2 * D_KV), lambda b, i: (b, OFF_K // (2 * D_KV))))
    for name in ("dw_w", "dw_b", "ln_g", "ln_b", "pw_w", "pw_b", "sink"):
        shape = wts[name].shape[1:]
        add(name, wts[name], pl.BlockSpec((None,) + shape,
                                          lambda b, i, nd=len(shape): (layer,) + (0,) * nd,
                                          pipeline_mode=pl.Buffered(1)))
    add("w_out", wts["w_out"], full(wts["w_out"].shape))
    if final:
        add("final_g", wts["final_g"], full(wts["final_g"].shape))
    names += ["out", "ext", "ycat"]
    return pl.pallas_call(
        functools.partial(_mix_kernel, names=tuple(names), tq=tq, local=local, final=final),
        grid=(nb, nt),
        in_specs=specs,
        out_specs=pl.BlockSpec((tq, d), lambda b, i: (b * nt + i, 0)),
        out_shape=jax.ShapeDtypeStruct((m, d), F32),
        scratch_shapes=[pltpu.VMEM((tq + 2 * CONV_HALO, D_CONV), F32),
                        pltpu.VMEM((tq, D_MODEL), BF16)],
        compiler_params=pltpu.CompilerParams(
            dimension_semantics=("arbitrary", "arbitrary"), vmem_limit_bytes=VMEM_LIMIT),
        name="mix_lat" if local else "mix_ctx",
    )(*args)


def _rope_tables(seq):
    rows = seq // GRID_W
    row = np.repeat(np.arange(rows, dtype=np.float64), GRID_W)
    col = np.tile(np.arange(GRID_W, dtype=np.float64), rows)
    half = HEAD_DIM // 2
    inv = ROPE_BASE ** (-np.arange(0, half, 2, dtype=np.float64) / half)
    ang_r = row[:, None] * inv[None, :]
    ang_c = col[:, None] * inv[None, :]
    ang = np.concatenate([ang_r, ang_r, ang_c, ang_c], axis=-1)
    cos, sin = np.cos(ang), np.sin(ang)
    first = (np.arange(HEAD_DIM) % half) < (half // 2)
    return tuple(jnp.asarray(t, F32)
                 for t in (cos, np.where(first, -sin, 0.0), np.where(first, 0.0, sin)))


def _block_diag(w):
    nblk, blk = w.shape[-3], w.shape[-2]
    n = nblk * blk
    rows = w.reshape(w.shape[:-3] + (n, blk))
    spread = jnp.asarray(np.arange(n)[None, :] % blk == np.arange(blk)[:, None], w.dtype)
    mask = jnp.asarray(np.arange(n)[:, None] // blk == np.arange(n)[None, :] // blk, w.dtype)
    return jnp.matmul(rows, spread, precision=lax.Precision.HIGHEST) * mask


def _pick_tile(n, pref):
    t = min(n, pref)
    while n % t:
        t //= 2
    return t


def kernel(x, c, ctx, c_ctx, norm_g, w_ada, b_ada, w_in, dw_w, dw_b, ln_g, ln_b, pw_w, pw_b,
           lru_conv_w, lru_conv_b, lru_w_r, lru_b_r, lru_w_i, lru_b_i, lru_lam, attn_sink,
           w_out, final_g):
    nb, seq, d = x.shape
    lc = ctx.shape[1]
    depth = w_in.shape[0]
    ctx_row = nb
    assert nb < ADA_ROWS

    c_rows = jnp.concatenate([c, c_ctx[None, :], jnp.zeros((ADA_ROWS - nb - 1, d), F32)], axis=0)
    ada = _ada_call(c_rows, w_ada, b_ada)
    ada4 = ada.reshape(depth, ADA_ROWS, 1, 3 * d)
    rope_tabs = _rope_tables(seq)
    norm_g3 = norm_g.reshape(depth, 1, d)
    vec4 = lambda p: p.reshape(depth, 2, 1, D_LRU)
    per_group = LANES // lru_w_r.shape[-1]
    grp = lambda w: _block_diag(w.reshape(w.shape[:2] + (-1, per_group) + w.shape[-2:]))
    wg_half = (0.5 * jnp.concatenate([grp(lru_w_r), grp(lru_w_i)], axis=-1)).astype(BF16)
    lru_params = (lru_conv_w, vec4(lru_conv_b), wg_half, vec4(lru_b_r), vec4(lru_b_i),
                  vec4(lru_lam))
    vec3 = lambda p: p.reshape(depth, 1, D_CONV)
    wts = {
        "dw_w": jnp.broadcast_to(dw_w[:, :, None, :], (depth, CONV_WIDTH, SUBLANES, D_CONV)),
        "dw_b": vec3(dw_b), "ln_g": vec3(ln_g), "ln_b": vec3(ln_b),
        "pw_w": pw_w.astype(BF16), "pw_b": vec3(pw_b),
        "sink": jnp.broadcast_to(attn_sink[:, :, None], (depth, N_Q_HEADS, HEAD_DIM)),
        "final_g": final_g.reshape(1, d),
    }

    x2 = x.reshape(nb * seq, d)
    xc2 = ctx.reshape(nb * lc, d)
    tm = _pick_tile(seq, 1024)
    tmc = _pick_tile(nb * lc, 1024)
    tq = _pick_tile(seq, 512)
    tt = _pick_tile(seq, 2048)
    tu = _pick_tile(tt, 512)
    zeros_h0 = jnp.zeros((nb, SUBLANES, D_LRU), F32)

    w_bf = w_in[0].astype(BF16)
    for l in range(depth):
        last = l == depth - 1
        side = [(w_out, l)] + ([] if last else [(w_in, l + 1)])
        u2, q4, casts = _inproj_call(x2, norm_g3, ada4, l, w_bf, rope_tabs, tm=tm,
                                     rows_per_batch=seq, ada_row_fn=lambda b: b, with_q=True,
                                     side_casts=side)
        uc2, qc4, _ = _inproj_call(xc2, norm_g3, ada4, l, w_bf, None, tm=tmc, rows_per_batch=tmc,
                                   ada_row_fn=lambda b: ctx_row, with_q=not last)
        wts["w_out"] = casts[0]
        if not last:
            w_bf = casts[1]

        ctx_kw = dict(rows_per_batch=lc, tt=lc, units=1)
        lat_kw = dict(rows_per_batch=seq, tt=tt, units=tt // tu)
        hp_c, xp_c, st_f = _lru_call(uc2, zeros_h0, None, lru_params, l, 0, **ctx_kw)
        ylru_c, st_r = _lru_call(None, zeros_h0, (xp_c, hp_c), lru_params, l, 1, **ctx_kw)
        hp, xp, _ = _lru_call(u2, st_f, None, lru_params, l, 0, **lat_kw)
        ylru, _ = _lru_call(None, st_r, (xp, hp), lru_params, l, 1, **lat_kw)

        x2_new = _mix_call(x2, u2, q4, uc2, ylru, ada4, l, lambda b: b, wts, tq=tq,
                           rows_per_batch=seq, ctx_rows=lc, local=True, final=last)
        if not last:
            xc2 = _mix_call(xc2, uc2, qc4, uc2, ylru_c, ada4, l, lambda b: ctx_row, wts, tq=lc,
                            rows_per_batch=lc, ctx_rows=lc, local=False, final=False)
        x2 = x2_new
    return x2.reshape(nb, seq, d)
```

```python
import functools
import math

import jax
import jax.numpy as jnp
import numpy as np
from jax import lax
from jax.experimental import pallas as pl
from jax.experimental.pallas import tpu as pltpu

F32 = jnp.float32
BF16 = jnp.bfloat16

D_MODEL = 2048
D_CONV = 512
D_LRU = 512
HEAD_DIM = 128
N_Q_HEADS = 8
N_KV_HEADS = 2
GQA_GROUP = N_Q_HEADS // N_KV_HEADS
D_ATTN = N_Q_HEADS * HEAD_DIM
D_KV = N_KV_HEADS * HEAD_DIM
GRID_W = 64
CONV_WIDTH = 31
CONV_PAD = (CONV_WIDTH - 1) // 2
LRU_CONV_WIDTH = 4
LRU_C = 8.0
WINDOW = 128
ROPE_BASE = 10000.0
EPS = 1e-6

OFF_K = D_LRU
OFF_V = OFF_K + D_KV
OFF_MEM_END = OFF_V + D_KV
OFF_CONV_GLU = OFF_MEM_END
OFF_CONV_GATE = OFF_CONV_GLU + 2 * D_CONV
OFF_LRU_GATE = OFF_CONV_GATE + D_CONV
OFF_Q = OFF_LRU_GATE + D_LRU
OFF_ATTN_GATE = OFF_Q + D_ATTN
D_IN = OFF_ATTN_GATE + D_ATTN
U_ATTN_GATE = OFF_Q
D_U = D_IN - D_ATTN

ADA_ROWS = 8
ADA_TILE_N = 1024
IN_TILE_N = 1024
IN_CHUNK_N = 256
Q_TILE = OFF_Q // IN_TILE_N
GLU_TILE = OFF_CONV_GLU // IN_TILE_N
OUT_CHUNK_N = 256
CONV_HALO = 16
CONV_ROWS = 64
CONV_COLS = 256
SUBLANES = 8
BF16_SUBLANES = 16
LANES = 128
VMEM_LIMIT = 58 * 1024 * 1024
F32_TINY = float(np.finfo(np.float32).tiny)
LOG2E = math.log2(math.e)
Q_SCALE = HEAD_DIM ** -0.5 * LOG2E

assert OFF_Q % IN_TILE_N == 0 and D_ATTN == IN_TILE_N and OFF_MEM_END == IN_TILE_N
assert OFF_CONV_GLU % IN_TILE_N == 0 and 2 * D_CONV == IN_TILE_N
assert CONV_HALO >= CONV_PAD and CONV_HALO % BF16_SUBLANES == 0


def _sigmoid(x):
    return 0.5 * jnp.tanh(0.5 * x) + 0.5


def _silu(x):
    hx = 0.5 * x
    return hx * jnp.tanh(hx) + hx


def _ada_kernel(c_ref, w_ref, b_ref, o_ref):
    ca = _silu(c_ref[...])
    o_ref[...] = jnp.dot(ca.astype(BF16), w_ref[...].astype(BF16),
                         preferred_element_type=F32) + b_ref[...]


def _ada_call(c_rows, w_ada, b_ada):
    depth, d, n = w_ada.shape
    tn = ADA_TILE_N
    return pl.pallas_call(
        _ada_kernel,
        grid=(depth, n // tn),
        in_specs=[
            pl.BlockSpec((ADA_ROWS, d), lambda l, j: (0, 0)),
            pl.BlockSpec((None, d, tn), lambda l, j: (l, 0, j)),
            pl.BlockSpec((None, 1, tn), lambda l, j: (l, 0, j)),
        ],
        out_specs=pl.BlockSpec((None, ADA_ROWS, tn), lambda l, j: (l, 0, j)),
        out_shape=jax.ShapeDtypeStruct((depth, ADA_ROWS, n), F32),
        compiler_params=pltpu.CompilerParams(
            dimension_semantics=("arbitrary", "arbitrary"), vmem_limit_bytes=VMEM_LIMIT),
        name="ada_proj",
    )(c_rows, w_ada, b_ada.reshape(depth, 1, n))


def _side_cast_specs(side_casts, n_steps, step_of):
    in_specs, args, out_specs, out_shapes = [], [], [], []
    for src, src_layer in side_casts:
        rows, cols = src.shape[1:]
        piece = BF16_SUBLANES
        while rows // piece > n_steps:
            piece *= 2
        n_pieces = rows // piece
        assert n_pieces * piece == rows and n_pieces <= n_steps
        piece_of = lambda *idx, n=n_pieces: jnp.minimum(step_of(*idx), n - 1)
        in_specs.append(pl.BlockSpec((None, piece, cols),
                                     lambda *idx, sl=src_layer, f=piece_of: (sl, f(*idx), 0)))
        args.append(src)
        out_specs.append(pl.BlockSpec((piece, cols), lambda *idx, f=piece_of: (f(*idx), 0)))
        out_shapes.append(jax.ShapeDtypeStruct((rows, cols), BF16))
    return in_specs, args, out_specs, out_shapes


def _rope(t, cos, sin_a, sin_b):
    quarter = HEAD_DIM // 4
    return (t * cos + pltpu.roll(t, HEAD_DIM - quarter, 1) * sin_a
            + pltpu.roll(t, quarter, 1) * sin_b)


def _inproj_kernel(*refs, rope, with_q, tm, n_side):
    refs = list(refs)
    x_ref, g_ref, shift_ref, scale_ref, w_ref = refs[:5]
    del refs[:5]
    if rope:
        cos_ref, sa_ref, sb_ref = refs[:3]
        del refs[:3]
    side_src = refs[:n_side]
    del refs[:n_side]
    o_ref = refs.pop(0)
    q_ref = refs.pop(0) if with_q else None
    side_dst = refs[:n_side]
    del refs[:n_side]
    h_ref = refs.pop(0)
    j = pl.program_id(1)

    for src, dst in zip(side_src, side_dst):
        dst[...] = src[...].astype(BF16)

    @pl.when(j == 0)
    def _():
        x = x_ref[...]
        ms = jnp.mean(x * x, axis=-1, keepdims=True)
        gain = g_ref[...] * (1.0 + scale_ref[...])
        h_ref[...] = (x * lax.rsqrt(ms + EPS) * gain + shift_ref[...]).astype(BF16)

    chunks = [slice(c0, c0 + IN_CHUNK_N) for c0 in range(0, IN_TILE_N, IN_CHUNK_N)]
    heads_per_chunk = IN_CHUNK_N // HEAD_DIM

    def proj(cols):
        return jnp.dot(h_ref[...], w_ref[:, cols], preferred_element_type=F32)

    def rope_heads(u):
        if not rope:
            return [u[:, hh * HEAD_DIM:(hh + 1) * HEAD_DIM] for hh in range(heads_per_chunk)]
        return [_rope(u[:, hh * HEAD_DIM:(hh + 1) * HEAD_DIM], cos_ref[...], sa_ref[...],
                      sb_ref[...]) for hh in range(heads_per_chunk)]

    def plain_tile():
        for cols in chunks:
            o_ref[:, cols] = proj(cols).astype(BF16)

    def q_tile():
        for ci, cols in enumerate(chunks):
            for hh, t in enumerate(rope_heads(proj(cols))):
                t = (t * Q_SCALE).astype(BF16)
                for qb in range(tm // WINDOW):
                    q_ref[qb, ci * heads_per_chunk + hh] = t[qb * WINDOW:(qb + 1) * WINDOW, :]

    def kv_tile():
        for cols in chunks:
            u = proj(cols)
            if OFF_K <= cols.start < OFF_V:
                u = jnp.concatenate(rope_heads(u), axis=1)
            o_ref[:, cols] = u.astype(BF16)

    def glu_tile():
        half = len(chunks) // 2
        for ci in range(half):
            o_ref[:, chunks[ci]] = (proj(chunks[ci]) * _sigmoid(proj(chunks[ci + half]))
                                    ).astype(BF16)
            o_ref[:, chunks[ci + half]] = jnp.zeros((tm, IN_CHUNK_N), BF16)

    def silu_tile():
        for cols in chunks:
            o_ref[:, cols] = _silu(proj(cols)).astype(BF16)

    if not with_q:
        plain_tile()
    else:
        pl.when(j == 0)(kv_tile if rope else plain_tile)
        pl.when(j == GLU_TILE)(glu_tile)
        pl.when(j == Q_TILE)(q_tile)
        pl.when((j != 0) & (j != GLU_TILE) & (j != Q_TILE))(silu_tile)


def _inproj_call(x2, norm_g, ada4, layer, w_bf, rope_tabs, *, tm, rows_per_batch, ada_row_fn,
                 with_q, side_casts=()):
    m, d = x2.shape
    tiles_per_batch = rows_per_batch // tm
    rope = rope_tabs is not None
    n_tiles = D_IN // IN_TILE_N if with_q else 1
    n_steps = (m // tm) * n_tiles

    def ada_spec(part):
        return pl.BlockSpec((None, None, 1, d),
                            lambda i, j: (layer, ada_row_fn(i // tiles_per_batch), 0, part))

    in_specs = [
        pl.BlockSpec((tm, d), lambda i, j: (i, 0)),
        pl.BlockSpec((None, 1, d), lambda i, j: (layer, 0, 0)),
        ada_spec(0), ada_spec(1),
        pl.BlockSpec((d, IN_TILE_N), lambda i, j: (0, j)),
    ]
    args = [x2, norm_g, ada4, ada4, w_bf]
    if rope:
        tab_spec = pl.BlockSpec((tm, HEAD_DIM), lambda i, j: (i % tiles_per_batch, 0))
        in_specs += [tab_spec] * 3
        args += list(rope_tabs)
    out_specs = [pl.BlockSpec((tm, IN_TILE_N), lambda i, j: (i, jnp.where(j > Q_TILE, j - 1, j)))]
    out_shape = [jax.ShapeDtypeStruct((m, D_U if with_q else IN_TILE_N), BF16)]
    if with_q:
        out_specs.append(pl.BlockSpec((tm // WINDOW, N_Q_HEADS, WINDOW, HEAD_DIM),
                                      lambda i, j: (i, 0, 0, 0)))
        out_shape.append(jax.ShapeDtypeStruct((m // WINDOW, N_Q_HEADS, WINDOW, HEAD_DIM), BF16))
    side = _side_cast_specs(side_casts, n_steps, lambda i, j: i * n_tiles + j)
    in_specs += side[0]
    args += side[1]
    out_specs += side[2]
    out_shape += side[3]
    outs = pl.pallas_call(
        functools.partial(_inproj_kernel, rope=rope, with_q=with_q, tm=tm,
                          n_side=len(side_casts)),
        grid=(m // tm, n_tiles),
        in_specs=in_specs,
        out_specs=out_specs,
        out_shape=out_shape,
        scratch_shapes=[pltpu.VMEM((tm, d), BF16)],
        compiler_params=pltpu.CompilerParams(
            dimension_semantics=("arbitrary", "arbitrary"), vmem_limit_bytes=VMEM_LIMIT),
        name="in_proj",
    )(*args)
    n_main = 2 if with_q else 1
    return outs[0], (outs[1] if with_q else None), list(outs[n_main:])


def _lru_kernel(*refs, reverse, tt, units):
    if reverse:
        (x_ref, h0_ref, acc_ref, pmat_ref, cw_ref, cb_ref, wg_ref, br_ref, bi_ref, lam_ref,
         o_ref, state_ref, ext3_ref, a3_ref, b3_ref, edge_ref, h_ref) = refs
        xp_ref = None
    else:
        (x_ref, h0_ref, pmat_ref, cw_ref, cb_ref, wg_ref, br_ref, bi_ref, lam_ref,
         o_ref, xp_ref, state_ref, ext3_ref, a3_ref, b3_ref, edge_ref, h_ref) = refs
        acc_ref = None
    c = pl.program_id(1)
    tu = tt // units
    seg = tu // SUBLANES
    hist = LRU_CONV_WIDTH - 1
    row = lax.broadcasted_iota(jnp.int32, (SUBLANES, D_LRU), 0)

    def blk(g):
        return slice(g * SUBLANES, (g + 1) * SUBLANES)

    @pl.when(c == 0)
    def _():
        edge_ref[...] = jnp.zeros((hist * SUBLANES, D_LRU), F32)
        h_ref[...] = h0_ref[...]

    lam = lam_ref[...]
    softplus_neg_lam = jnp.maximum(-lam, 0.0) + jnp.log1p(jnp.exp(-jnp.abs(lam)))
    half_k = (-0.5 * LRU_C) * softplus_neg_lam
    half_br = 0.5 * br_ref[...]
    half_bi = 0.5 * bi_ref[...]

    order = list(range(units - 1, -1, -1) if reverse else range(units))
    for u in order:
        _lru_gates(slice(u * tu, (u + 1) * tu), x_ref, xp_ref, pmat_ref, cw_ref, cb_ref, wg_ref,
                   half_br, half_bi, half_k, ext3_ref.at[u], a3_ref.at[u], b3_ref.at[u], edge_ref,
                   reverse=reverse, tt=tu, seg=seg, hist=hist, row=row, blk=blk)

    def step(t, carry):
        g_ = (seg - 1 - t) if reverse else t
        rows = pl.ds(pl.multiple_of(g_ * SUBLANES, SUBLANES), SUBLANES)
        new = []
        for u, (h, p) in zip(order, carry):
            a = a3_ref[u, rows, :]
            h = a * h + b3_ref[u, rows, :]
            p = a * p
            b3_ref[u, rows, :] = h
            a3_ref[u, rows, :] = p
            new.append((h, p))
        return tuple(new)

    init = tuple((jnp.zeros((SUBLANES, D_LRU), F32), jnp.ones((SUBLANES, D_LRU), F32))
                 for _ in order)
    finals = lax.fori_loop(0, seg, step, init, unroll=8)

    for u, (h_fin, p_fin) in zip(order, finals):
        _lru_finish(slice(u * tu, (u + 1) * tu), h_fin, p_fin, acc_ref, pmat_ref, o_ref,
                    a3_ref.at[u], b3_ref.at[u], h_ref, reverse=reverse, tt=tu, seg=seg, row=row)
    state_ref[...] = h_ref[...]


def _lru_gates(rows_u, x_ref, xp_ref, pmat_ref, cw_ref, cb_ref, wg_ref, half_br, half_bi, half_k,
               ext_ref, a_ref, b_ref, edge_ref, *, reverse, tt, seg, hist, row, blk):
    base = 0 if reverse else hist
    if reverse:
        xp = x_ref[rows_u, :].astype(F32)
    else:
        xp = jnp.dot(pmat_ref[...], x_ref[rows_u, :], preferred_element_type=F32)
        xp_ref[rows_u, :] = xp.astype(BF16)
    ext_ref[base * SUBLANES:base * SUBLANES + tt, :] = xp
    for j in range(1, hist + 1):
        if reverse:
            rolled = pltpu.roll(ext_ref[blk(j - 1), :], SUBLANES - 1, 0)
            ext_ref[blk(seg + j - 1), :] = jnp.where(row == SUBLANES - 1, edge_ref[blk(j - 1), :],
                                                     rolled)
        else:
            rolled = pltpu.roll(ext_ref[blk(hist + seg - j), :], 1, 0)
            ext_ref[blk(hist - j), :] = jnp.where(row == 0, edge_ref[blk(j - 1), :], rolled)
        edge_ref[blk(j - 1), :] = rolled

    xc = jnp.broadcast_to(cb_ref[...], (tt, D_LRU))
    for k in range(LRU_CONV_WIDTH):
        off = (hist - k) if reverse else k
        xc = xc + cw_ref[k:k + 1, :] * ext_ref[off * SUBLANES:off * SUBLANES + tt, :]

    xcb = xc.astype(BF16)
    g_r, g_i = [], []
    for j in range(D_LRU // LANES):
        gj = jnp.dot(xcb[:, j * LANES:(j + 1) * LANES], wg_ref[j], preferred_element_type=F32)
        g_r.append(gj[:, :LANES])
        g_i.append(gj[:, LANES:])
    t_r = jnp.tanh(jnp.concatenate(g_r, axis=1) + half_br)
    t_i = jnp.tanh(jnp.concatenate(g_i, axis=1) + half_bi)
    log_a = half_k * t_r + half_k
    a_ref[...] = jnp.exp(log_a)
    th = jnp.tanh(log_a)
    one_minus_a2 = (-2.0 * th) / (1.0 - th)
    root = one_minus_a2 * lax.rsqrt(jnp.maximum(one_minus_a2, F32_TINY))
    half_xc = 0.5 * xc
    b_ref[...] = root * (half_xc * t_i + half_xc)


def _lru_finish(rows_u, h_fin, p_fin, acc_ref, pmat_ref, o_ref, a_ref, b_ref, h_ref, *, reverse,
                tt, seg, row):
    a, b = p_fin, h_fin
    for d in (1, 2, 4):
        keep = (row < SUBLANES - d) if reverse else (row >= d)
        shift = (SUBLANES - d) if reverse else d
        a_s = jnp.where(keep, pltpu.roll(a, shift, 0), 1.0)
        b_s = jnp.where(keep, pltpu.roll(b, shift, 0), 0.0)
        b = a * b_s + b
        a = a * a_s
    h_in = h_ref[...]
    end = a * h_in + b
    if reverse:
        carry_in = jnp.where(row < SUBLANES - 1, pltpu.roll(end, SUBLANES - 1, 0), h_in)
        h_ref[...] = jnp.broadcast_to(end[0:1, :], (SUBLANES, D_LRU))
    else:
        carry_in = jnp.where(row >= 1, pltpu.roll(end, 1, 0), h_in)
        h_ref[...] = jnp.broadcast_to(end[SUBLANES - 1:SUBLANES, :], (SUBLANES, D_LRU))

    h_all = (b_ref[...].reshape(seg, SUBLANES, D_LRU)
             + a_ref[...].reshape(seg, SUBLANES, D_LRU) * carry_in[None]).reshape(tt, D_LRU)
    if reverse:
        h_sum = (h_all + acc_ref[rows_u, :].astype(F32)).astype(BF16)
        o_ref[rows_u, :] = jnp.dot(pmat_ref[...], h_sum, preferred_element_type=F32
                                   ).astype(o_ref.dtype)
    else:
        o_ref[rows_u, :] = h_all.astype(o_ref.dtype)


def _lru_call(x_rows, h0, fwd, params, layer, direction, *, rows_per_batch, tt, units):
    cw, cb, wg_half, br, bi, lam = params
    reverse = direction == 1
    m = x_rows.shape[0] if not reverse else fwd[0].shape[0]
    nb = m // rows_per_batch
    nc = rows_per_batch // tt
    tu = tt // units

    def chunk(c):
        return (nc - 1 - c) if reverse else c

    row_spec = pl.BlockSpec((tt, D_LRU), lambda b, c: (b * nc + chunk(c), 0))
    full = lambda shape: pl.BlockSpec(shape, lambda b, c: (0,) * len(shape))
    h0_spec = pl.BlockSpec((None, SUBLANES, D_LRU), lambda b, c: (b, 0, 0))
    seg = tu // SUBLANES
    src = (np.arange(tu) % SUBLANES) * seg + np.arange(tu) // SUBLANES
    perm = src[:, None] == np.arange(tu)[None, :]
    if reverse:
        in_specs = [row_spec, h0_spec, row_spec]
        args = [fwd[0], h0, fwd[1], jnp.asarray(perm.T, BF16)]
    else:
        in_specs = [row_spec, h0_spec]
        args = [x_rows, h0, jnp.asarray(perm, BF16)]
    sel = lambda shape: pl.BlockSpec((None, None) + shape,
                                     lambda b, c: (layer, direction) + (0,) * len(shape))
    in_specs += [full((tu, tu)),
                 sel((LRU_CONV_WIDTH, D_LRU)), sel((1, D_LRU)), sel(wg_half.shape[2:]),
                 sel((1, D_LRU)), sel((1, D_LRU)), sel((1, D_LRU))]
    args += [cw, cb, wg_half, br, bi, lam]
    edge_rows = (LRU_CONV_WIDTH - 1) * SUBLANES
    rows_out = jax.ShapeDtypeStruct((m, D_LRU), BF16)
    state_spec = pl.BlockSpec((None, SUBLANES, D_LRU), lambda b, c: (b, 0, 0))
    state_out = jax.ShapeDtypeStruct((nb, SUBLANES, D_LRU), F32)
    return pl.pallas_call(
        functools.partial(_lru_kernel, reverse=reverse, tt=tt, units=units),
        grid=(nb, nc),
        in_specs=in_specs,
        out_specs=[row_spec, state_spec] if reverse else [row_spec, row_spec, state_spec],
        out_shape=[rows_out, state_out] if reverse else [rows_out, rows_out, state_out],
        scratch_shapes=[pltpu.VMEM((units, tu + edge_rows, D_LRU), F32),
                        pltpu.VMEM((units, tu, D_LRU), F32),
                        pltpu.VMEM((units, tu, D_LRU), F32),
                        pltpu.VMEM((edge_rows, D_LRU), F32),
                        pltpu.VMEM((SUBLANES, D_LRU), F32)],
        compiler_params=pltpu.CompilerParams(
            dimension_semantics=("arbitrary", "arbitrary"), vmem_limit_bytes=VMEM_LIMIT),
        name="rglru_rev" if reverse else "rglru_fwd",
    )(*args)


def _mix_kernel(*refs, names, tq, local, final, n_side):
    r = dict(zip(names, refs))
    i = pl.program_id(1)
    nt = pl.num_programs(1)
    ext_ref, ycat_ref = r["ext"], r["ycat"]
    for k in range(n_side):
        r[f"side_dst{k}"][...] = r[f"side_src{k}"][...].astype(BF16)

    def glu(ref):
        return ref[...].astype(F32)

    ext_ref[CONV_HALO:CONV_HALO + tq, :] = glu(r["glu"])
    zero_halo = jnp.zeros((CONV_HALO, D_CONV), F32)
    if local:
        ext_ref[0:CONV_HALO, :] = jnp.where(i > 0, glu(r["glu_prev"]), zero_halo)
        ext_ref[CONV_HALO + tq:, :] = jnp.where(i < nt - 1, glu(r["glu_next"]), zero_halo)
    else:
        ext_ref[0:CONV_HALO, :] = zero_halo
        ext_ref[CONV_HALO + tq:, :] = zero_halo

    for rc in range(tq // CONV_ROWS):
        t0 = rc * CONV_ROWS
        rows = slice(t0, t0 + CONV_ROWS)
        pieces = []
        for cc in range(D_CONV // CONV_COLS):
            cols = slice(cc * CONV_COLS, (cc + 1) * CONV_COLS)
            acc = jnp.broadcast_to(r["dw_b"][:, cols], (CONV_ROWS, CONV_COLS))
            for b in range(SUBLANES):
                z = None
                for a in range(-(-(CONV_WIDTH + 1) // SUBLANES)):
                    o = SUBLANES * a + b
                    if o < 1 or o > CONV_WIDTH:
                        continue
                    xs = ext_ref[t0 + SUBLANES * a:t0 + SUBLANES * a + CONV_ROWS + SUBLANES, cols]
                    term = (xs.reshape(-1, SUBLANES, CONV_COLS) * r["dw_w"][o - 1, :, cols][None]
                            ).reshape(CONV_ROWS + SUBLANES, CONV_COLS)
                    z = term if z is None else z + term
                acc = acc + z[b:b + CONV_ROWS, :]
            pieces.append(acc)
        acc = jnp.concatenate(pieces, axis=1)
        mu = jnp.mean(acc, axis=-1, keepdims=True)
        cen = acc - mu
        var = jnp.mean(cen * cen, axis=-1, keepdims=True)
        y = cen * lax.rsqrt(var + EPS) * r["ln_g"][...] + r["ln_b"][...]
        y = _silu(y).astype(BF16)
        y = jnp.dot(y, r["pw_w"][...], preferred_element_type=F32) + r["pw_b"][...]
        y = y * r["gates"][rows, :D_CONV].astype(F32)
        ycat_ref[rows, 0:D_CONV] = y.astype(BF16)

    ycat_ref[:, D_CONV:D_CONV + D_LRU] = (
        r["ylru"][...].astype(F32) * r["gates"][:, D_CONV:].astype(F32)).astype(BF16)

    qrows = GQA_GROUP * WINDOW
    qi = lax.broadcasted_iota(jnp.int32, (qrows, WINDOW), 0) % WINDOW
    kj = lax.broadcasted_iota(jnp.int32, (qrows, WINDOW), 1)
    neg_inf = jnp.float32(-jnp.inf)
    dn = (((1,), (1,)), ((), ()))
    n_qb = tq // WINDOW
    lc = r["kvc"].shape[0]
    for qb in range(n_qb):
        rows = slice(qb * WINDOW, (qb + 1) * WINDOW)
        for g in range(N_KV_HEADS):
            k_cols = slice(g * HEAD_DIM, (g + 1) * HEAD_DIM)
            v_cols = slice(D_KV + g * HEAD_DIM, D_KV + (g + 1) * HEAD_DIM)
            q_st = r["q"][qb, g * GQA_GROUP:(g + 1) * GQA_GROUP].reshape(qrows, HEAD_DIM)
            sink = jnp.concatenate(
                [jnp.broadcast_to(r["sink"][g * GQA_GROUP + hh:g * GQA_GROUP + hh + 1, 0:1],
                                  (WINDOW, 1)) for hh in range(GQA_GROUP)], axis=0) * LOG2E
            blocks = []
            for cb in range(lc // WINDOW):
                crow = slice(cb * WINDOW, (cb + 1) * WINDOW)
                s = lax.dot_general(q_st, r["kvc"][crow, k_cols], dn, preferred_element_type=F32)
                blocks.append((s, r["kvc"][crow, v_cols]))
            if local:
                if qb == 0:
                    k_p, v_p = r["kv_prev"][:, k_cols], r["kv_prev"][:, v_cols]
                    ok_p = i > 0
                else:
                    prows = slice((qb - 1) * WINDOW, qb * WINDOW)
                    k_p, v_p = r["kv"][prows, k_cols], r["kv"][prows, v_cols]
                    ok_p = True
                if qb == n_qb - 1:
                    k_n, v_n = r["kv_next"][:, k_cols], r["kv_next"][:, v_cols]
                    ok_n = i < nt - 1
                else:
                    nrows = slice((qb + 1) * WINDOW, (qb + 2) * WINDOW)
                    k_n, v_n = r["kv"][nrows, k_cols], r["kv"][nrows, v_cols]
                    ok_n = True
                s_p = lax.dot_general(q_st, k_p, dn, preferred_element_type=F32)
                s_c = lax.dot_general(q_st, r["kv"][rows, k_cols], dn, preferred_element_type=F32)
                s_n = lax.dot_general(q_st, k_n, dn, preferred_element_type=F32)
                s_p = jnp.where((kj >= qi) & ok_p, s_p, neg_inf)
                s_n = jnp.where((kj <= qi) & ok_n, s_n, neg_inf)
                blocks += [(s_p, v_p), (s_c, r["kv"][rows, v_cols]), (s_n, v_n)]
            m_el = blocks[0][0]
            for s, _ in blocks[1:]:
                m_el = jnp.maximum(m_el, s)
            m = jnp.maximum(jnp.max(m_el, axis=-1, keepdims=True), sink)
            den_el = None
            o = None
            for s, v in blocks:
                p = jnp.exp2(s - m)
                den_el = p if den_el is None else den_el + p
                pv = jnp.dot(p.astype(BF16), v, preferred_element_type=F32)
                o = pv if o is None else o + pv
            den = jnp.sum(den_el, axis=-1, keepdims=True) + jnp.exp2(sink - m)
            o = o / den
            for hh in range(GQA_GROUP):
                c0 = (g * GQA_GROUP + hh) * HEAD_DIM
                gate = r["attn_gate"][rows, c0:c0 + HEAD_DIM].astype(F32)
                ycat_ref[rows, D_CONV + D_LRU + c0:D_CONV + D_LRU + c0 + HEAD_DIM] = (
                    o[hh * WINDOW:(hh + 1) * WINDOW, :] * gate).astype(BF16)

    out_ref = r["out"]
    for c0 in range(0, out_ref.shape[1], OUT_CHUNK_N):
        cols = slice(c0, c0 + OUT_CHUNK_N)
        y = jnp.dot(ycat_ref[...], r["w_out"][:, cols], preferred_element_type=F32)
        out_ref[:, cols] = r["x"][:, cols] + r["gate"][:, cols] * y
    if final:
        xn = out_ref[...]
        ms = jnp.mean(xn * xn, axis=-1, keepdims=True)
        out_ref[...] = xn * lax.rsqrt(ms + EPS) * r["final_g"][...]


def _mix_call(x2, u2, q4, uc2, ylru, ada4, layer, ada_row_fn, wts, *, tq, rows_per_batch,
              ctx_rows, local, final, side_casts=()):
    m, d = x2.shape
    nt = rows_per_batch // tq
    nb = m // rows_per_batch
    names, specs, args = [], [], []

    def add(name, arr, spec):
        names.append(name)
        specs.append(spec)
        args.append(arr)

    def rowblk(width, col_off):
        return pl.BlockSpec((tq, width), lambda b, i: (b * nt + i, col_off // width))

    def halo(rows, width, col_off, nxt):
        per = tq // rows
        last = m // rows - 1
        if nxt:
            fn = lambda b, i: (jnp.minimum((b * nt + i + 1) * per, last), col_off // width)
        else:
            fn = lambda b, i: (jnp.maximum((b * nt + i) * per - 1, 0), col_off // width)
        return pl.BlockSpec((rows, width), fn)

    full = lambda shape: pl.BlockSpec(shape, lambda b, i: (0,) * len(shape),
                                      pipeline_mode=pl.Buffered(1))

    add("x", x2, pl.BlockSpec((tq, d), lambda b, i: (b * nt + i, 0)))
    add("gate", ada4, pl.BlockSpec((None, None, 1, d), lambda b, i: (layer, ada_row_fn(b), 0, 2)))
    add("glu", u2, rowblk(D_CONV, OFF_CONV_GLU))
    if local:
        add("glu_prev", u2, halo(CONV_HALO, D_CONV, OFF_CONV_GLU, False))
        add("glu_next", u2, halo(CONV_HALO, D_CONV, OFF_CONV_GLU, True))
    add("gates", u2, rowblk(D_CONV + D_LRU, OFF_CONV_GATE))
    add("ylru", ylru, pl.BlockSpec((tq, D_LRU), lambda b, i: (b * nt + i, 0)))
    add("q", q4, pl.BlockSpec((tq // WINDOW, N_Q_HEADS, WINDOW, HEAD_DIM),
                              lambda b, i: (b * nt + i, 0, 0, 0)))
    add("attn_gate", u2, rowblk(D_ATTN, U_ATTN_GATE))
    if local:
        add("kv", u2, rowblk(2 * D_KV, OFF_K))
        add("kv_prev", u2, halo(WINDOW, 2 * D_KV, OFF_K, False))
        add("kv_next", u2, halo(WINDOW, 2 * D_KV, OFF_K, True))
    add("kvc", uc2, pl.BlockSpec((ctx_rows, 2 * D_KV), lambda b, i: (b, OFF_K // (2 * D_KV))))
    for name in ("dw_w", "dw_b", "ln_g", "ln_b", "pw_w", "pw_b", "sink"):
        shape = wts[name].shape[1:]
        add(name, wts[name], pl.BlockSpec((None,) + shape,
                                          lambda b, i, nd=len(shape): (layer,) + (0,) * nd,
                                          pipeline_mode=pl.Buffered(1)))
    add("w_out", wts["w_out"], full(wts["w_out"].shape))
    if final:
        add("final_g", wts["final_g"], full(wts["final_g"].shape))
    side = _side_cast_specs(side_casts, nb * nt, lambda b, i: b * nt + i)
    for k, (spec, arr) in enumerate(zip(side[0], side[1])):
        add(f"side_src{k}", arr, spec)
    names += ["out"] + [f"side_dst{k}" for k in range(len(side_casts))] + ["ext", "ycat"]
    outs = pl.pallas_call(
        functools.partial(_mix_kernel, names=tuple(names), tq=tq, local=local, final=final,
                          n_side=len(side_casts)),
        grid=(nb, nt),
        in_specs=specs,
        out_specs=[pl.BlockSpec((tq, d), lambda b, i: (b * nt + i, 0))] + side[2],
        out_shape=[jax.ShapeDtypeStruct((m, d), F32)] + side[3],
        scratch_shapes=[pltpu.VMEM((tq + 2 * CONV_HALO, D_CONV), F32),
                        pltpu.VMEM((tq, D_MODEL), BF16)],
        compiler_params=pltpu.CompilerParams(
            dimension_semantics=("arbitrary", "arbitrary"), vmem_limit_bytes=VMEM_LIMIT),
        name="mix_lat" if local else "mix_ctx",
    )(*args)
    return outs[0], list(outs[1:])


def _rope_tables(seq):
    rows = seq // GRID_W
    row = np.repeat(np.arange(rows, dtype=np.float64), GRID_W)
    col = np.tile(np.arange(GRID_W, dtype=np.float64), rows)
    half = HEAD_DIM // 2
    inv = ROPE_BASE ** (-np.arange(0, half, 2, dtype=np.float64) / half)
    ang_r = row[:, None] * inv[None, :]
    ang_c = col[:, None] * inv[None, :]
    ang = np.concatenate([ang_r, ang_r, ang_c, ang_c], axis=-1)
    cos, sin = np.cos(ang), np.sin(ang)
    first = (np.arange(HEAD_DIM) % half) < (half // 2)
    return tuple(jnp.asarray(t, F32)
                 for t in (cos, np.where(first, -sin, 0.0), np.where(first, 0.0, sin)))


def _block_diag(w):
    nblk, blk = w.shape[-3], w.shape[-2]
    n = nblk * blk
    rows = w.reshape(w.shape[:-3] + (n, blk))
    spread = jnp.asarray(np.arange(n)[None, :] % blk == np.arange(blk)[:, None], w.dtype)
    mask = jnp.asarray(np.arange(n)[:, None] // blk == np.arange(n)[None, :] // blk, w.dtype)
    return jnp.matmul(rows, spread, precision=lax.Precision.HIGHEST) * mask


def _pick_tile(n, pref):
    t = min(n, pref)
    while n % t:
        t //= 2
    return t


def kernel(x, c, ctx, c_ctx, norm_g, w_ada, b_ada, w_in, dw_w, dw_b, ln_g, ln_b, pw_w, pw_b,
           lru_conv_w, lru_conv_b, lru_w_r, lru_b_r, lru_w_i, lru_b_i, lru_lam, attn_sink,
           w_out, final_g):
    nb, seq, d = x.shape
    lc = ctx.shape[1]
    depth = w_in.shape[0]
    ctx_row = nb
    assert nb < ADA_ROWS

    c_rows = jnp.concatenate([c, c_ctx[None, :], jnp.zeros((ADA_ROWS - nb - 1, d), F32)], axis=0)
    ada = _ada_call(c_rows, w_ada, b_ada)
    ada4 = ada.reshape(depth, ADA_ROWS, 1, 3 * d)
    rope_tabs = _rope_tables(seq)
    norm_g3 = norm_g.reshape(depth, 1, d)
    vec4 = lambda p: p.reshape(depth, 2, 1, D_LRU)
    per_group = LANES // lru_w_r.shape[-1]
    grp = lambda w: _block_diag(w.reshape(w.shape[:2] + (-1, per_group) + w.shape[-2:]))
    wg_half = (0.5 * jnp.concatenate([grp(lru_w_r), grp(lru_w_i)], axis=-1)).astype(BF16)
    lru_params = (lru_conv_w, vec4(lru_conv_b), wg_half, vec4(lru_b_r), vec4(lru_b_i),
                  vec4(lru_lam))
    vec3 = lambda p: p.reshape(depth, 1, D_CONV)
    wts = {
        "dw_w": jnp.broadcast_to(dw_w[:, :, None, :], (depth, CONV_WIDTH, SUBLANES, D_CONV)),
        "dw_b": vec3(dw_b), "ln_g": vec3(ln_g), "ln_b": vec3(ln_b),
        "pw_w": pw_w.astype(BF16), "pw_b": vec3(pw_b),
        "sink": jnp.broadcast_to(attn_sink[:, :, None], (depth, N_Q_HEADS, HEAD_DIM)),
        "final_g": final_g.reshape(1, d),
    }

    x2 = x.reshape(nb * seq, d)
    xc2 = ctx.reshape(nb * lc, d)
    tm = _pick_tile(seq, 1024)
    tmc = _pick_tile(nb * lc, 1024)
    tq = _pick_tile(seq, 512)
    tt = _pick_tile(seq, 2048)
    tu = _pick_tile(tt, 512)
    zeros_h0 = jnp.zeros((nb, SUBLANES, D_LRU), F32)

    w_bf = w_in[0].astype(BF16)
    for l in range(depth):
        last = l == depth - 1
        u2, q4, casts = _inproj_call(x2, norm_g3, ada4, l, w_bf, rope_tabs, tm=tm,
                                     rows_per_batch=seq, ada_row_fn=lambda b: b, with_q=True,
                                     side_casts=[(w_out, l)])
        uc2, qc4, _ = _inproj_call(xc2, norm_g3, ada4, l, w_bf, None, tm=tmc, rows_per_batch=tmc,
                                   ada_row_fn=lambda b: ctx_row, with_q=not last)
        wts["w_out"] = casts[0]

        ctx_kw = dict(rows_per_batch=lc, tt=lc, units=1)
        lat_kw = dict(rows_per_batch=seq, tt=tt, units=tt // tu)
        hp_c, xp_c, st_f = _lru_call(uc2, zeros_h0, None, lru_params, l, 0, **ctx_kw)
        ylru_c, st_r = _lru_call(None, zeros_h0, (xp_c, hp_c), lru_params, l, 1, **ctx_kw)
        hp, xp, _ = _lru_call(u2, st_f, None, lru_params, l, 0, **lat_kw)
        ylru, _ = _lru_call(None, st_r, (xp, hp), lru_params, l, 1, **lat_kw)

        x2_new, casts = _mix_call(x2, u2, q4, uc2, ylru, ada4, l, lambda b: b, wts, tq=tq,
                                  rows_per_batch=seq, ctx_rows=lc, local=True, final=last,
                                  side_casts=[] if last else [(w_in, l + 1)])
        if not last:
            xc2, _ = _mix_call(xc2, uc2, qc4, uc2, ylru_c, ada4, l, lambda b: ctx_row, wts, tq=lc,
                               rows_per_batch=lc, ctx_rows=lc, local=False, final=False)
            w_bf = casts[0]
        x2 = x2_new
    return x2.reshape(nb, seq, d)
```

```python
import functools
import math

import jax
import jax.numpy as jnp
import numpy as np
from jax import lax
from jax.experimental import pallas as pl
from jax.experimental.pallas import tpu as pltpu

F32 = jnp.float32
BF16 = jnp.bfloat16

D_MODEL = 2048
D_CONV = 512
D_LRU = 512
HEAD_DIM = 128
N_Q_HEADS = 8
N_KV_HEADS = 2
GQA_GROUP = N_Q_HEADS // N_KV_HEADS
D_ATTN = N_Q_HEADS * HEAD_DIM
D_KV = N_KV_HEADS * HEAD_DIM
GRID_W = 64
CONV_WIDTH = 31
CONV_PAD = (CONV_WIDTH - 1) // 2
LRU_CONV_WIDTH = 4
LRU_C = 8.0
WINDOW = 128
ROPE_BASE = 10000.0
EPS = 1e-6

OFF_K = D_LRU
OFF_V = OFF_K + D_KV
OFF_MEM_END = OFF_V + D_KV
OFF_CONV_GLU = OFF_MEM_END
OFF_CONV_GATE = OFF_CONV_GLU + 2 * D_CONV
OFF_LRU_GATE = OFF_CONV_GATE + D_CONV
OFF_Q = OFF_LRU_GATE + D_LRU
OFF_ATTN_GATE = OFF_Q + D_ATTN
D_IN = OFF_ATTN_GATE + D_ATTN
U_ATTN_GATE = OFF_Q
D_U = D_IN - D_ATTN

ADA_ROWS = 8
ADA_TILE_N = 1024
IN_TILE_N = 1024
IN_CHUNK_N = 256
Q_TILE = OFF_Q // IN_TILE_N
GLU_TILE = OFF_CONV_GLU // IN_TILE_N
OUT_CHUNK_N = 256
CONV_HALO = 16
CONV_ROWS = 64
CONV_COLS = 256
SUBLANES = 8
BF16_SUBLANES = 16
LANES = 128
VMEM_LIMIT = 58 * 1024 * 1024
F32_TINY = float(np.finfo(np.float32).tiny)
LOG2E = math.log2(math.e)
Q_SCALE = HEAD_DIM ** -0.5 * LOG2E

assert OFF_Q % IN_TILE_N == 0 and D_ATTN == IN_TILE_N and OFF_MEM_END == IN_TILE_N
assert OFF_CONV_GLU % IN_TILE_N == 0 and 2 * D_CONV == IN_TILE_N
assert CONV_HALO >= CONV_PAD and CONV_HALO % BF16_SUBLANES == 0


def _sigmoid(x):
    return 0.5 * jnp.tanh(0.5 * x) + 0.5


def _silu(x):
    hx = 0.5 * x
    return hx * jnp.tanh(hx) + hx


def _ada_kernel(*refs, n_side):
    c_ref, w_ref, b_ref = refs[:3]
    side_src = refs[3:3 + n_side]
    o_ref = refs[3 + n_side]
    side_dst = refs[4 + n_side:]
    for src, dst in zip(side_src, side_dst):
        dst[...] = src[...].astype(BF16)
    ca = _silu(c_ref[...])
    o_ref[...] = jnp.dot(ca.astype(BF16), w_ref[...].astype(BF16),
                         preferred_element_type=F32) + b_ref[...]


def _ada_call(c_rows, w_ada, b_ada, side_casts=()):
    depth, d, n = w_ada.shape
    tn = ADA_TILE_N
    n_tiles = n // tn
    side = _side_cast_specs(side_casts, depth * n_tiles, lambda l, j: l * n_tiles + j)
    outs = pl.pallas_call(
        functools.partial(_ada_kernel, n_side=len(side_casts)),
        grid=(depth, n_tiles),
        in_specs=[
            pl.BlockSpec((ADA_ROWS, d), lambda l, j: (0, 0)),
            pl.BlockSpec((None, d, tn), lambda l, j: (l, 0, j)),
            pl.BlockSpec((None, 1, tn), lambda l, j: (l, 0, j)),
        ] + side[0],
        out_specs=[pl.BlockSpec((None, ADA_ROWS, tn), lambda l, j: (l, 0, j))] + side[2],
        out_shape=[jax.ShapeDtypeStruct((depth, ADA_ROWS, n), F32)] + side[3],
        compiler_params=pltpu.CompilerParams(
            dimension_semantics=("arbitrary", "arbitrary"), vmem_limit_bytes=VMEM_LIMIT),
        name="ada_proj",
    )(c_rows, w_ada, b_ada.reshape(depth, 1, n), *side[1])
    return outs[0], list(outs[1:])


def _side_cast_specs(side_casts, n_steps, step_of):
    in_specs, args, out_specs, out_shapes = [], [], [], []
    for src, src_layer in side_casts:
        rows, cols = src.shape[1:]
        piece = BF16_SUBLANES
        while rows // piece > n_steps:
            piece *= 2
        n_pieces = rows // piece
        assert n_pieces * piece == rows and n_pieces <= n_steps
        piece_of = lambda *idx, n=n_pieces: jnp.minimum(step_of(*idx), n - 1)
        in_specs.append(pl.BlockSpec((None, piece, cols),
                                     lambda *idx, sl=src_layer, f=piece_of: (sl, f(*idx), 0)))
        args.append(src)
        out_specs.append(pl.BlockSpec((piece, cols), lambda *idx, f=piece_of: (f(*idx), 0)))
        out_shapes.append(jax.ShapeDtypeStruct((rows, cols), BF16))
    return in_specs, args, out_specs, out_shapes


def _rope(t, cos, sin_a, sin_b):
    quarter = HEAD_DIM // 4
    return (t * cos + pltpu.roll(t, HEAD_DIM - quarter, 1) * sin_a
            + pltpu.roll(t, quarter, 1) * sin_b)


def _inproj_kernel(*refs, rope, with_q, tm, n_side):
    refs = list(refs)
    x_ref, g_ref, shift_ref, scale_ref, w_ref = refs[:5]
    del refs[:5]
    if rope:
        cos_ref, sa_ref, sb_ref = refs[:3]
        del refs[:3]
    side_src = refs[:n_side]
    del refs[:n_side]
    o_ref = refs.pop(0)
    q_ref = refs.pop(0) if with_q else None
    side_dst = refs[:n_side]
    del refs[:n_side]
    h_ref = refs.pop(0)
    j = pl.program_id(1)

    for src, dst in zip(side_src, side_dst):
        dst[...] = src[...].astype(BF16)

    @pl.when(j == 0)
    def _():
        x = x_ref[...]
        ms = jnp.mean(x * x, axis=-1, keepdims=True)
        gain = g_ref[...] * (1.0 + scale_ref[...])
        h_ref[...] = (x * lax.rsqrt(ms + EPS) * gain + shift_ref[...]).astype(BF16)

    chunks = [slice(c0, c0 + IN_CHUNK_N) for c0 in range(0, IN_TILE_N, IN_CHUNK_N)]
    heads_per_chunk = IN_CHUNK_N // HEAD_DIM

    def proj(cols):
        return jnp.dot(h_ref[...], w_ref[:, cols], preferred_element_type=F32)

    def rope_heads(u):
        if not rope:
            return [u[:, hh * HEAD_DIM:(hh + 1) * HEAD_DIM] for hh in range(heads_per_chunk)]
        return [_rope(u[:, hh * HEAD_DIM:(hh + 1) * HEAD_DIM], cos_ref[...], sa_ref[...],
                      sb_ref[...]) for hh in range(heads_per_chunk)]

    def plain_tile():
        for cols in chunks:
            o_ref[:, cols] = proj(cols).astype(BF16)

    def q_tile():
        for ci, cols in enumerate(chunks):
            for hh, t in enumerate(rope_heads(proj(cols))):
                t = (t * Q_SCALE).astype(BF16)
                for qb in range(tm // WINDOW):
                    q_ref[qb, ci * heads_per_chunk + hh] = t[qb * WINDOW:(qb + 1) * WINDOW, :]

    def kv_tile():
        for cols in chunks:
            u = proj(cols)
            if OFF_K <= cols.start < OFF_V:
                u = jnp.concatenate(rope_heads(u), axis=1)
            o_ref[:, cols] = u.astype(BF16)

    def glu_tile():
        half = len(chunks) // 2
        for ci in range(half):
            o_ref[:, chunks[ci]] = (proj(chunks[ci]) * _sigmoid(proj(chunks[ci + half]))
                                    ).astype(BF16)
            o_ref[:, chunks[ci + half]] = jnp.zeros((tm, IN_CHUNK_N), BF16)

    def silu_tile():
        for cols in chunks:
            o_ref[:, cols] = _silu(proj(cols)).astype(BF16)

    if not with_q:
        plain_tile()
    else:
        pl.when(j == 0)(kv_tile if rope else plain_tile)
        pl.when(j == GLU_TILE)(glu_tile)
        pl.when(j == Q_TILE)(q_tile)
        pl.when((j != 0) & (j != GLU_TILE) & (j != Q_TILE))(silu_tile)


def _inproj_call(x2, norm_g, ada4, layer, w_bf, rope_tabs, *, tm, rows_per_batch, ada_row_fn,
                 with_q, side_casts=()):
    m, d = x2.shape
    tiles_per_batch = rows_per_batch // tm
    rope = rope_tabs is not None
    n_tiles = D_IN // IN_TILE_N if with_q else 1
    n_steps = (m // tm) * n_tiles

    def ada_spec(part):
        return pl.BlockSpec((None, None, 1, d),
                            lambda i, j: (layer, ada_row_fn(i // tiles_per_batch), 0, part))

    in_specs = [
        pl.BlockSpec((tm, d), lambda i, j: (i, 0)),
        pl.BlockSpec((None, 1, d), lambda i, j: (layer, 0, 0)),
        ada_spec(0), ada_spec(1),
        pl.BlockSpec((d, IN_TILE_N), lambda i, j: (0, j)),
    ]
    args = [x2, norm_g, ada4, ada4, w_bf]
    if rope:
        tab_spec = pl.BlockSpec((tm, HEAD_DIM), lambda i, j: (i % tiles_per_batch, 0))
        in_specs += [tab_spec] * 3
        args += list(rope_tabs)
    out_specs = [pl.BlockSpec((tm, IN_TILE_N), lambda i, j: (i, jnp.where(j > Q_TILE, j - 1, j)))]
    out_shape = [jax.ShapeDtypeStruct((m, D_U if with_q else IN_TILE_N), BF16)]
    if with_q:
        out_specs.append(pl.BlockSpec((tm // WINDOW, N_Q_HEADS, WINDOW, HEAD_DIM),
                                      lambda i, j: (i, 0, 0, 0)))
        out_shape.append(jax.ShapeDtypeStruct((m // WINDOW, N_Q_HEADS, WINDOW, HEAD_DIM), BF16))
    side = _side_cast_specs(side_casts, n_steps, lambda i, j: i * n_tiles + j)
    in_specs += side[0]
    args += side[1]
    out_specs += side[2]
    out_shape += side[3]
    outs = pl.pallas_call(
        functools.partial(_inproj_kernel, rope=rope, with_q=with_q, tm=tm,
                          n_side=len(side_casts)),
        grid=(m // tm, n_tiles),
        in_specs=in_specs,
        out_specs=out_specs,
        out_shape=out_shape,
        scratch_shapes=[pltpu.VMEM((tm, d), BF16)],
        compiler_params=pltpu.CompilerParams(
            dimension_semantics=("arbitrary", "arbitrary"), vmem_limit_bytes=VMEM_LIMIT),
        name="in_proj",
    )(*args)
    n_main = 2 if with_q else 1
    return outs[0], (outs[1] if with_q else None), list(outs[n_main:])


def _lru_kernel(*refs, reverse, tt, units):
    if reverse:
        (x_ref, h0_ref, acc_ref, pmat_ref, cw_ref, cb_ref, wg_ref, br_ref, bi_ref, lam_ref,
         o_ref, state_ref, ext3_ref, a3_ref, b3_ref, edge_ref, h_ref) = refs
        xp_ref = None
    else:
        (x_ref, h0_ref, pmat_ref, cw_ref, cb_ref, wg_ref, br_ref, bi_ref, lam_ref,
         o_ref, xp_ref, state_ref, ext3_ref, a3_ref, b3_ref, edge_ref, h_ref) = refs
        acc_ref = None
    c = pl.program_id(1)
    tu = tt // units
    seg = tu // SUBLANES
    hist = LRU_CONV_WIDTH - 1
    row = lax.broadcasted_iota(jnp.int32, (SUBLANES, D_LRU), 0)

    def blk(g):
        return slice(g * SUBLANES, (g + 1) * SUBLANES)

    @pl.when(c == 0)
    def _():
        edge_ref[...] = jnp.zeros((hist * SUBLANES, D_LRU), F32)
        h_ref[...] = h0_ref[...]

    lam = lam_ref[...]
    softplus_neg_lam = jnp.maximum(-lam, 0.0) + jnp.log1p(jnp.exp(-jnp.abs(lam)))
    half_k = (-0.5 * LRU_C) * softplus_neg_lam
    half_br = 0.5 * br_ref[...]
    half_bi = 0.5 * bi_ref[...]

    order = list(range(units - 1, -1, -1) if reverse else range(units))
    for u in order:
        _lru_gates(slice(u * tu, (u + 1) * tu), x_ref, xp_ref, pmat_ref, cw_ref, cb_ref, wg_ref,
                   half_br, half_bi, half_k, ext3_ref.at[u], a3_ref.at[u], b3_ref.at[u], edge_ref,
                   reverse=reverse, tt=tu, seg=seg, hist=hist, row=row, blk=blk)

    def step(t, carry):
        g_ = (seg - 1 - t) if reverse else t
        rows = pl.ds(pl.multiple_of(g_ * SUBLANES, SUBLANES), SUBLANES)
        new = []
        for u, (h, p) in zip(order, carry):
            a = a3_ref[u, rows, :]
            h = a * h + b3_ref[u, rows, :]
            p = a * p
            b3_ref[u, rows, :] = h
            a3_ref[u, rows, :] = p
            new.append((h, p))
        return tuple(new)

    init = tuple((jnp.zeros((SUBLANES, D_LRU), F32), jnp.ones((SUBLANES, D_LRU), F32))
                 for _ in order)
    finals = lax.fori_loop(0, seg, step, init, unroll=8)

    for u, (h_fin, p_fin) in zip(order, finals):
        _lru_finish(slice(u * tu, (u + 1) * tu), h_fin, p_fin, acc_ref, pmat_ref, o_ref,
                    a3_ref.at[u], b3_ref.at[u], h_ref, reverse=reverse, tt=tu, seg=seg, row=row)
    state_ref[...] = h_ref[...]


def _lru_gates(rows_u, x_ref, xp_ref, pmat_ref, cw_ref, cb_ref, wg_ref, half_br, half_bi, half_k,
               ext_ref, a_ref, b_ref, edge_ref, *, reverse, tt, seg, hist, row, blk):
    base = 0 if reverse else hist
    if reverse:
        xp = x_ref[rows_u, :].astype(F32)
    else:
        xp = jnp.dot(pmat_ref[...], x_ref[rows_u, :], preferred_element_type=F32)
        xp_ref[rows_u, :] = xp.astype(BF16)
    ext_ref[base * SUBLANES:base * SUBLANES + tt, :] = xp
    for j in range(1, hist + 1):
        if reverse:
            rolled = pltpu.roll(ext_ref[blk(j - 1), :], SUBLANES - 1, 0)
            ext_ref[blk(seg + j - 1), :] = jnp.where(row == SUBLANES - 1, edge_ref[blk(j - 1), :],
                                                     rolled)
        else:
            rolled = pltpu.roll(ext_ref[blk(hist + seg - j), :], 1, 0)
            ext_ref[blk(hist - j), :] = jnp.where(row == 0, edge_ref[blk(j - 1), :], rolled)
        edge_ref[blk(j - 1), :] = rolled

    xc = jnp.broadcast_to(cb_ref[...], (tt, D_LRU))
    for k in range(LRU_CONV_WIDTH):
        off = (hist - k) if reverse else k
        xc = xc + cw_ref[k:k + 1, :] * ext_ref[off * SUBLANES:off * SUBLANES + tt, :]

    xcb = xc.astype(BF16)
    g_r, g_i = [], []
    for j in range(D_LRU // LANES):
        gj = jnp.dot(xcb[:, j * LANES:(j + 1) * LANES], wg_ref[j], preferred_element_type=F32)
        g_r.append(gj[:, :LANES])
        g_i.append(gj[:, LANES:])
    t_r = jnp.tanh(jnp.concatenate(g_r, axis=1) + half_br)
    t_i = jnp.tanh(jnp.concatenate(g_i, axis=1) + half_bi)
    log_a = half_k * t_r + half_k
    a_ref[...] = jnp.exp(log_a)
    th = jnp.tanh(log_a)
    one_minus_a2 = (-2.0 * th) / (1.0 - th)
    root = one_minus_a2 * lax.rsqrt(jnp.maximum(one_minus_a2, F32_TINY))
    half_xc = 0.5 * xc
    b_ref[...] = root * (half_xc * t_i + half_xc)


def _lru_finish(rows_u, h_fin, p_fin, acc_ref, pmat_ref, o_ref, a_ref, b_ref, h_ref, *, reverse,
                tt, seg, row):
    a, b = p_fin, h_fin
    for d in (1, 2, 4):
        keep = (row < SUBLANES - d) if reverse else (row >= d)
        shift = (SUBLANES - d) if reverse else d
        a_s = jnp.where(keep, pltpu.roll(a, shift, 0), 1.0)
        b_s = jnp.where(keep, pltpu.roll(b, shift, 0), 0.0)
        b = a * b_s + b
        a = a * a_s
    h_in = h_ref[...]
    end = a * h_in + b
    if reverse:
        carry_in = jnp.where(row < SUBLANES - 1, pltpu.roll(end, SUBLANES - 1, 0), h_in)
        h_ref[...] = jnp.broadcast_to(end[0:1, :], (SUBLANES, D_LRU))
    else:
        carry_in = jnp.where(row >= 1, pltpu.roll(end, 1, 0), h_in)
        h_ref[...] = jnp.broadcast_to(end[SUBLANES - 1:SUBLANES, :], (SUBLANES, D_LRU))

    h_all = (b_ref[...].reshape(seg, SUBLANES, D_LRU)
             + a_ref[...].reshape(seg, SUBLANES, D_LRU) * carry_in[None]).reshape(tt, D_LRU)
    if reverse:
        h_sum = (h_all + acc_ref[rows_u, :].astype(F32)).astype(BF16)
        o_ref[rows_u, :] = jnp.dot(pmat_ref[...], h_sum, preferred_element_type=F32
                                   ).astype(o_ref.dtype)
    else:
        o_ref[rows_u, :] = h_all.astype(o_ref.dtype)


def _lru_call(x_rows, h0, fwd, params, layer, direction, *, rows_per_batch, tt, units):
    cw, cb, wg_half, br, bi, lam = params
    reverse = direction == 1
    m = x_rows.shape[0] if not reverse else fwd[0].shape[0]
    nb = m // rows_per_batch
    nc = rows_per_batch // tt
    tu = tt // units

    def chunk(c):
        return (nc - 1 - c) if reverse else c

    row_spec = pl.BlockSpec((tt, D_LRU), lambda b, c: (b * nc + chunk(c), 0))
    full = lambda shape: pl.BlockSpec(shape, lambda b, c: (0,) * len(shape))
    h0_spec = pl.BlockSpec((None, SUBLANES, D_LRU), lambda b, c: (b, 0, 0))
    seg = tu // SUBLANES
    src = (np.arange(tu) % SUBLANES) * seg + np.arange(tu) // SUBLANES
    perm = src[:, None] == np.arange(tu)[None, :]
    if reverse:
        in_specs = [row_spec, h0_spec, row_spec]
        args = [fwd[0], h0, fwd[1], jnp.asarray(perm.T, BF16)]
    else:
        in_specs = [row_spec, h0_spec]
        args = [x_rows, h0, jnp.asarray(perm, BF16)]
    sel = lambda shape: pl.BlockSpec((None, None) + shape,
                                     lambda b, c: (layer, direction) + (0,) * len(shape))
    in_specs += [full((tu, tu)),
                 sel((LRU_CONV_WIDTH, D_LRU)), sel((1, D_LRU)), sel(wg_half.shape[2:]),
                 sel((1, D_LRU)), sel((1, D_LRU)), sel((1, D_LRU))]
    args += [cw, cb, wg_half, br, bi, lam]
    edge_rows = (LRU_CONV_WIDTH - 1) * SUBLANES
    rows_out = jax.ShapeDtypeStruct((m, D_LRU), BF16)
    state_spec = pl.BlockSpec((None, SUBLANES, D_LRU), lambda b, c: (b, 0, 0))
    state_out = jax.ShapeDtypeStruct((nb, SUBLANES, D_LRU), F32)
    return pl.pallas_call(
        functools.partial(_lru_kernel, reverse=reverse, tt=tt, units=units),
        grid=(nb, nc),
        in_specs=in_specs,
        out_specs=[row_spec, state_spec] if reverse else [row_spec, row_spec, state_spec],
        out_shape=[rows_out, state_out] if reverse else [rows_out, rows_out, state_out],
        scratch_shapes=[pltpu.VMEM((units, tu + edge_rows, D_LRU), F32),
                        pltpu.VMEM((units, tu, D_LRU), F32),
                        pltpu.VMEM((units, tu, D_LRU), F32),
                        pltpu.VMEM((edge_rows, D_LRU), F32),
                        pltpu.VMEM((SUBLANES, D_LRU), F32)],
        compiler_params=pltpu.CompilerParams(
            dimension_semantics=("arbitrary", "arbitrary"), vmem_limit_bytes=VMEM_LIMIT),
        name="rglru_rev" if reverse else "rglru_fwd",
    )(*args)


def _mix_kernel(*refs, names, tq, local, final, n_side):
    r = dict(zip(names, refs))
    i = pl.program_id(1)
    nt = pl.num_programs(1)
    ext_ref, ycat_ref = r["ext"], r["ycat"]
    for k in range(n_side):
        r[f"side_dst{k}"][...] = r[f"side_src{k}"][...].astype(BF16)

    def glu(ref):
        return ref[...].astype(F32)

    ext_ref[CONV_HALO:CONV_HALO + tq, :] = glu(r["glu"])
    zero_halo = jnp.zeros((CONV_HALO, D_CONV), F32)
    if local:
        ext_ref[0:CONV_HALO, :] = jnp.where(i > 0, glu(r["glu_prev"]), zero_halo)
        ext_ref[CONV_HALO + tq:, :] = jnp.where(i < nt - 1, glu(r["glu_next"]), zero_halo)
    else:
        ext_ref[0:CONV_HALO, :] = zero_halo
        ext_ref[CONV_HALO + tq:, :] = zero_halo

    for rc in range(tq // CONV_ROWS):
        t0 = rc * CONV_ROWS
        rows = slice(t0, t0 + CONV_ROWS)
        pieces = []
        for cc in range(D_CONV // CONV_COLS):
            cols = slice(cc * CONV_COLS, (cc + 1) * CONV_COLS)
            acc = jnp.broadcast_to(r["dw_b"][:, cols], (CONV_ROWS, CONV_COLS))
            for b in range(SUBLANES):
                z = None
                for a in range(-(-(CONV_WIDTH + 1) // SUBLANES)):
                    o = SUBLANES * a + b
                    if o < 1 or o > CONV_WIDTH:
                        continue
                    xs = ext_ref[t0 + SUBLANES * a:t0 + SUBLANES * a + CONV_ROWS + SUBLANES, cols]
                    term = (xs.reshape(-1, SUBLANES, CONV_COLS) * r["dw_w"][o - 1, :, cols][None]
                            ).reshape(CONV_ROWS + SUBLANES, CONV_COLS)
                    z = term if z is None else z + term
                acc = acc + z[b:b + CONV_ROWS, :]
            pieces.append(acc)
        acc = jnp.concatenate(pieces, axis=1)
        mu = jnp.mean(acc, axis=-1, keepdims=True)
        cen = acc - mu
        var = jnp.mean(cen * cen, axis=-1, keepdims=True)
        y = cen * lax.rsqrt(var + EPS) * r["ln_g"][...] + r["ln_b"][...]
        y = _silu(y).astype(BF16)
        y = jnp.dot(y, r["pw_w"][...], preferred_element_type=F32) + r["pw_b"][...]
        y = y * r["gates"][rows, :D_CONV].astype(F32)
        ycat_ref[rows, 0:D_CONV] = y.astype(BF16)

    ycat_ref[:, D_CONV:D_CONV + D_LRU] = (
        r["ylru"][...].astype(F32) * r["gates"][:, D_CONV:].astype(F32)).astype(BF16)

    qrows = GQA_GROUP * WINDOW
    qi = lax.broadcasted_iota(jnp.int32, (qrows, WINDOW), 0) % WINDOW
    kj = lax.broadcasted_iota(jnp.int32, (qrows, WINDOW), 1)
    neg_inf = jnp.float32(-jnp.inf)
    dn = (((1,), (1,)), ((), ()))
    n_qb = tq // WINDOW
    lc = r["kvc"].shape[0]
    for qb in range(n_qb):
        rows = slice(qb * WINDOW, (qb + 1) * WINDOW)
        for g in range(N_KV_HEADS):
            k_cols = slice(g * HEAD_DIM, (g + 1) * HEAD_DIM)
            v_cols = slice(D_KV + g * HEAD_DIM, D_KV + (g + 1) * HEAD_DIM)
            q_st = r["q"][qb, g * GQA_GROUP:(g + 1) * GQA_GROUP].reshape(qrows, HEAD_DIM)
            sink = jnp.concatenate(
                [jnp.broadcast_to(r["sink"][g * GQA_GROUP + hh:g * GQA_GROUP + hh + 1, 0:1],
                                  (WINDOW, 1)) for hh in range(GQA_GROUP)], axis=0) * LOG2E
            blocks = []
            for cb in range(lc // WINDOW):
                crow = slice(cb * WINDOW, (cb + 1) * WINDOW)
                s = lax.dot_general(q_st, r["kvc"][crow, k_cols], dn, preferred_element_type=F32)
                blocks.append((s, r["kvc"][crow, v_cols]))
            if local:
                if qb == 0:
                    k_p, v_p = r["kv_prev"][:, k_cols], r["kv_prev"][:, v_cols]
                    ok_p = i > 0
                else:
                    prows = slice((qb - 1) * WINDOW, qb * WINDOW)
                    k_p, v_p = r["kv"][prows, k_cols], r["kv"][prows, v_cols]
                    ok_p = True
                if qb == n_qb - 1:
                    k_n, v_n = r["kv_next"][:, k_cols], r["kv_next"][:, v_cols]
                    ok_n = i < nt - 1
                else:
                    nrows = slice((qb + 1) * WINDOW, (qb + 2) * WINDOW)
                    k_n, v_n = r["kv"][nrows, k_cols], r["kv"][nrows, v_cols]
                    ok_n = True
                s_p = lax.dot_general(q_st, k_p, dn, preferred_element_type=F32)
                s_c = lax.dot_general(q_st, r["kv"][rows, k_cols], dn, preferred_element_type=F32)
                s_n = lax.dot_general(q_st, k_n, dn, preferred_element_type=F32)
                s_p = jnp.where((kj >= qi) & ok_p, s_p, neg_inf)
                s_n = jnp.where((kj <= qi) & ok_n, s_n, neg_inf)
                blocks += [(s_p, v_p), (s_c, r["kv"][rows, v_cols]), (s_n, v_n)]
            m_el = blocks[0][0]
            for s, _ in blocks[1:]:
                m_el = jnp.maximum(m_el, s)
            m = jnp.maximum(jnp.max(m_el, axis=-1, keepdims=True), sink)
            den_el = None
            o = None
            for s, v in blocks:
                p = jnp.exp2(s - m)
                den_el = p if den_el is None else den_el + p
                pv = jnp.dot(p.astype(BF16), v, preferred_element_type=F32)
                o = pv if o is None else o + pv
            den = jnp.sum(den_el, axis=-1, keepdims=True) + jnp.exp2(sink - m)
            o = o / den
            for hh in range(GQA_GROUP):
                c0 = (g * GQA_GROUP + hh) * HEAD_DIM
                gate = r["attn_gate"][rows, c0:c0 + HEAD_DIM].astype(F32)
                ycat_ref[rows, D_CONV + D_LRU + c0:D_CONV + D_LRU + c0 + HEAD_DIM] = (
                    o[hh * WINDOW:(hh + 1) * WINDOW, :] * gate).astype(BF16)

    out_ref = r["out"]
    for c0 in range(0, out_ref.shape[1], OUT_CHUNK_N):
        cols = slice(c0, c0 + OUT_CHUNK_N)
        y = jnp.dot(ycat_ref[...], r["w_out"][:, cols], preferred_element_type=F32)
        out_ref[:, cols] = r["x"][:, cols] + r["gate"][:, cols] * y
    if final:
        xn = out_ref[...]
        ms = jnp.mean(xn * xn, axis=-1, keepdims=True)
        out_ref[...] = xn * lax.rsqrt(ms + EPS) * r["final_g"][...]


def _mix_call(x2, u2, q4, uc2, ylru, ada4, layer, ada_row_fn, wts, *, tq, rows_per_batch,
              ctx_rows, local, final, side_casts=()):
    m, d = x2.shape
    nt = rows_per_batch // tq
    nb = m // rows_per_batch
    names, specs, args = [], [], []

    def add(name, arr, spec):
        names.append(name)
        specs.append(spec)
        args.append(arr)

    def rowblk(width, col_off):
        return pl.BlockSpec((tq, width), lambda b, i: (b * nt + i, col_off // width))

    def halo(rows, width, col_off, nxt):
        per = tq // rows
        last = m // rows - 1
        if nxt:
            fn = lambda b, i: (jnp.minimum((b * nt + i + 1) * per, last), col_off // width)
        else:
            fn = lambda b, i: (jnp.maximum((b * nt + i) * per - 1, 0), col_off // width)
        return pl.BlockSpec((rows, width), fn)

    full = lambda shape: pl.BlockSpec(shape, lambda b, i: (0,) * len(shape),
                                      pipeline_mode=pl.Buffered(1))

    add("x", x2, pl.BlockSpec((tq, d), lambda b, i: (b * nt + i, 0)))
    add("gate", ada4, pl.BlockSpec((None, None, 1, d), lambda b, i: (layer, ada_row_fn(b), 0, 2)))
    add("glu", u2, rowblk(D_CONV, OFF_CONV_GLU))
    if local:
        add("glu_prev", u2, halo(CONV_HALO, D_CONV, OFF_CONV_GLU, False))
        add("glu_next", u2, halo(CONV_HALO, D_CONV, OFF_CONV_GLU, True))
    add("gates", u2, rowblk(D_CONV + D_LRU, OFF_CONV_GATE))
    add("ylru", ylru, pl.BlockSpec((tq, D_LRU), lambda b, i: (b * nt + i, 0)))
    add("q", q4, pl.BlockSpec((tq // WINDOW, N_Q_HEADS, WINDOW, HEAD_DIM),
                              lambda b, i: (b * nt + i, 0, 0, 0)))
    add("attn_gate", u2, rowblk(D_ATTN, U_ATTN_GATE))
    if local:
        add("kv", u2, rowblk(2 * D_KV, OFF_K))
        add("kv_prev", u2, halo(WINDOW, 2 * D_KV, OFF_K, False))
        add("kv_next", u2, halo(WINDOW, 2 * D_KV, OFF_K, True))
    add("kvc", uc2, pl.BlockSpec((ctx_rows, 2 * D_KV), lambda b, i: (b, OFF_K // (2 * D_KV))))
    for name in ("dw_w", "dw_b", "ln_g", "ln_b", "pw_w", "pw_b", "sink"):
        shape = wts[name].shape[1:]
        add(name, wts[name], pl.BlockSpec((None,) + shape,
                                          lambda b, i, nd=len(shape): (layer,) + (0,) * nd,
                                          pipeline_mode=pl.Buffered(1)))
    add("w_out", wts["w_out"], full(wts["w_out"].shape))
    if final:
        add("final_g", wts["final_g"], full(wts["final_g"].shape))
    side = _side_cast_specs(side_casts, nb * nt, lambda b, i: b * nt + i)
    for k, (spec, arr) in enumerate(zip(side[0], side[1])):
        add(f"side_src{k}", arr, spec)
    names += ["out"] + [f"side_dst{k}" for k in range(len(side_casts))] + ["ext", "ycat"]
    outs = pl.pallas_call(
        functools.partial(_mix_kernel, names=tuple(names), tq=tq, local=local, final=final,
                          n_side=len(side_casts)),
        grid=(nb, nt),
        in_specs=specs,
        out_specs=[pl.BlockSpec((tq, d), lambda b, i: (b * nt + i, 0))] + side[2],
        out_shape=[jax.ShapeDtypeStruct((m, d), F32)] + side[3],
        scratch_shapes=[pltpu.VMEM((tq + 2 * CONV_HALO, D_CONV), F32),
                        pltpu.VMEM((tq, D_MODEL), BF16)],
        compiler_params=pltpu.CompilerParams(
            dimension_semantics=("arbitrary", "arbitrary"), vmem_limit_bytes=VMEM_LIMIT),
        name="mix_lat" if local else "mix_ctx",
    )(*args)
    return outs[0], list(outs[1:])


def _rope_tables(seq):
    rows = seq // GRID_W
    row = np.repeat(np.arange(rows, dtype=np.float64), GRID_W)
    col = np.tile(np.arange(GRID_W, dtype=np.float64), rows)
    half = HEAD_DIM // 2
    inv = ROPE_BASE ** (-np.arange(0, half, 2, dtype=np.float64) / half)
    ang_r = row[:, None] * inv[None, :]
    ang_c = col[:, None] * inv[None, :]
    ang = np.concatenate([ang_r, ang_r, ang_c, ang_c], axis=-1)
    cos, sin = np.cos(ang), np.sin(ang)
    first = (np.arange(HEAD_DIM) % half) < (half // 2)
    return tuple(jnp.asarray(t, F32)
                 for t in (cos, np.where(first, -sin, 0.0), np.where(first, 0.0, sin)))


def _block_diag(w):
    nblk, blk = w.shape[-3], w.shape[-2]
    n = nblk * blk
    rows = w.reshape(w.shape[:-3] + (n, blk))
    spread = jnp.asarray(np.arange(n)[None, :] % blk == np.arange(blk)[:, None], w.dtype)
    mask = jnp.asarray(np.arange(n)[:, None] // blk == np.arange(n)[None, :] // blk, w.dtype)
    return jnp.matmul(rows, spread, precision=lax.Precision.HIGHEST) * mask


def _pick_tile(n, pref):
    t = min(n, pref)
    while n % t:
        t //= 2
    return t


def kernel(x, c, ctx, c_ctx, norm_g, w_ada, b_ada, w_in, dw_w, dw_b, ln_g, ln_b, pw_w, pw_b,
           lru_conv_w, lru_conv_b, lru_w_r, lru_b_r, lru_w_i, lru_b_i, lru_lam, attn_sink,
           w_out, final_g):
    nb, seq, d = x.shape
    lc = ctx.shape[1]
    depth = w_in.shape[0]
    ctx_row = nb
    assert nb < ADA_ROWS

    c_rows = jnp.concatenate([c, c_ctx[None, :], jnp.zeros((ADA_ROWS - nb - 1, d), F32)], axis=0)
    ada, (w_bf,) = _ada_call(c_rows, w_ada, b_ada, side_casts=[(w_in, 0)])
    ada4 = ada.reshape(depth, ADA_ROWS, 1, 3 * d)
    rope_tabs = _rope_tables(seq)
    norm_g3 = norm_g.reshape(depth, 1, d)
    vec4 = lambda p: p.reshape(depth, 2, 1, D_LRU)
    per_group = LANES // lru_w_r.shape[-1]
    grp = lambda w: _block_diag(w.reshape(w.shape[:2] + (-1, per_group) + w.shape[-2:]))
    wg_half = (0.5 * jnp.concatenate([grp(lru_w_r), grp(lru_w_i)], axis=-1)).astype(BF16)
    lru_params = (lru_conv_w, vec4(lru_conv_b), wg_half, vec4(lru_b_r), vec4(lru_b_i),
                  vec4(lru_lam))
    vec3 = lambda p: p.reshape(depth, 1, D_CONV)
    wts = {
        "dw_w": jnp.broadcast_to(dw_w[:, :, None, :], (depth, CONV_WIDTH, SUBLANES, D_CONV)),
        "dw_b": vec3(dw_b), "ln_g": vec3(ln_g), "ln_b": vec3(ln_b),
        "pw_w": pw_w.astype(BF16), "pw_b": vec3(pw_b),
        "sink": jnp.broadcast_to(attn_sink[:, :, None], (depth, N_Q_HEADS, HEAD_DIM)),
        "final_g": final_g.reshape(1, d),
    }

    x2 = x.reshape(nb * seq, d)
    xc2 = ctx.reshape(nb * lc, d)
    tm = _pick_tile(seq, 1024)
    tmc = _pick_tile(nb * lc, 1024)
    tq = _pick_tile(seq, 512)
    tt = _pick_tile(seq, 2048)
    tu = _pick_tile(tt, 512)
    zeros_h0 = jnp.zeros((nb, SUBLANES, D_LRU), F32)

    for l in range(depth):
        last = l == depth - 1
        u2, q4, casts = _inproj_call(x2, norm_g3, ada4, l, w_bf, rope_tabs, tm=tm,
                                     rows_per_batch=seq, ada_row_fn=lambda b: b, with_q=True,
                                     side_casts=[(w_out, l)])
        uc2, qc4, _ = _inproj_call(xc2, norm_g3, ada4, l, w_bf, None, tm=tmc, rows_per_batch=tmc,
                                   ada_row_fn=lambda b: ctx_row, with_q=not last)
        wts["w_out"] = casts[0]

        ctx_kw = dict(rows_per_batch=lc, tt=lc, units=1)
        lat_kw = dict(rows_per_batch=seq, tt=tt, units=tt // tu)
        hp_c, xp_c, st_f = _lru_call(uc2, zeros_h0, None, lru_params, l, 0, **ctx_kw)
        ylru_c, st_r = _lru_call(None, zeros_h0, (xp_c, hp_c), lru_params, l, 1, **ctx_kw)
        hp, xp, _ = _lru_call(u2, st_f, None, lru_params, l, 0, **lat_kw)
        ylru, _ = _lru_call(None, st_r, (xp, hp), lru_params, l, 1, **lat_kw)

        x2_new, casts = _mix_call(x2, u2, q4, uc2, ylru, ada4, l, lambda b: b, wts, tq=tq,
                                  rows_per_batch=seq, ctx_rows=lc, local=True, final=last,
                                  side_casts=[] if last else [(w_in, l + 1)])
        if not last:
            xc2, _ = _mix_call(xc2, uc2, qc4, uc2, ylru_c, ada4, l, lambda b: ctx_row, wts, tq=lc,
                               rows_per_batch=lc, ctx_rows=lc, local=False, final=False)
            w_bf = casts[0]
        x2 = x2_new
    return x2.reshape(nb, seq, d)
```

```python
import functools
import math

import jax
import jax.numpy as jnp
import numpy as np
from jax import lax
from jax.experimental import pallas as pl
from jax.experimental.pallas import tpu as pltpu

F32 = jnp.float32
BF16 = jnp.bfloat16

D_MODEL = 2048
D_CONV = 512
D_LRU = 512
HEAD_DIM = 128
N_Q_HEADS = 8
N_KV_HEADS = 2
GQA_GROUP = N_Q_HEADS // N_KV_HEADS
D_ATTN = N_Q_HEADS * HEAD_DIM
D_KV = N_KV_HEADS * HEAD_DIM
GRID_W = 64
CONV_WIDTH = 31
CONV_PAD = (CONV_WIDTH - 1) // 2
LRU_CONV_WIDTH = 4
LRU_C = 8.0
WINDOW = 128
ROPE_BASE = 10000.0
EPS = 1e-6

OFF_K = D_LRU
OFF_V = OFF_K + D_KV
OFF_MEM_END = OFF_V + D_KV
OFF_CONV_GLU = OFF_MEM_END
OFF_CONV_GATE = OFF_CONV_GLU + 2 * D_CONV
OFF_LRU_GATE = OFF_CONV_GATE + D_CONV
OFF_Q = OFF_LRU_GATE + D_LRU
OFF_ATTN_GATE = OFF_Q + D_ATTN
D_IN = OFF_ATTN_GATE + D_ATTN
U_ATTN_GATE = OFF_Q
D_U = D_IN - D_ATTN

ADA_ROWS = 8
ADA_TILE_N = 1024
IN_TILE_N = 1024
IN_CHUNK_N = 256
Q_TILE = OFF_Q // IN_TILE_N
GLU_TILE = OFF_CONV_GLU // IN_TILE_N
OUT_CHUNK_N = 256
CONV_HALO = 16
CONV_ROWS = 64
CONV_COLS = 256
SUBLANES = 8
BF16_SUBLANES = 16
LANES = 128
VMEM_LIMIT = 56 * 1024 * 1024
F32_TINY = float(np.finfo(np.float32).tiny)
LOG2E = math.log2(math.e)
Q_SCALE = HEAD_DIM ** -0.5 * LOG2E

assert OFF_Q % IN_TILE_N == 0 and D_ATTN == IN_TILE_N and OFF_MEM_END == IN_TILE_N
assert OFF_CONV_GLU % IN_TILE_N == 0 and 2 * D_CONV == IN_TILE_N
assert CONV_HALO >= CONV_PAD and CONV_HALO % BF16_SUBLANES == 0


def _sigmoid(x):
    return 0.5 * jnp.tanh(0.5 * x) + 0.5


def _silu(x):
    hx = 0.5 * x
    return hx * jnp.tanh(hx) + hx


def _ada_kernel(*refs, n_side):
    c_ref, w_ref, b_ref = refs[:3]
    side_src = refs[3:3 + n_side]
    o_ref = refs[3 + n_side]
    side_dst = refs[4 + n_side:]
    for src, dst in zip(side_src, side_dst):
        dst[...] = src[...].astype(BF16)
    ca = _silu(c_ref[...])
    o_ref[...] = jnp.dot(ca.astype(BF16), w_ref[...].astype(BF16),
                         preferred_element_type=F32) + b_ref[...]


def _ada_call(c_rows, w_ada, b_ada, side_casts=()):
    depth, d, n = w_ada.shape
    tn = ADA_TILE_N
    n_tiles = n // tn
    side = _side_cast_specs(side_casts, depth * n_tiles, lambda l, j: l * n_tiles + j)
    outs = pl.pallas_call(
        functools.partial(_ada_kernel, n_side=len(side_casts)),
        grid=(depth, n_tiles),
        in_specs=[
            pl.BlockSpec((ADA_ROWS, d), lambda l, j: (0, 0)),
            pl.BlockSpec((None, d, tn), lambda l, j: (l, 0, j)),
            pl.BlockSpec((None, 1, tn), lambda l, j: (l, 0, j)),
        ] + side[0],
        out_specs=[pl.BlockSpec((None, ADA_ROWS, tn), lambda l, j: (l, 0, j))] + side[2],
        out_shape=[jax.ShapeDtypeStruct((depth, ADA_ROWS, n), F32)] + side[3],
        compiler_params=pltpu.CompilerParams(
            dimension_semantics=("arbitrary", "arbitrary"), vmem_limit_bytes=VMEM_LIMIT),
        name="ada_proj",
    )(c_rows, w_ada, b_ada.reshape(depth, 1, n), *side[1])
    return outs[0], list(outs[1:])


def _side_cast_specs(side_casts, n_steps, step_of):
    in_specs, args, out_specs, out_shapes = [], [], [], []
    for src, src_layer in side_casts:
        rows, cols = src.shape[1:]
        piece = BF16_SUBLANES
        while rows // piece > n_steps:
            piece *= 2
        n_pieces = rows // piece
        assert n_pieces * piece == rows and n_pieces <= n_steps
        piece_of = lambda *idx, n=n_pieces: jnp.minimum(step_of(*idx), n - 1)
        in_specs.append(pl.BlockSpec((None, piece, cols),
                                     lambda *idx, sl=src_layer, f=piece_of: (sl, f(*idx), 0)))
        args.append(src)
        out_specs.append(pl.BlockSpec((piece, cols), lambda *idx, f=piece_of: (f(*idx), 0)))
        out_shapes.append(jax.ShapeDtypeStruct((rows, cols), BF16))
    return in_specs, args, out_specs, out_shapes


def _rope(t, cos, sin_a, sin_b):
    quarter = HEAD_DIM // 4
    return (t * cos + pltpu.roll(t, HEAD_DIM - quarter, 1) * sin_a
            + pltpu.roll(t, quarter, 1) * sin_b)


def _inproj_kernel(*refs, rope, with_q, tm, n_side):
    refs = list(refs)
    x_ref, g_ref, shift_ref, scale_ref, w_ref = refs[:5]
    del refs[:5]
    if rope:
        cos_ref, sa_ref, sb_ref = refs[:3]
        del refs[:3]
    side_src = refs[:n_side]
    del refs[:n_side]
    o_ref = refs.pop(0)
    q_ref = refs.pop(0) if with_q else None
    side_dst = refs[:n_side]
    del refs[:n_side]
    h_ref = refs.pop(0)
    j = pl.program_id(1)

    for src, dst in zip(side_src, side_dst):
        dst[...] = src[...].astype(BF16)

    @pl.when(j == 0)
    def _():
        x = x_ref[...]
        ms = jnp.mean(x * x, axis=-1, keepdims=True)
        gain = g_ref[...] * (1.0 + scale_ref[...])
        h_ref[...] = (x * lax.rsqrt(ms + EPS) * gain + shift_ref[...]).astype(BF16)

    chunks = [slice(c0, c0 + IN_CHUNK_N) for c0 in range(0, IN_TILE_N, IN_CHUNK_N)]
    heads_per_chunk = IN_CHUNK_N // HEAD_DIM

    def proj(cols):
        return jnp.dot(h_ref[...], w_ref[:, cols], preferred_element_type=F32)

    def rope_heads(u):
        if not rope:
            return [u[:, hh * HEAD_DIM:(hh + 1) * HEAD_DIM] for hh in range(heads_per_chunk)]
        return [_rope(u[:, hh * HEAD_DIM:(hh + 1) * HEAD_DIM], cos_ref[...], sa_ref[...],
                      sb_ref[...]) for hh in range(heads_per_chunk)]

    def plain_tile():
        for cols in chunks:
            o_ref[:, cols] = proj(cols).astype(BF16)

    def q_tile():
        for ci, cols in enumerate(chunks):
            for hh, t in enumerate(rope_heads(proj(cols))):
                t = (t * Q_SCALE).astype(BF16)
                for qb in range(tm // WINDOW):
                    q_ref[qb, ci * heads_per_chunk + hh] = t[qb * WINDOW:(qb + 1) * WINDOW, :]

    def kv_tile():
        for cols in chunks:
            u = proj(cols)
            if OFF_K <= cols.start < OFF_V:
                u = jnp.concatenate(rope_heads(u), axis=1)
            o_ref[:, cols] = u.astype(BF16)

    def glu_tile():
        half = len(chunks) // 2
        for ci in range(half):
            o_ref[:, chunks[ci]] = (proj(chunks[ci]) * _sigmoid(proj(chunks[ci + half]))
                                    ).astype(BF16)
            o_ref[:, chunks[ci + half]] = jnp.zeros((tm, IN_CHUNK_N), BF16)

    def silu_tile():
        for cols in chunks:
            o_ref[:, cols] = _silu(proj(cols)).astype(BF16)

    if not with_q:
        plain_tile()
    else:
        pl.when(j == 0)(kv_tile if rope else plain_tile)
        pl.when(j == GLU_TILE)(glu_tile)
        pl.when(j == Q_TILE)(q_tile)
        pl.when((j != 0) & (j != GLU_TILE) & (j != Q_TILE))(silu_tile)


def _inproj_call(x2, norm_g, ada4, layer, w_bf, rope_tabs, *, tm, rows_per_batch, ada_row_fn,
                 with_q, side_casts=()):
    m, d = x2.shape
    tiles_per_batch = rows_per_batch // tm
    rope = rope_tabs is not None
    n_tiles = D_IN // IN_TILE_N if with_q else 1
    n_steps = (m // tm) * n_tiles

    def ada_spec(part):
        return pl.BlockSpec((None, None, 1, d),
                            lambda i, j: (layer, ada_row_fn(i // tiles_per_batch), 0, part))

    in_specs = [
        pl.BlockSpec((tm, d), lambda i, j: (i, 0)),
        pl.BlockSpec((None, 1, d), lambda i, j: (layer, 0, 0)),
        ada_spec(0), ada_spec(1),
        pl.BlockSpec((d, IN_TILE_N), lambda i, j: (0, j)),
    ]
    args = [x2, norm_g, ada4, ada4, w_bf]
    if rope:
        tab_spec = pl.BlockSpec((tm, HEAD_DIM), lambda i, j: (i % tiles_per_batch, 0))
        in_specs += [tab_spec] * 3
        args += list(rope_tabs)
    out_specs = [pl.BlockSpec((tm, IN_TILE_N), lambda i, j: (i, jnp.where(j > Q_TILE, j - 1, j)))]
    out_shape = [jax.ShapeDtypeStruct((m, D_U if with_q else IN_TILE_N), BF16)]
    if with_q:
        out_specs.append(pl.BlockSpec((tm // WINDOW, N_Q_HEADS, WINDOW, HEAD_DIM),
                                      lambda i, j: (i, 0, 0, 0)))
        out_shape.append(jax.ShapeDtypeStruct((m // WINDOW, N_Q_HEADS, WINDOW, HEAD_DIM), BF16))
    side = _side_cast_specs(side_casts, n_steps, lambda i, j: i * n_tiles + j)
    in_specs += side[0]
    args += side[1]
    out_specs += side[2]
    out_shape += side[3]
    outs = pl.pallas_call(
        functools.partial(_inproj_kernel, rope=rope, with_q=with_q, tm=tm,
                          n_side=len(side_casts)),
        grid=(m // tm, n_tiles),
        in_specs=in_specs,
        out_specs=out_specs,
        out_shape=out_shape,
        scratch_shapes=[pltpu.VMEM((tm, d), BF16)],
        compiler_params=pltpu.CompilerParams(
            dimension_semantics=("arbitrary", "arbitrary"), vmem_limit_bytes=VMEM_LIMIT),
        name="in_proj",
    )(*args)
    n_main = 2 if with_q else 1
    return outs[0], (outs[1] if with_q else None), list(outs[n_main:])


def _lru_kernel(*refs, reverse, tt, units):
    if reverse:
        (x_ref, h0_ref, acc_ref, pmat_ref, cw_ref, cb_ref, wg_ref, br_ref, bi_ref, lam_ref,
         o_ref, state_ref, ext3_ref, a3_ref, b3_ref, edge_ref, h_ref) = refs
        xp_ref = None
    else:
        (x_ref, h0_ref, pmat_ref, cw_ref, cb_ref, wg_ref, br_ref, bi_ref, lam_ref,
         o_ref, xp_ref, state_ref, ext3_ref, a3_ref, b3_ref, edge_ref, h_ref) = refs
        acc_ref = None
    c = pl.program_id(1)
    tu = tt // units
    seg = tu // SUBLANES
    hist = LRU_CONV_WIDTH - 1
    row = lax.broadcasted_iota(jnp.int32, (SUBLANES, D_LRU), 0)

    def blk(g):
        return slice(g * SUBLANES, (g + 1) * SUBLANES)

    @pl.when(c == 0)
    def _():
        edge_ref[...] = jnp.zeros((hist * SUBLANES, D_LRU), F32)
        h_ref[...] = h0_ref[...]

    lam = lam_ref[...]
    softplus_neg_lam = jnp.maximum(-lam, 0.0) + jnp.log1p(jnp.exp(-jnp.abs(lam)))
    half_k = (-0.5 * LRU_C) * softplus_neg_lam
    half_br = 0.5 * br_ref[...]
    half_bi = 0.5 * bi_ref[...]

    order = list(range(units - 1, -1, -1) if reverse else range(units))
    for u in order:
        _lru_gates(slice(u * tu, (u + 1) * tu), x_ref, xp_ref, pmat_ref, cw_ref, cb_ref, wg_ref,
                   half_br, half_bi, half_k, ext3_ref.at[u], a3_ref.at[u], b3_ref.at[u], edge_ref,
                   reverse=reverse, tt=tu, seg=seg, hist=hist, row=row, blk=blk)

    def step(t, carry):
        g_ = (seg - 1 - t) if reverse else t
        rows = pl.ds(pl.multiple_of(g_ * SUBLANES, SUBLANES), SUBLANES)
        new = []
        for u, (h, p) in zip(order, carry):
            a = a3_ref[u, rows, :]
            h = a * h + b3_ref[u, rows, :]
            p = a * p
            b3_ref[u, rows, :] = h
            a3_ref[u, rows, :] = p
            new.append((h, p))
        return tuple(new)

    init = tuple((jnp.zeros((SUBLANES, D_LRU), F32), jnp.ones((SUBLANES, D_LRU), F32))
                 for _ in order)
    finals = lax.fori_loop(0, seg, step, init, unroll=8)

    for u, (h_fin, p_fin) in zip(order, finals):
        _lru_finish(slice(u * tu, (u + 1) * tu), h_fin, p_fin, acc_ref, pmat_ref, o_ref,
                    a3_ref.at[u], b3_ref.at[u], h_ref, reverse=reverse, tt=tu, seg=seg, row=row)
    state_ref[...] = h_ref[...]


def _lru_gates(rows_u, x_ref, xp_ref, pmat_ref, cw_ref, cb_ref, wg_ref, half_br, half_bi, half_k,
               ext_ref, a_ref, b_ref, edge_ref, *, reverse, tt, seg, hist, row, blk):
    base = 0 if reverse else hist
    if reverse:
        xp = x_ref[rows_u, :].astype(F32)
    else:
        xp = jnp.dot(pmat_ref[...], x_ref[rows_u, :], preferred_element_type=F32)
        xp_ref[rows_u, :] = xp.astype(BF16)
    ext_ref[base * SUBLANES:base * SUBLANES + tt, :] = xp
    for j in range(1, hist + 1):
        if reverse:
            rolled = pltpu.roll(ext_ref[blk(j - 1), :], SUBLANES - 1, 0)
            ext_ref[blk(seg + j - 1), :] = jnp.where(row == SUBLANES - 1, edge_ref[blk(j - 1), :],
                                                     rolled)
        else:
            rolled = pltpu.roll(ext_ref[blk(hist + seg - j), :], 1, 0)
            ext_ref[blk(hist - j), :] = jnp.where(row == 0, edge_ref[blk(j - 1), :], rolled)
        edge_ref[blk(j - 1), :] = rolled

    xc = jnp.broadcast_to(cb_ref[...], (tt, D_LRU))
    for k in range(LRU_CONV_WIDTH):
        off = (hist - k) if reverse else k
        xc = xc + cw_ref[k:k + 1, :] * ext_ref[off * SUBLANES:off * SUBLANES + tt, :]

    xcb = xc.astype(BF16)
    g_r, g_i = [], []
    for j in range(D_LRU // LANES):
        gj = jnp.dot(xcb[:, j * LANES:(j + 1) * LANES], wg_ref[j], preferred_element_type=F32)
        g_r.append(gj[:, :LANES])
        g_i.append(gj[:, LANES:])
    t_r = jnp.tanh(jnp.concatenate(g_r, axis=1) + half_br)
    t_i = jnp.tanh(jnp.concatenate(g_i, axis=1) + half_bi)
    log_a = half_k * t_r + half_k
    a_ref[...] = jnp.exp(log_a)
    th = jnp.tanh(log_a)
    one_minus_a2 = (-2.0 * th) / (1.0 - th)
    root = one_minus_a2 * lax.rsqrt(jnp.maximum(one_minus_a2, F32_TINY))
    half_xc = 0.5 * xc
    b_ref[...] = root * (half_xc * t_i + half_xc)


def _lru_finish(rows_u, h_fin, p_fin, acc_ref, pmat_ref, o_ref, a_ref, b_ref, h_ref, *, reverse,
                tt, seg, row):
    a, b = p_fin, h_fin
    for d in (1, 2, 4):
        keep = (row < SUBLANES - d) if reverse else (row >= d)
        shift = (SUBLANES - d) if reverse else d
        a_s = jnp.where(keep, pltpu.roll(a, shift, 0), 1.0)
        b_s = jnp.where(keep, pltpu.roll(b, shift, 0), 0.0)
        b = a * b_s + b
        a = a * a_s
    h_in = h_ref[...]
    end = a * h_in + b
    if reverse:
        carry_in = jnp.where(row < SUBLANES - 1, pltpu.roll(end, SUBLANES - 1, 0), h_in)
        h_ref[...] = jnp.broadcast_to(end[0:1, :], (SUBLANES, D_LRU))
    else:
        carry_in = jnp.where(row >= 1, pltpu.roll(end, 1, 0), h_in)
        h_ref[...] = jnp.broadcast_to(end[SUBLANES - 1:SUBLANES, :], (SUBLANES, D_LRU))

    h_all = (b_ref[...].reshape(seg, SUBLANES, D_LRU)
             + a_ref[...].reshape(seg, SUBLANES, D_LRU) * carry_in[None]).reshape(tt, D_LRU)
    if reverse:
        h_sum = (h_all + acc_ref[rows_u, :].astype(F32)).astype(BF16)
        o_ref[rows_u, :] = jnp.dot(pmat_ref[...], h_sum, preferred_element_type=F32
                                   ).astype(o_ref.dtype)
    else:
        o_ref[rows_u, :] = h_all.astype(o_ref.dtype)


def _lru_call(x_rows, h0, fwd, params, layer, direction, *, rows_per_batch, tt, units):
    cw, cb, wg_half, br, bi, lam = params
    reverse = direction == 1
    m = x_rows.shape[0] if not reverse else fwd[0].shape[0]
    nb = m // rows_per_batch
    nc = rows_per_batch // tt
    tu = tt // units

    def chunk(c):
        return (nc - 1 - c) if reverse else c

    row_spec = pl.BlockSpec((tt, D_LRU), lambda b, c: (b * nc + chunk(c), 0))
    full = lambda shape: pl.BlockSpec(shape, lambda b, c: (0,) * len(shape))
    h0_spec = pl.BlockSpec((None, SUBLANES, D_LRU), lambda b, c: (b, 0, 0))
    seg = tu // SUBLANES
    src = (np.arange(tu) % SUBLANES) * seg + np.arange(tu) // SUBLANES
    perm = src[:, None] == np.arange(tu)[None, :]
    if reverse:
        in_specs = [row_spec, h0_spec, row_spec]
        args = [fwd[0], h0, fwd[1], jnp.asarray(perm.T, BF16)]
    else:
        in_specs = [row_spec, h0_spec]
        args = [x_rows, h0, jnp.asarray(perm, BF16)]
    sel = lambda shape: pl.BlockSpec((None, None) + shape,
                                     lambda b, c: (layer, direction) + (0,) * len(shape))
    in_specs += [full((tu, tu)),
                 sel((LRU_CONV_WIDTH, D_LRU)), sel((1, D_LRU)), sel(wg_half.shape[2:]),
                 sel((1, D_LRU)), sel((1, D_LRU)), sel((1, D_LRU))]
    args += [cw, cb, wg_half, br, bi, lam]
    edge_rows = (LRU_CONV_WIDTH - 1) * SUBLANES
    rows_out = jax.ShapeDtypeStruct((m, D_LRU), BF16)
    state_spec = pl.BlockSpec((None, SUBLANES, D_LRU), lambda b, c: (b, 0, 0))
    state_out = jax.ShapeDtypeStruct((nb, SUBLANES, D_LRU), F32)
    return pl.pallas_call(
        functools.partial(_lru_kernel, reverse=reverse, tt=tt, units=units),
        grid=(nb, nc),
        in_specs=in_specs,
        out_specs=[row_spec, state_spec] if reverse else [row_spec, row_spec, state_spec],
        out_shape=[rows_out, state_out] if reverse else [rows_out, rows_out, state_out],
        scratch_shapes=[pltpu.VMEM((units, tu + edge_rows, D_LRU), F32),
                        pltpu.VMEM((units, tu, D_LRU), F32),
                        pltpu.VMEM((units, tu, D_LRU), F32),
                        pltpu.VMEM((edge_rows, D_LRU), F32),
                        pltpu.VMEM((SUBLANES, D_LRU), F32)],
        compiler_params=pltpu.CompilerParams(
            dimension_semantics=("arbitrary", "arbitrary"), vmem_limit_bytes=VMEM_LIMIT),
        name="rglru_rev" if reverse else "rglru_fwd",
    )(*args)


def _mix_kernel(*refs, names, tq, local, final, n_side):
    r = dict(zip(names, refs))
    i = pl.program_id(1)
    nt = pl.num_programs(1)
    ext_ref, ycat_ref = r["ext"], r["ycat"]
    for k in range(n_side):
        r[f"side_dst{k}"][...] = r[f"side_src{k}"][...].astype(BF16)

    def glu(ref):
        return ref[...].astype(F32)

    ext_ref[CONV_HALO:CONV_HALO + tq, :] = glu(r["glu"])
    zero_halo = jnp.zeros((CONV_HALO, D_CONV), F32)
    if local:
        ext_ref[0:CONV_HALO, :] = jnp.where(i > 0, glu(r["glu_prev"]), zero_halo)
        ext_ref[CONV_HALO + tq:, :] = jnp.where(i < nt - 1, glu(r["glu_next"]), zero_halo)
    else:
        ext_ref[0:CONV_HALO, :] = zero_halo
        ext_ref[CONV_HALO + tq:, :] = zero_halo

    for rc in range(tq // CONV_ROWS):
        t0 = rc * CONV_ROWS
        rows = slice(t0, t0 + CONV_ROWS)
        pieces = []
        for cc in range(D_CONV // CONV_COLS):
            cols = slice(cc * CONV_COLS, (cc + 1) * CONV_COLS)
            acc = jnp.broadcast_to(r["dw_b"][:, cols], (CONV_ROWS, CONV_COLS))
            for b in range(SUBLANES):
                z = None
                for a in range(-(-(CONV_WIDTH + 1) // SUBLANES)):
                    o = SUBLANES * a + b
                    if o < 1 or o > CONV_WIDTH:
                        continue
                    xs = ext_ref[t0 + SUBLANES * a:t0 + SUBLANES * a + CONV_ROWS + SUBLANES, cols]
                    term = (xs.reshape(-1, SUBLANES, CONV_COLS) * r["dw_w"][o - 1, :, cols][None]
                            ).reshape(CONV_ROWS + SUBLANES, CONV_COLS)
                    z = term if z is None else z + term
                acc = acc + z[b:b + CONV_ROWS, :]
            pieces.append(acc)
        acc = jnp.concatenate(pieces, axis=1)
        mu = jnp.mean(acc, axis=-1, keepdims=True)
        cen = acc - mu
        var = jnp.mean(cen * cen, axis=-1, keepdims=True)
        y = cen * lax.rsqrt(var + EPS) * r["ln_g"][...] + r["ln_b"][...]
        y = _silu(y).astype(BF16)
        y = jnp.dot(y, r["pw_w"][...], preferred_element_type=F32) + r["pw_b"][...]
        y = y * r["gates"][rows, :D_CONV].astype(F32)
        ycat_ref[rows, 0:D_CONV] = y.astype(BF16)

    ycat_ref[:, D_CONV:D_CONV + D_LRU] = (
        r["ylru"][...].astype(F32) * r["gates"][:, D_CONV:].astype(F32)).astype(BF16)

    qrows = GQA_GROUP * WINDOW
    qi = lax.broadcasted_iota(jnp.int32, (qrows, WINDOW), 0) % WINDOW
    kj = lax.broadcasted_iota(jnp.int32, (qrows, WINDOW), 1)
    neg_inf = jnp.float32(-jnp.inf)
    dn = (((1,), (1,)), ((), ()))
    n_qb = tq // WINDOW
    lc = r["kvc"].shape[0]
    for qb in range(n_qb):
        rows = slice(qb * WINDOW, (qb + 1) * WINDOW)
        for g in range(N_KV_HEADS):
            k_cols = slice(g * HEAD_DIM, (g + 1) * HEAD_DIM)
            v_cols = slice(D_KV + g * HEAD_DIM, D_KV + (g + 1) * HEAD_DIM)
            q_st = r["q"][qb, g * GQA_GROUP:(g + 1) * GQA_GROUP].reshape(qrows, HEAD_DIM)
            sink = jnp.concatenate(
                [jnp.broadcast_to(r["sink"][g * GQA_GROUP + hh:g * GQA_GROUP + hh + 1, 0:1],
                                  (WINDOW, 1)) for hh in range(GQA_GROUP)], axis=0) * LOG2E
            blocks = []
            for cb in range(lc // WINDOW):
                crow = slice(cb * WINDOW, (cb + 1) * WINDOW)
                s = lax.dot_general(q_st, r["kvc"][crow, k_cols], dn, preferred_element_type=F32)
                blocks.append((s, r["kvc"][crow, v_cols]))
            if local:
                if qb == 0:
                    k_p, v_p = r["kv_prev"][:, k_cols], r["kv_prev"][:, v_cols]
                    ok_p = i > 0
                else:
                    prows = slice((qb - 1) * WINDOW, qb * WINDOW)
                    k_p, v_p = r["kv"][prows, k_cols], r["kv"][prows, v_cols]
                    ok_p = True
                if qb == n_qb - 1:
                    k_n, v_n = r["kv_next"][:, k_cols], r["kv_next"][:, v_cols]
                    ok_n = i < nt - 1
                else:
                    nrows = slice((qb + 1) * WINDOW, (qb + 2) * WINDOW)
                    k_n, v_n = r["kv"][nrows, k_cols], r["kv"][nrows, v_cols]
                    ok_n = True
                s_p = lax.dot_general(q_st, k_p, dn, preferred_element_type=F32)
                s_c = lax.dot_general(q_st, r["kv"][rows, k_cols], dn, preferred_element_type=F32)
                s_n = lax.dot_general(q_st, k_n, dn, preferred_element_type=F32)
                s_p = jnp.where((kj >= qi) & ok_p, s_p, neg_inf)
                s_n = jnp.where((kj <= qi) & ok_n, s_n, neg_inf)
                blocks += [(s_p, v_p), (s_c, r["kv"][rows, v_cols]), (s_n, v_n)]
            m_el = blocks[0][0]
            for s, _ in blocks[1:]:
                m_el = jnp.maximum(m_el, s)
            m = jnp.maximum(jnp.max(m_el, axis=-1, keepdims=True), sink)
            den_el = None
            o = None
            for s, v in blocks:
                p = jnp.exp2(s - m)
                den_el = p if den_el is None else den_el + p
                pv = jnp.dot(p.astype(BF16), v, preferred_element_type=F32)
                o = pv if o is None else o + pv
            den = jnp.sum(den_el, axis=-1, keepdims=True) + jnp.exp2(sink - m)
            o = o / den
            for hh in range(GQA_GROUP):
                c0 = (g * GQA_GROUP + hh) * HEAD_DIM
                gate = r["attn_gate"][rows, c0:c0 + HEAD_DIM].astype(F32)
                ycat_ref[rows, D_CONV + D_LRU + c0:D_CONV + D_LRU + c0 + HEAD_DIM] = (
                    o[hh * WINDOW:(hh + 1) * WINDOW, :] * gate).astype(BF16)

    out_ref = r["out"]
    for c0 in range(0, out_ref.shape[1], OUT_CHUNK_N):
        cols = slice(c0, c0 + OUT_CHUNK_N)
        y = jnp.dot(ycat_ref[...], r["w_out"][:, cols], preferred_element_type=F32)
        out_ref[:, cols] = r["x"][:, cols] + r["gate"][:, cols] * y
    if final:
        xn = out_ref[...]
        ms = jnp.mean(xn * xn, axis=-1, keepdims=True)
        out_ref[...] = xn * lax.rsqrt(ms + EPS) * r["final_g"][...]


def _mix_call(x2, u2, q4, uc2, ylru, ada4, layer, ada_row_fn, wts, *, tq, rows_per_batch,
              ctx_rows, local, final, side_casts=()):
    m, d = x2.shape
    nt = rows_per_batch // tq
    nb = m // rows_per_batch
    names, specs, args = [], [], []

    def add(name, arr, spec):
        names.append(name)
        specs.append(spec)
        args.append(arr)

    def rowblk(width, col_off):
        return pl.BlockSpec((tq, width), lambda b, i: (b * nt + i, col_off // width))

    def halo(rows, width, col_off, nxt):
        per = tq // rows
        last = m // rows - 1
        if nxt:
            fn = lambda b, i: (jnp.minimum((b * nt + i + 1) * per, last), col_off // width)
        else:
            fn = lambda b, i: (jnp.maximum((b * nt + i) * per - 1, 0), col_off // width)
        return pl.BlockSpec((rows, width), fn)

    full = lambda shape: pl.BlockSpec(shape, lambda b, i: (0,) * len(shape),
                                      pipeline_mode=pl.Buffered(1))

    add("x", x2, pl.BlockSpec((tq, d), lambda b, i: (b * nt + i, 0)))
    add("gate", ada4, pl.BlockSpec((None, None, 1, d), lambda b, i: (layer, ada_row_fn(b), 0, 2)))
    add("glu", u2, rowblk(D_CONV, OFF_CONV_GLU))
    if local:
        add("glu_prev", u2, halo(CONV_HALO, D_CONV, OFF_CONV_GLU, False))
        add("glu_next", u2, halo(CONV_HALO, D_CONV, OFF_CONV_GLU, True))
    add("gates", u2, rowblk(D_CONV + D_LRU, OFF_CONV_GATE))
    add("ylru", ylru, pl.BlockSpec((tq, D_LRU), lambda b, i: (b * nt + i, 0)))
    add("q", q4, pl.BlockSpec((tq // WINDOW, N_Q_HEADS, WINDOW, HEAD_DIM),
                              lambda b, i: (b * nt + i, 0, 0, 0)))
    add("attn_gate", u2, rowblk(D_ATTN, U_ATTN_GATE))
    if local:
        add("kv", u2, rowblk(2 * D_KV, OFF_K))
        add("kv_prev", u2, halo(WINDOW, 2 * D_KV, OFF_K, False))
        add("kv_next", u2, halo(WINDOW, 2 * D_KV, OFF_K, True))
    add("kvc", uc2, pl.BlockSpec((ctx_rows, 2 * D_KV), lambda b, i: (b, OFF_K // (2 * D_KV))))
    for name in ("dw_w", "dw_b", "ln_g", "ln_b", "pw_w", "pw_b", "sink"):
        shape = wts[name].shape[1:]
        add(name, wts[name], pl.BlockSpec((None,) + shape,
                                          lambda b, i, nd=len(shape): (layer,) + (0,) * nd,
                                          pipeline_mode=pl.Buffered(1)))
    add("w_out", wts["w_out"], full(wts["w_out"].shape))
    if final:
        add("final_g", wts["final_g"], full(wts["final_g"].shape))
    side = _side_cast_specs(side_casts, nb * nt, lambda b, i: b * nt + i)
    for k, (spec, arr) in enumerate(zip(side[0], side[1])):
        add(f"side_src{k}", arr, spec)
    names += ["out"] + [f"side_dst{k}" for k in range(len(side_casts))] + ["ext", "ycat"]
    outs = pl.pallas_call(
        functools.partial(_mix_kernel, names=tuple(names), tq=tq, local=local, final=final,
                          n_side=len(side_casts)),
        grid=(nb, nt),
        in_specs=specs,
        out_specs=[pl.BlockSpec((tq, d), lambda b, i: (b * nt + i, 0))] + side[2],
        out_shape=[jax.ShapeDtypeStruct((m, d), F32)] + side[3],
        scratch_shapes=[pltpu.VMEM((tq + 2 * CONV_HALO, D_CONV), F32),
                        pltpu.VMEM((tq, D_MODEL), BF16)],
        compiler_params=pltpu.CompilerParams(
            dimension_semantics=("arbitrary", "arbitrary"), vmem_limit_bytes=VMEM_LIMIT),
        name="mix_lat" if local else "mix_ctx",
    )(*args)
    return outs[0], list(outs[1:])


def _rope_tables(seq):
    rows = seq // GRID_W
    row = np.repeat(np.arange(rows, dtype=np.float64), GRID_W)
    col = np.tile(np.arange(GRID_W, dtype=np.float64), rows)
    half = HEAD_DIM // 2
    inv = ROPE_BASE ** (-np.arange(0, half, 2, dtype=np.float64) / half)
    ang_r = row[:, None] * inv[None, :]
    ang_c = col[:, None] * inv[None, :]
    ang = np.concatenate([ang_r, ang_r, ang_c, ang_c], axis=-1)
    cos, sin = np.cos(ang), np.sin(ang)
    first = (np.arange(HEAD_DIM) % half) < (half // 2)
    return tuple(jnp.asarray(t, F32)
                 for t in (cos, np.where(first, -sin, 0.0), np.where(first, 0.0, sin)))


def _block_diag(w):
    nblk, blk = w.shape[-3], w.shape[-2]
    n = nblk * blk
    rows = w.reshape(w.shape[:-3] + (n, blk))
    spread = jnp.asarray(np.arange(n)[None, :] % blk == np.arange(blk)[:, None], w.dtype)
    mask = jnp.asarray(np.arange(n)[:, None] // blk == np.arange(n)[None, :] // blk, w.dtype)
    return jnp.matmul(rows, spread, precision=lax.Precision.HIGHEST) * mask


def _pick_tile(n, pref):
    t = min(n, pref)
    while n % t:
        t //= 2
    return t


def kernel(x, c, ctx, c_ctx, norm_g, w_ada, b_ada, w_in, dw_w, dw_b, ln_g, ln_b, pw_w, pw_b,
           lru_conv_w, lru_conv_b, lru_w_r, lru_b_r, lru_w_i, lru_b_i, lru_lam, attn_sink,
           w_out, final_g):
    nb, seq, d = x.shape
    lc = ctx.shape[1]
    depth = w_in.shape[0]
    ctx_row = nb
    assert nb < ADA_ROWS

    c_rows = jnp.concatenate([c, c_ctx[None, :], jnp.zeros((ADA_ROWS - nb - 1, d), F32)], axis=0)
    ada, (w_bf,) = _ada_call(c_rows, w_ada, b_ada, side_casts=[(w_in, 0)])
    ada4 = ada.reshape(depth, ADA_ROWS, 1, 3 * d)
    rope_tabs = _rope_tables(seq)
    norm_g3 = norm_g.reshape(depth, 1, d)
    vec4 = lambda p: p.reshape(depth, 2, 1, D_LRU)
    per_group = LANES // lru_w_r.shape[-1]
    grp = lambda w: _block_diag(w.reshape(w.shape[:2] + (-1, per_group) + w.shape[-2:]))
    wg_half = (0.5 * jnp.concatenate([grp(lru_w_r), grp(lru_w_i)], axis=-1)).astype(BF16)
    lru_params = (lru_conv_w, vec4(lru_conv_b), wg_half, vec4(lru_b_r), vec4(lru_b_i),
                  vec4(lru_lam))
    vec3 = lambda p: p.reshape(depth, 1, D_CONV)
    wts = {
        "dw_w": jnp.broadcast_to(dw_w[:, :, None, :], (depth, CONV_WIDTH, SUBLANES, D_CONV)),
        "dw_b": vec3(dw_b), "ln_g": vec3(ln_g), "ln_b": vec3(ln_b),
        "pw_w": pw_w.astype(BF16), "pw_b": vec3(pw_b),
        "sink": jnp.broadcast_to(attn_sink[:, :, None], (depth, N_Q_HEADS, HEAD_DIM)),
        "final_g": final_g.reshape(1, d),
    }

    x2 = x.reshape(nb * seq, d)
    xc2 = ctx.reshape(nb * lc, d)
    tm = _pick_tile(seq, 1024)
    tmc = _pick_tile(nb * lc, 1024)
    tq = _pick_tile(seq, 512)
    tt = _pick_tile(seq, 2048)
    tu = _pick_tile(tt, 512)
    zeros_h0 = jnp.zeros((nb, SUBLANES, D_LRU), F32)

    for l in range(depth):
        last = l == depth - 1
        u2, q4, casts = _inproj_call(x2, norm_g3, ada4, l, w_bf, rope_tabs, tm=tm,
                                     rows_per_batch=seq, ada_row_fn=lambda b: b, with_q=True,
                                     side_casts=[(w_out, l)] + ([] if last else [(w_in, l + 1)]))
        uc2, qc4, _ = _inproj_call(xc2, norm_g3, ada4, l, w_bf, None, tm=tmc, rows_per_batch=tmc,
                                   ada_row_fn=lambda b: ctx_row, with_q=not last)
        wts["w_out"] = casts[0]

        ctx_kw = dict(rows_per_batch=lc, tt=lc, units=1)
        lat_kw = dict(rows_per_batch=seq, tt=tt, units=tt // tu)
        hp_c, xp_c, st_f = _lru_call(uc2, zeros_h0, None, lru_params, l, 0, **ctx_kw)
        ylru_c, st_r = _lru_call(None, zeros_h0, (xp_c, hp_c), lru_params, l, 1, **ctx_kw)
        hp, xp, _ = _lru_call(u2, st_f, None, lru_params, l, 0, **lat_kw)
        ylru, _ = _lru_call(None, st_r, (xp, hp), lru_params, l, 1, **lat_kw)

        x2_new, _ = _mix_call(x2, u2, q4, uc2, ylru, ada4, l, lambda b: b, wts, tq=tq,
                              rows_per_batch=seq, ctx_rows=lc, local=True, final=last)
        if not last:
            xc2, _ = _mix_call(xc2, uc2, qc4, uc2, ylru_c, ada4, l, lambda b: ctx_row, wts, tq=lc,
                               rows_per_batch=lc, ctx_rows=lc, local=False, final=False)
            w_bf = casts[1]
        x2 = x2_new
    return x2.reshape(nb, seq, d)
```
